```python
import numpy as np
import jax
import jax.numpy as jnp
from jax import lax

D_MODEL = 1024
BATCH = 8
SEQ = 4096
DEPTH = 2

GRID_W = 64
CTX_LEN = 256
EPS = 1e-6
N_DIR = 2

D_CONV = D_MODEL // 4
D_MLSTM = D_MODEL // 2
D_LRU = D_MODEL // 4
D_MIX = D_CONV + D_MLSTM + D_LRU
CONV_WIDTH = 31
MLSTM_HEADS = 4
MLSTM_HD = D_MLSTM // MLSTM_HEADS
MLSTM_CHUNK = 128
LRU_HEADS = 4
LRU_BW = D_LRU // LRU_HEADS
LRU_CONV_WIDTH = 4
LRU_C = 8.0
SPLITS = (D_CONV, D_CONV, D_MLSTM, D_MLSTM, D_MLSTM, D_MLSTM,
          N_DIR * MLSTM_HEADS, N_DIR * MLSTM_HEADS, D_LRU, D_LRU)
IN_COLS = sum(SPLITS)
N_GROUPS = 4
EXPERTS_PER_GROUP = 4
N_EXPERTS = N_GROUPS * EXPERTS_PER_GROUP
TOP_K = 2
D_EXPERT = 512

kernel_name = 'hybrid_conv_mlstm_rglru_hmoe_dit'


def _standardize(x):
    mu = jnp.mean(x, axis=-1, keepdims=True)
    xc = x - mu
    return xc * lax.rsqrt(jnp.mean(xc * xc, axis=-1, keepdims=True) + EPS)


def rms_norm(x, g):
    xf = x.astype(jnp.float32)
    y = xf * lax.rsqrt(jnp.mean(xf * xf, axis=-1, keepdims=True) + EPS)
    return (y * g.astype(jnp.float32)).astype(x.dtype)


def layer_norm(x, g, b):
    y = _standardize(x.astype(jnp.float32)) * g.astype(jnp.float32) + b.astype(jnp.float32)
    return y.astype(x.dtype)


def _rev(a, d, axis):
    return jnp.flip(a, axis=axis) if d == 1 else a


def to_scan_order(a, rows, layer):
    if layer % 2 == 0:
        return a
    b, t, d = a.shape
    return a.reshape(b, rows, GRID_W, d).transpose(0, 2, 1, 3).reshape(b, t, d)


def from_scan_order(a, rows, layer):
    if layer % 2 == 0:
        return a
    b, t, d = a.shape
    return a.reshape(b, GRID_W, rows, d).transpose(0, 2, 1, 3).reshape(b, t, d)


def depthwise_conv(x, w, b, pad):
    y = lax.conv_general_dilated(x, w[:, None, :].astype(x.dtype), (1,), [pad],
                                 dimension_numbers=('NWC', 'WIO', 'NWC'),
                                 feature_group_count=x.shape[-1])
    return y + b.astype(x.dtype)


def conformer_conv(val, gate, w, b, ln_g, ln_b):
    u = val * jax.nn.sigmoid(gate)
    u = depthwise_conv(u, w, b, (CONV_WIDTH // 2, CONV_WIDTH // 2))
    return jax.nn.silu(layer_norm(u, ln_g, ln_b))


def mlstm_chunkwise(q, k, v, ig, fg, state):
    bsz, nh, t, dh = q.shape
    nc = t // MLSTM_CHUNK

    def chunks(a):
        return jnp.moveaxis(a.reshape(bsz, nh, nc, MLSTM_CHUNK, *a.shape[3:]), 2, 0)

    logf = jax.nn.log_sigmoid(fg)
    k = k * (dh ** -0.5)
    mask = jnp.tril(jnp.ones((MLSTM_CHUNK, MLSTM_CHUNK), dtype=bool))

    def step(carry, inp):
        cmat, nvec, m = carry
        qc, kc, vc, ic, lf = inp
        b = jnp.cumsum(lf, axis=-1)
        log_d = jnp.where(mask, b[..., :, None] - b[..., None, :] + ic[..., None, :], -jnp.inf)
        m_inter = b + m[..., None]
        m_t = jnp.maximum(m_inter, jnp.max(log_d, axis=-1))
        dmat = jnp.exp(log_d - m_t[..., None])
        inter = jnp.exp(m_inter - m_t)
        s = jnp.einsum('bhtd,bhsd->bhts', qc, kc) * dmat
        num = jnp.einsum('bhts,bhsd->bhtd', s, vc) + inter[..., None] * jnp.einsum('bhtk,bhkv->bhtv', qc, cmat)
        den = jnp.sum(s, axis=-1) + inter * jnp.einsum('bhtk,bhk->bht', qc, nvec)
        h = num / jnp.maximum(jnp.abs(den), jnp.exp(-m_t))[..., None]
        b_last = b[..., -1]
        log_w = b_last[..., None] - b + ic
        m_new = jnp.maximum(b_last + m, jnp.max(log_w, axis=-1))
        wgt = jnp.exp(log_w - m_new[..., None])
        decay = jnp.exp(b_last + m - m_new)
        cmat = decay[..., None, None] * cmat + jnp.einsum('bhs,bhsk,bhsv->bhkv', wgt, kc, vc)
        nvec = decay[..., None] * nvec + jnp.einsum('bhs,bhsk->bhk', wgt, kc)
        return (cmat, nvec, m_new), h

    state, h = lax.scan(step, state, (chunks(q), chunks(k), chunks(v), chunks(ig), chunks(logf)))
    return jnp.moveaxis(h, 0, 2).reshape(bsz, nh, t, dh), state


def mlstm_group(z_ctx, z_lat, b_i, b_f, norm_g):
    def heads(a):
        b, t, _ = a.shape
        return a.astype(jnp.float32).reshape(b, t, MLSTM_HEADS, MLSTM_HD).transpose(0, 2, 1, 3)

    def gate(a, d, bias):
        b, t, _ = a.shape
        g = a.astype(jnp.float32).reshape(b, t, N_DIR, MLSTM_HEADS)[:, :, d] + bias[d]
        return g.transpose(0, 2, 1)

    def finish(h, o):
        b, _, t, _ = h.shape
        hn = _standardize(h) * norm_g.astype(jnp.float32).reshape(MLSTM_HEADS, 1, MLSTM_HD)
        hn = hn.transpose(0, 2, 1, 3).reshape(b, t, D_MLSTM)
        return (jax.nn.sigmoid(o.astype(jnp.float32)) * hn).astype(o.dtype)

    qc, kc, vc = (heads(a) for a in z_ctx[:3])
    ql, kl, vl = (heads(a) for a in z_lat[:3])
    bsz = qc.shape[0]
    h_ctx = jnp.zeros_like(qc)
    h_lat = jnp.zeros_like(ql)
    for d in range(N_DIR):
        state0 = (jnp.zeros((bsz, MLSTM_HEADS, MLSTM_HD, MLSTM_HD), jnp.float32),
                  jnp.zeros((bsz, MLSTM_HEADS, MLSTM_HD), jnp.float32),
                  jnp.zeros((bsz, MLSTM_HEADS), jnp.float32))
        out_c, state_c = mlstm_chunkwise(_rev(qc, d, 2), _rev(kc, d, 2), _rev(vc, d, 2),
                                         _rev(gate(z_ctx[4], d, b_i), d, 2),
                                         _rev(gate(z_ctx[5], d, b_f), d, 2), state0)
        out_l, _ = mlstm_chunkwise(_rev(ql, d, 2), _rev(kl, d, 2), _rev(vl, d, 2),
                                   _rev(gate(z_lat[4], d, b_i), d, 2),
                                   _rev(gate(z_lat[5], d, b_f), d, 2), state_c)
        h_ctx = h_ctx + _rev(out_c, d, 2)
        h_lat = h_lat + _rev(out_l, d, 2)
    return finish(h_ctx, z_ctx[3]), finish(h_lat, z_lat[3])


def rglru_scan(x, conv_w, conv_b, w_a, b_a, w_x, b_x, lam, h0):
    xc = depthwise_conv(x, conv_w, conv_b, (LRU_CONV_WIDTH - 1, 0)).astype(jnp.float32)
    bsz, t, _ = xc.shape
    xb = xc.reshape(bsz, t, LRU_HEADS, LRU_BW)
    r = jax.nn.sigmoid(jnp.einsum('btnc,ncd->btnd', xb, w_a.astype(jnp.float32)).reshape(bsz, t, D_LRU) + b_a)
    i = jax.nn.sigmoid(jnp.einsum('btnc,ncd->btnd', xb, w_x.astype(jnp.float32)).reshape(bsz, t, D_LRU) + b_x)
    log_a = -LRU_C * r * jax.nn.softplus(-lam.astype(jnp.float32))
    a = jnp.exp(log_a)
    u = jnp.sqrt(-jnp.expm1(2.0 * log_a)) * (i * xc)

    def combine(left, right):
        a1, b1 = left
        a2, b2 = right
        return a1 * a2, a2 * b1 + b2

    a_cum, h = lax.associative_scan(combine, (a, u), axis=1)
    h = h + a_cum * h0[:, None, :]
    return h, h[:, -1]


def rglru_group(x_ctx, g_ctx, x_lat, g_lat, conv_w, conv_b, w_a, b_a, w_x, b_x, lam):
    bsz = x_ctx.shape[0]
    h_ctx = jnp.zeros(x_ctx.shape, jnp.float32)
    h_lat = jnp.zeros(x_lat.shape, jnp.float32)
    for d in range(N_DIR):
        p = (conv_w[d], conv_b[d], w_a[d], b_a[d], w_x[d], b_x[d], lam[d])
        h0 = jnp.zeros((bsz, D_LRU), jnp.float32)
        out_c, state_c = rglru_scan(_rev(x_ctx, d, 1), *p, h0)
        out_l, _ = rglru_scan(_rev(x_lat, d, 1), *p, state_c)
        h_ctx = h_ctx + _rev(out_c, d, 1)
        h_lat = h_lat + _rev(out_l, d, 1)
    y_ctx = h_ctx * jax.nn.gelu(g_ctx.astype(jnp.float32))
    y_lat = h_lat * jax.nn.gelu(g_lat.astype(jnp.float32))
    return y_ctx.astype(x_ctx.dtype), y_lat.astype(x_lat.dtype)


def hier_moe(h, w_rg, b_rg, w_re, b_re, w_gate, w_up, w_down):
    shp = h.shape
    hf = h.reshape(-1, shp[-1])
    n = hf.shape[0]
    g_logit = (hf @ w_rg).astype(jnp.float32) + b_rg
    g_prob = jax.nn.softmax(g_logit, axis=-1)
    g_sel = jnp.argmax(g_logit, axis=-1)
    e_logit = ((hf @ w_re).astype(jnp.float32) + b_re).reshape(n, N_GROUPS, EXPERTS_PER_GROUP)
    e_logit = jnp.take_along_axis(e_logit, g_sel[:, None, None], axis=1)[:, 0]
    top_v, top_i = lax.top_k(e_logit, TOP_K)
    top_w = jax.nn.softmax(top_v, axis=-1) * jnp.take_along_axis(g_prob, g_sel[:, None], axis=1)
    expert_id = g_sel[:, None] * EXPERTS_PER_GROUP + top_i
    combine = jnp.sum(jax.nn.one_hot(expert_id, N_EXPERTS, dtype=jnp.float32) * top_w[..., None], axis=1)
    y = jnp.zeros(hf.shape, jnp.float32)
    for e in range(N_EXPERTS):
        ye = (jax.nn.silu(hf @ w_gate[e]) * (hf @ w_up[e])) @ w_down[e]
        y = y + combine[:, e:e + 1] * ye
    return y.astype(h.dtype).reshape(shp)


def setup_inputs(seed: int = 0) -> dict:
    key = jax.random.key(seed)
    ks = iter(jax.random.split(key, 48))

    def nrm(shape, s):
        return jax.random.normal(next(ks), shape, jnp.float32) * s

    L = DEPTH
    u = jax.random.uniform(next(ks), (L, N_DIR, D_LRU), jnp.float32, 0.9, 0.999)
    return {
        'x': nrm((BATCH, SEQ, D_MODEL), 1.0),
        'c': nrm((BATCH, D_MODEL), 1.0),
        'ctx': nrm((BATCH, CTX_LEN, D_MODEL), 1.0),
        'c_ctx': nrm((D_MODEL,), 1.0),
        'w_mod': nrm((L, D_MODEL, 6 * D_MODEL), 0.5 * D_MODEL ** -0.5),
        'b_mod': nrm((L, 6 * D_MODEL), 0.02),
        'norm1_g': 1.0 + nrm((L, D_MODEL), 0.02),
        'norm2_g': 1.0 + nrm((L, D_MODEL), 0.02),
        'w_in': nrm((L, D_MODEL, IN_COLS), D_MODEL ** -0.5),
        'conv_w': nrm((L, CONV_WIDTH, D_CONV), CONV_WIDTH ** -0.5),
        'conv_b': nrm((L, D_CONV), 0.02),
        'conv_ln_g': 1.0 + nrm((L, D_CONV), 0.02),
        'conv_ln_b': nrm((L, D_CONV), 0.02),
        'mlstm_b_i': nrm((L, N_DIR, MLSTM_HEADS), 0.1),
        'mlstm_b_f': jnp.linspace(3.0, 6.0, MLSTM_HEADS, dtype=jnp.float32) + nrm((L, N_DIR, MLSTM_HEADS), 0.1),
        'mlstm_norm_g': 1.0 + nrm((L, D_MLSTM), 0.02),
        'lru_conv_w': nrm((L, N_DIR, LRU_CONV_WIDTH, D_LRU), LRU_CONV_WIDTH ** -0.5),
        'lru_conv_b': nrm((L, N_DIR, D_LRU), 0.02),
        'lru_w_a': nrm((L, N_DIR, LRU_HEADS, LRU_BW, LRU_BW), LRU_BW ** -0.5),
        'lru_b_a': nrm((L, N_DIR, D_LRU), 0.02),
        'lru_w_x': nrm((L, N_DIR, LRU_HEADS, LRU_BW, LRU_BW), LRU_BW ** -0.5),
        'lru_b_x': nrm((L, N_DIR, D_LRU), 0.02),
        'lru_lambda': jnp.log(u) - jnp.log1p(-u),
        'w_out': nrm((L, D_MIX, D_MODEL), D_MIX ** -0.5),
        'w_rg': nrm((L, D_MODEL, N_GROUPS), D_MODEL ** -0.5),
        'b_rg': nrm((L, N_GROUPS), 0.01),
        'w_re': nrm((L, D_MODEL, N_EXPERTS), D_MODEL ** -0.5),
        'b_re': nrm((L, N_EXPERTS), 0.01),
        'w_gate': nrm((L, N_EXPERTS, D_MODEL, D_EXPERT), D_MODEL ** -0.5),
        'w_up': nrm((L, N_EXPERTS, D_MODEL, D_EXPERT), D_MODEL ** -0.5),
        'w_down': nrm((L, N_EXPERTS, D_EXPERT, D_MODEL), D_EXPERT ** -0.5),
        'final_g': 1.0 + nrm((D_MODEL,), 0.02),
    }


def reference(x, c, ctx, c_ctx, w_mod, b_mod, norm1_g, norm2_g, w_in, conv_w, conv_b, conv_ln_g,
              conv_ln_b, mlstm_b_i, mlstm_b_f, mlstm_norm_g, lru_conv_w, lru_conv_b, lru_w_a, lru_b_a,
              lru_w_x, lru_b_x, lru_lambda, w_out, w_rg, b_rg, w_re, b_re, w_gate, w_up, w_down, final_g):
    rows = x.shape[1] // GRID_W
    split_at = [int(s) for s in np.cumsum(SPLITS)[:-1]]
    h_lat, h_ctx = x, ctx
    for l in range(DEPTH):
        last = l == DEPTH - 1
        mod_lat = jnp.split((jax.nn.silu(c) @ w_mod[l] + b_mod[l])[:, None, :], 6, axis=-1)
        mod_ctx = jnp.split(jax.nn.silu(c_ctx) @ w_mod[l] + b_mod[l], 6, axis=-1)
        u_lat = to_scan_order(rms_norm(h_lat, norm1_g[l]) * (1 + mod_lat[1]) + mod_lat[0], rows, l)
        u_ctx = rms_norm(h_ctx, norm1_g[l]) * (1 + mod_ctx[1]) + mod_ctx[0]
        z_lat = jnp.split(u_lat @ w_in[l], split_at, axis=-1)
        z_ctx = jnp.split(u_ctx @ w_in[l], split_at, axis=-1)
        conv_p = (conv_w[l], conv_b[l], conv_ln_g[l], conv_ln_b[l])
        a_lat = conformer_conv(z_lat[0], z_lat[1], *conv_p)
        m_ctx, m_lat = mlstm_group(z_ctx[2:8], z_lat[2:8], mlstm_b_i[l], mlstm_b_f[l], mlstm_norm_g[l])
        r_ctx, r_lat = rglru_group(z_ctx[8], z_ctx[9], z_lat[8], z_lat[9], lru_conv_w[l], lru_conv_b[l],
                                   lru_w_a[l], lru_b_a[l], lru_w_x[l], lru_b_x[l], lru_lambda[l])
        y_lat = from_scan_order(jnp.concatenate([a_lat, m_lat, r_lat], axis=-1) @ w_out[l], rows, l)
        h_lat = h_lat + mod_lat[2] * y_lat
        moe_p = (w_rg[l], b_rg[l], w_re[l], b_re[l], w_gate[l], w_up[l], w_down[l])
        v_lat = rms_norm(h_lat, norm2_g[l]) * (1 + mod_lat[4]) + mod_lat[3]
        h_lat = h_lat + mod_lat[5] * hier_moe(v_lat, *moe_p)
        if not last:
            a_ctx = conformer_conv(z_ctx[0], z_ctx[1], *conv_p)
            y_ctx = jnp.concatenate([a_ctx, m_ctx, r_ctx], axis=-1) @ w_out[l]
            h_ctx = h_ctx + mod_ctx[2] * y_ctx
            v_ctx = rms_norm(h_ctx, norm2_g[l]) * (1 + mod_ctx[4]) + mod_ctx[3]
            h_ctx = h_ctx + mod_ctx[5] * hier_moe(v_ctx, *moe_p)
    return rms_norm(h_lat, final_g)
```

```python
import functools

import jax
import jax.numpy as jnp
from jax import lax
from jax.experimental import pallas as pl
from jax.experimental.pallas import tpu as pltpu

EPS = 1e-6
GRID_W = 64
D_CONV_FRAC = 4
CONV_WIDTH = 31
CONV_HALO = 16
MLSTM_HEADS = 4
MLSTM_CHUNK = 128
LRU_HEADS = 4
LRU_CONV_WIDTH = 4
LRU_C = 8.0
N_DIR = 2
N_GROUPS = 4
EXPERTS_PER_GROUP = 4
N_EXPERTS = N_GROUPS * EXPERTS_PER_GROUP
ROUTER_LANES = 128
GATE_LANES = 128
SUBLANES = 8

VMEM_LIMIT = 56 * 1024 * 1024
TOKEN_TILE = 512
SCAN_BLOCK = 256
MOE_TILE = 512

F32 = jnp.float32
BF16 = jnp.bfloat16


def _cparams(sem):
    return pltpu.CompilerParams(dimension_semantics=sem, vmem_limit_bytes=VMEM_LIMIT)


def _sigmoid(x):
    return jax.nn.sigmoid(x)


def _log_sigmoid(x):
    return jnp.minimum(x, 0.0) - jnp.log1p(jnp.exp(-jnp.abs(x)))


def _softplus(x):
    return jnp.maximum(x, 0.0) + jnp.log1p(jnp.exp(-jnp.abs(x)))


def _gelu_tanh(x):
    return x * (0.5 * (1.0 + jnp.tanh(0.7978845608028654 * (x + 0.044715 * (x * x * x)))))


def _dot(a, b):
    return jnp.dot(a, b, preferred_element_type=F32)


def _split_hi_lo(x):
    hi = x.astype(BF16)
    lo = (x - hi.astype(F32)).astype(BF16)
    return hi, lo


def _load_tile(ref, transposed, kb, d):
    if not transposed:
        return ref[0]
    return jnp.concatenate([ref[0, :, w * d:(w + 1) * d] for w in range(kb)], axis=0)


def _store_tile(ref, val, transposed, kb, d):
    if not transposed:
        ref[0] = val.astype(ref.dtype)
        return
    rows = val.shape[0] // kb
    for w in range(kb):
        ref[0, :, w * d:(w + 1) * d] = val[w * rows:(w + 1) * rows].astype(ref.dtype)


def _tile_spec(t, tm, d, transposed):
    if not transposed:
        return pl.BlockSpec((1, tm, d), lambda b, j: (b, j, 0))
    rows = t // GRID_W
    kb = tm // rows
    return pl.BlockSpec((1, rows, kb * d), lambda b, j: (b, 0, j))


def _tile_view(a, transposed):
    if not transposed:
        return a
    b, t, d = a.shape
    return a.reshape(b, t // GRID_W, GRID_W * d)


def _tile_unview(a, t, transposed):
    if not transposed:
        return a
    b = a.shape[0]
    return a.reshape(b, t, -1)


def _mod_kernel(c_ref, w_ref, b_ref, o_ref):
    c = c_ref[...]
    s = (c * _sigmoid(c)).astype(BF16)
    o_ref[0] = _dot(s, w_ref[0].astype(BF16)) + b_ref[0]


def _modulation(cvec, w_mod, b_mod):
    nl, d, d6 = w_mod.shape
    rp = cvec.shape[0]
    tn = d6 // 4
    return pl.pallas_call(
        _mod_kernel,
        grid=(nl, d6 // tn),
        in_specs=[pl.BlockSpec((rp, d), lambda l, j: (0, 0)),
                  pl.BlockSpec((1, d, tn), lambda l, j: (l, 0, j)),
                  pl.BlockSpec((1, 1, tn), lambda l, j: (l, 0, j))],
        out_specs=pl.BlockSpec((1, rp, tn), lambda l, j: (l, 0, j)),
        out_shape=jax.ShapeDtypeStruct((nl, rp, d6), F32),
        compiler_params=_cparams(("arbitrary", "arbitrary")),
        name="modulation",
    )(cvec, w_mod, b_mod.reshape(nl, 1, d6))


def _in_kernel(*refs, transposed, has_prev, kb, d, splits):
    if has_prev:
        h_ref, y_ref, g2_ref, shift_ref, scale_ref, g_ref, w_ref = refs[:7]
        hcur_ref = refs[7]
        outs = refs[8:]
    else:
        h_ref, shift_ref, scale_ref, g_ref, w_ref = refs[:5]
        outs = refs[5:]
    h = _load_tile(h_ref, transposed, kb, d)
    if has_prev:
        h = h + g2_ref[0] * _load_tile(y_ref, transposed, kb, d)
        _store_tile(hcur_ref, h, transposed, kb, d)
    ms = jnp.mean(h * h, axis=-1, keepdims=True)
    u = (h * lax.rsqrt(ms + EPS)) * g_ref[...]
    u = (u * (1.0 + scale_ref[0]) + shift_ref[0]).astype(BF16)
    c0 = 0
    for o_ref, width in zip(outs, splits):
        o_ref[0] = _dot(u, w_ref[:, c0:c0 + width]).astype(o_ref.dtype)
        c0 += width


def _in_proj(h, shift, scale, norm_g, w, splits, out_dtypes, tm, transposed, prev=None):
    b, t, d = h.shape
    per_batch = shift.shape[0] == b
    mod_spec = pl.BlockSpec((1, 1, d), (lambda bi, j: (bi, 0, 0)) if per_batch else (lambda bi, j: (0, 0, 0)))
    kb = tm // (t // GRID_W) if transposed else 1
    tok = _tile_spec(t, tm, d, transposed)
    has_prev = prev is not None
    ins, in_specs = [_tile_view(h, transposed)], [tok]
    if has_prev:
        y_prev, g2_prev = prev
        ins += [_tile_view(y_prev, transposed), g2_prev]
        in_specs += [tok, mod_spec]
    ins += [shift, scale, norm_g.reshape(1, d), w]
    in_specs += [mod_spec, mod_spec, pl.BlockSpec((1, d), lambda bi, j: (0, 0)),
                 pl.BlockSpec(w.shape, lambda bi, j: (0, 0))]
    out_shape, out_specs = [], []
    if has_prev:
        out_shape.append(jax.ShapeDtypeStruct(ins[0].shape, F32))
        out_specs.append(tok)
    for width, dt in zip(splits, out_dtypes):
        out_shape.append(jax.ShapeDtypeStruct((b, t, width), dt))
        out_specs.append(pl.BlockSpec((1, tm, width), lambda bi, j: (bi, j, 0)))
    res = pl.pallas_call(
        functools.partial(_in_kernel, transposed=transposed, has_prev=has_prev, kb=kb, d=d, splits=splits),
        grid=(b, t // tm),
        in_specs=in_specs, out_specs=out_specs, out_shape=out_shape,
        compiler_params=_cparams(("arbitrary", "arbitrary")),
        name="in_proj",
    )(*ins)
    if has_prev:
        return _tile_unview(res[0], t, transposed), res[1:]
    return None, res


def _conv_kernel(prev_ref, cur_ref, next_ref, w_ref, b_ref, lng_ref, lnb_ref, o_ref, *, tb, nblk, dc):
    j = pl.program_id(1)

    def glu(x):
        return x[:, :dc] * _sigmoid(x[:, dc:])

    up = jnp.where(j > 0, glu(prev_ref[0]), 0.0)
    un = jnp.where(j < nblk - 1, glu(next_ref[0]), 0.0)
    ext = jnp.concatenate([up, glu(cur_ref[0]), un], axis=0)
    base = CONV_HALO - CONV_WIDTH // 2
    acc = jnp.zeros((tb, dc), F32)
    for k in range(CONV_WIDTH):
        acc = acc + w_ref[k:k + 1, :] * ext[base + k:base + k + tb, :]
    acc = acc + b_ref[...]
    mu = jnp.mean(acc, axis=-1, keepdims=True)
    xc = acc - mu
    y = xc * lax.rsqrt(jnp.mean(xc * xc, axis=-1, keepdims=True) + EPS) * lng_ref[...] + lnb_ref[...]
    o_ref[0] = (y * _sigmoid(y)).astype(o_ref.dtype)


def _conformer_conv(cv, w, b, ln_g, ln_b, tb):
    bsz, t, c2 = cv.shape
    dc = c2 // 2
    nblk = t // tb
    hb = tb // CONV_HALO
    nh = t // CONV_HALO
    wp = jnp.zeros((CONV_WIDTH + 1, dc), F32).at[:CONV_WIDTH].set(w)
    vec = pl.BlockSpec((1, dc), lambda bi, j: (0, 0))
    return pl.pallas_call(
        functools.partial(_conv_kernel, tb=tb, nblk=nblk, dc=dc),
        grid=(bsz, nblk),
        in_specs=[pl.BlockSpec((1, CONV_HALO, c2), lambda bi, j: (bi, jnp.maximum(j * hb - 1, 0), 0)),
                  pl.BlockSpec((1, tb, c2), lambda bi, j: (bi, j, 0)),
                  pl.BlockSpec((1, CONV_HALO, c2), lambda bi, j: (bi, jnp.minimum((j + 1) * hb, nh - 1), 0)),
                  pl.BlockSpec((CONV_WIDTH + 1, dc), lambda bi, j: (0, 0)), vec, vec, vec],
        out_specs=pl.BlockSpec((1, tb, dc), lambda bi, j: (bi, j, 0)),
        out_shape=jax.ShapeDtypeStruct((bsz, t, dc), BF16),
        compiler_params=_cparams(("arbitrary", "arbitrary")),
        name="conformer_conv",
    )(cv, cv, cv, wp, b.reshape(1, dc), ln_g.reshape(1, dc), ln_b.reshape(1, dc))


def _mlstm_block(q_ref, g_ref, bias_ref, out_ref, c_ref, n_ref, m_ref, d, nchunks, hd):
    L = MLSTM_CHUNK
    nh = MLSTM_HEADS
    dm = nh * hd
    scale = hd ** -0.5
    row = lax.broadcasted_iota(jnp.int32, (L, L), 0)
    col = lax.broadcasted_iota(jnp.int32, (L, L), 1)
    if d == 0:
        mask = col <= row
        tri_cols = jnp.where(col <= row, 1.0, 0.0).astype(BF16)
        tri_rows = jnp.where(row <= col, 1.0, 0.0).astype(BF16)
        order = range(nchunks)
    else:
        mask = col >= row
        tri_cols = jnp.where(col >= row, 1.0, 0.0).astype(BF16)
        tri_rows = jnp.where(row >= col, 1.0, 0.0).astype(BF16)
        order = range(nchunks - 1, -1, -1)
    for ci in order:
        r0 = ci * L
        g = g_ref[0, r0:r0 + L, :] + bias_ref[...]
        gt = g.T[0:2 * N_DIR * nh, :]
        hi, lo = _split_hi_lo(_log_sigmoid(g))
        cum_c = _dot(tri_cols, hi) + _dot(tri_cols, lo)
        hi, lo = _split_hi_lo(_log_sigmoid(gt))
        cum_r = _dot(hi, tri_rows) + _dot(lo, tri_rows)
        for h in range(nh):
            ic = d * nh + h
            fc = N_DIR * nh + d * nh + h
            idx = d * nh + h
            b_col = cum_c[:, fc:fc + 1]
            b_row = cum_r[fc:fc + 1, :]
            ig_col = g[:, ic:ic + 1]
            ig_row = gt[ic:ic + 1, :]
            m = m_ref[idx]
            log_d = jnp.where(mask, b_col - b_row + ig_row, -jnp.inf)
            m_inter = b_col + m
            m_t = jnp.maximum(m_inter, jnp.max(log_d, axis=-1, keepdims=True))
            dmat = jnp.exp(log_d - m_t)
            inter = jnp.exp(m_inter - m_t)
            q = q_ref[0, r0:r0 + L, h * hd:(h + 1) * hd]
            k = q_ref[0, r0:r0 + L, dm + h * hd:dm + (h + 1) * hd]
            v = q_ref[0, r0:r0 + L, 2 * dm + h * hd:2 * dm + (h + 1) * hd]
            s = lax.dot_general(q, k, (((1,), (1,)), ((), ())), preferred_element_type=F32) * scale * dmat
            cmat = c_ref[idx]
            nvec = n_ref[idx]
            num = _dot(s.astype(BF16), v) + inter * _dot(q, cmat.astype(BF16))
            den = jnp.sum(s, axis=-1, keepdims=True) + inter * jnp.sum(q.astype(F32) * nvec, axis=-1, keepdims=True)
            out_ref[0, r0:r0 + L, h * hd:(h + 1) * hd] = num / jnp.maximum(jnp.abs(den), jnp.exp(-m_t))
            b_last = b_col[L - 1:L, :] if d == 0 else b_col[0:1, :]
            log_w = b_last - b_col + ig_col
            m_new = jnp.maximum(b_last + m, jnp.max(log_w, axis=0, keepdims=True))
            wgt = jnp.exp(log_w - m_new)
            decay = jnp.exp(b_last + m - m_new)
            kw = k.astype(F32) * (wgt * scale)
            c_ref[idx] = decay * cmat + lax.dot_general(kw.astype(BF16), v, (((0,), (0,)), ((), ())),
                                                        preferred_element_type=F32)
            n_ref[idx] = decay * nvec + jnp.sum(kw, axis=0, keepdims=True)
            m_ref[idx] = m_new


def _mlstm_kernel(qc_ref, gc_ref, qf_ref, gf_ref, qb_ref, gb_ref, bias_ref,
                  hcf_ref, hcb_ref, hf_ref, hb_ref, c_ref, n_ref, m_ref, *, nc_ctx, nc_lat, hd):
    s = pl.program_id(1)
    state = (c_ref, n_ref, m_ref)

    @pl.when(s == 0)
    def _():
        c_ref[...] = jnp.zeros_like(c_ref)
        n_ref[...] = jnp.zeros_like(n_ref)
        m_ref[...] = jnp.zeros_like(m_ref)
        _mlstm_block(qc_ref, gc_ref, bias_ref, hcf_ref, *state, 0, nc_ctx, hd)
        _mlstm_block(qc_ref, gc_ref, bias_ref, hcb_ref, *state, 1, nc_ctx, hd)

    @pl.when(s > 0)
    def _():
        _mlstm_block(qf_ref, gf_ref, bias_ref, hf_ref, *state, 0, nc_lat, hd)
        _mlstm_block(qb_ref, gb_ref, bias_ref, hb_ref, *state, 1, nc_lat, hd)


def _mlstm(qkv_c, gt_c, qkv_l, gt_l, bias, tb):
    bsz, tc, w3 = qkv_c.shape
    t = qkv_l.shape[1]
    dm = w3 // 3
    hd = dm // MLSTM_HEADS
    nb = t // tb
    nstate = N_DIR * MLSTM_HEADS

    def fwd(bi, s):
        return (bi, jnp.maximum(s - 1, 0), 0)

    def bwd(bi, s):
        return (bi, nb - 1 - jnp.maximum(s - 1, 0), 0)

    def ctx(bi, s):
        return (bi, 0, 0)

    return pl.pallas_call(
        functools.partial(_mlstm_kernel, nc_ctx=tc // MLSTM_CHUNK, nc_lat=tb // MLSTM_CHUNK, hd=hd),
        grid=(bsz, nb + 1),
        in_specs=[pl.BlockSpec((1, tc, w3), ctx), pl.BlockSpec((1, tc, GATE_LANES), ctx),
                  pl.BlockSpec((1, tb, w3), fwd), pl.BlockSpec((1, tb, GATE_LANES), fwd),
                  pl.BlockSpec((1, tb, w3), bwd), pl.BlockSpec((1, tb, GATE_LANES), bwd),
                  pl.BlockSpec((1, GATE_LANES), lambda bi, s: (0, 0))],
        out_specs=[pl.BlockSpec((1, tc, dm), ctx), pl.BlockSpec((1, tc, dm), ctx),
                   pl.BlockSpec((1, tb, dm), fwd), pl.BlockSpec((1, tb, dm), bwd)],
        out_shape=[jax.ShapeDtypeStruct((bsz, tc, dm), F32), jax.ShapeDtypeStruct((bsz, tc, dm), F32),
                   jax.ShapeDtypeStruct((bsz, t, dm), F32), jax.ShapeDtypeStruct((bsz, t, dm), F32)],
        scratch_shapes=[pltpu.VMEM((nstate, hd, hd), F32), pltpu.VMEM((nstate, 1, hd), F32),
                        pltpu.VMEM((nstate, 1, 1), F32)],
        compiler_params=_cparams(("arbitrary", "arbitrary")),
        name="mlstm",
    )(qkv_c, gt_c, qkv_l, gt_l, qkv_l, gt_l, bias)


def _lru_block(x, halo, d, cw, cb, wax, bax, lam, h0):
    tb, dl = x.shape
    kw = LRU_CONV_WIDTH
    if d == 0:
        ext = jnp.concatenate([halo, x], axis=0)
        taps = [ext[SUBLANES - (kw - 1) + j:SUBLANES - (kw - 1) + j + tb] for j in range(kw)]
    else:
        ext = jnp.concatenate([x, halo], axis=0)
        taps = [ext[kw - 1 - j:kw - 1 - j + tb] for j in range(kw)]
    xc = cb
    for j in range(kw):
        xc = xc + cw[j:j + 1, :] * taps[j]
    ri = _sigmoid(_dot(xc.astype(BF16), wax) + bax)
    r, i = ri[:, :dl], ri[:, dl:]
    log_a = (-LRU_C * r) * _softplus(-lam)
    a = jnp.exp(log_a)
    th = jnp.tanh(log_a)
    u = jnp.sqrt(-2.0 * th / (1.0 - th)) * (i * xc)
    sub = lax.broadcasted_iota(jnp.int32, (tb, dl), 0) & (SUBLANES - 1)
    for sh in (1, 2, 4):
        if d == 0:
            a_s, u_s, msk = pltpu.roll(a, sh, 0), pltpu.roll(u, sh, 0), sub >= sh
        else:
            a_s, u_s, msk = pltpu.roll(a, tb - sh, 0), pltpu.roll(u, tb - sh, 0), sub < SUBLANES - sh
        u = jnp.where(msk, a * u_s + u, u)
        a = jnp.where(msk, a * a_s, a)
    ngrp = tb // SUBLANES
    outs = [None] * ngrp
    carry = h0
    for j in (range(ngrp) if d == 0 else range(ngrp - 1, -1, -1)):
        hj = u[j * SUBLANES:(j + 1) * SUBLANES] + a[j * SUBLANES:(j + 1) * SUBLANES] * carry
        outs[j] = hj
        carry = hj[SUBLANES - 1:SUBLANES] if d == 0 else hj[0:1]
    return jnp.concatenate(outs, axis=0), carry


def _lru_kernel(xc_ref, xf_ref, xb_ref, cw_ref, cb_ref, wax_ref, bax_ref, lam_ref,
                hcf_ref, hcb_ref, hf_ref, hb_ref, hcar_ref, halo_ref):
    s = pl.program_id(1)

    def params(d):
        return cw_ref[d], cb_ref[d], wax_ref[d], bax_ref[d], lam_ref[d]

    @pl.when(s == 0)
    def _():
        x = xc_ref[0]
        zero_halo = jnp.zeros((SUBLANES, x.shape[1]), F32)
        zero_h = jnp.zeros((1, x.shape[1]), F32)
        for d, o_ref in ((0, hcf_ref), (1, hcb_ref)):
            h, carry = _lru_block(x, zero_halo, d, *params(d), zero_h)
            o_ref[0] = h
            hcar_ref[d] = jnp.broadcast_to(carry, hcar_ref.shape[1:])

    @pl.when(s > 0)
    def _():
        for d, x_ref, o_ref in ((0, xf_ref, hf_ref), (1, xb_ref, hb_ref)):
            x = x_ref[0]
            halo = jnp.where(s > 1, halo_ref[d], 0.0)
            h, carry = _lru_block(x, halo, d, *params(d), hcar_ref[d][0:1])
            o_ref[0] = h
            hcar_ref[d] = jnp.broadcast_to(carry, hcar_ref.shape[1:])
            halo_ref[d] = x[x.shape[0] - SUBLANES:] if d == 0 else x[:SUBLANES]


def _rglru(lru_c, lru_l, cw, cb, wax, bax, lam, tb):
    bsz, tc, w2 = lru_c.shape
    t = lru_l.shape[1]
    dl = w2 // 2
    nb = t // tb

    def fwd(bi, s):
        return (bi, jnp.maximum(s - 1, 0), 0)

    def bwd(bi, s):
        return (bi, nb - 1 - jnp.maximum(s - 1, 0), 0)

    def ctx(bi, s):
        return (bi, 0, 0)

    def whole(a):
        return pl.BlockSpec(a.shape, lambda bi, s: (0,) * a.ndim)

    return pl.pallas_call(
        _lru_kernel,
        grid=(bsz, nb + 1),
        in_specs=[pl.BlockSpec((1, tc, dl), ctx), pl.BlockSpec((1, tb, dl), fwd), pl.BlockSpec((1, tb, dl), bwd),
                  whole(cw), whole(cb), whole(wax), whole(bax), whole(lam)],
        out_specs=[pl.BlockSpec((1, tc, dl), ctx), pl.BlockSpec((1, tc, dl), ctx),
                   pl.BlockSpec((1, tb, dl), fwd), pl.BlockSpec((1, tb, dl), bwd)],
        out_shape=[jax.ShapeDtypeStruct((bsz, tc, dl), F32), jax.ShapeDtypeStruct((bsz, tc, dl), F32),
                   jax.ShapeDtypeStruct((bsz, t, dl), F32), jax.ShapeDtypeStruct((bsz, t, dl), F32)],
        scratch_shapes=[pltpu.VMEM((N_DIR, SUBLANES, dl), F32), pltpu.VMEM((N_DIR, SUBLANES, dl), F32)],
        compiler_params=_cparams(("arbitrary", "arbitrary")),
        name="rglru",
    )(lru_c, lru_l, lru_l, cw, cb, wax, bax, lam)


def _route(logits):
    ng, ne = N_GROUPS, EXPERTS_PER_GROUP
    lane = lax.broadcasted_iota(jnp.int32, logits.shape, 1).astype(F32)
    big = float(ROUTER_LANES)
    is_g = lane < ng
    gl = jnp.where(is_g, logits, -jnp.inf)
    gmax = jnp.max(gl, axis=-1, keepdims=True)
    g_sel = jnp.min(jnp.where(gl == gmax, lane, big), axis=-1, keepdims=True)
    p_g = 1.0 / jnp.sum(jnp.where(is_g, jnp.exp(logits - gmax), 0.0), axis=-1, keepdims=True)
    lo = ng + ne * g_sel
    el = jnp.where((lane >= lo) & (lane < lo + ne), logits, -jnp.inf)
    v1 = jnp.max(el, axis=-1, keepdims=True)
    i1 = jnp.min(jnp.where(el == v1, lane, big), axis=-1, keepdims=True)
    el2 = jnp.where(lane == i1, -jnp.inf, el)
    v2 = jnp.max(el2, axis=-1, keepdims=True)
    i2 = jnp.min(jnp.where(el2 == v2, lane, big), axis=-1, keepdims=True)
    e2 = jnp.exp(v2 - v1)
    w1 = p_g / (1.0 + e2)
    w2 = p_g * e2 / (1.0 + e2)
    return jnp.where(lane == i1, w1, 0.0) + jnp.where(lane == i2, w2, 0.0)


def _out_kernel(a_ref, mf_ref, mb_ref, o_ref, lf_ref, lb_ref, lg_ref, h_ref, g1_ref, sh2_ref, sc2_ref,
                n2g_ref, mng_ref, wout_ref, wrh_ref, wrl_ref, br_ref,
                hnew_ref, v_ref, comb_ref, *, transposed, kb, d, hd):
    dc = a_ref.shape[2]
    dm = mf_ref.shape[2]
    mh = mf_ref[0] + mb_ref[0]
    parts = []
    for h in range(dm // hd):
        x = mh[:, h * hd:(h + 1) * hd]
        xc = x - jnp.mean(x, axis=-1, keepdims=True)
        parts.append(xc * lax.rsqrt(jnp.mean(xc * xc, axis=-1, keepdims=True) + EPS) * mng_ref[:, h * hd:(h + 1) * hd])
    m_out = (_sigmoid(o_ref[0]) * jnp.concatenate(parts, axis=-1)).astype(BF16)
    r_out = ((lf_ref[0] + lb_ref[0]) * _gelu_tanh(lg_ref[0])).astype(BF16)
    y = (_dot(a_ref[0], wout_ref[0:dc, :]) + _dot(m_out, wout_ref[dc:dc + dm, :])
         + _dot(r_out, wout_ref[dc + dm:, :]))
    hn = _load_tile(h_ref, transposed, kb, d) + g1_ref[0] * y
    _store_tile(hnew_ref, hn, transposed, kb, d)
    ms = jnp.mean(hn * hn, axis=-1, keepdims=True)
    v = (hn * lax.rsqrt(ms + EPS)) * n2g_ref[...]
    v = v * (1.0 + sc2_ref[0]) + sh2_ref[0]
    _store_tile(v_ref, v, transposed, kb, d)
    vh, vl = _split_hi_lo(v)
    logits = _dot(vh, wrh_ref[...]) + _dot(vl, wrh_ref[...]) + _dot(vh, wrl_ref[...]) + br_ref[...]
    _store_tile(comb_ref, _route(logits), transposed, kb, ROUTER_LANES)


def _out_proj(a, mf, mb, o, lf, lb, lru, h, g1, sh2, sc2, n2g, mng, w_out, wrh, wrl, br, tm, transposed):
    bsz, t, d = h.shape
    dc, dm, dl = a.shape[2], mf.shape[2], lf.shape[2]
    per_batch = g1.shape[0] == bsz
    mod_spec = pl.BlockSpec((1, 1, d), (lambda bi, j: (bi, 0, 0)) if per_batch else (lambda bi, j: (0, 0, 0)))
    kb = tm // (t // GRID_W) if transposed else 1

    def scan(width, blk=0):
        return pl.BlockSpec((1, tm, width), lambda bi, j: (bi, j, blk))

    def whole(x):
        return pl.BlockSpec(x.shape, lambda bi, j: (0,) * x.ndim)

    tok = _tile_spec(t, tm, d, transposed)
    tok_r = _tile_spec(t, tm, ROUTER_LANES, transposed)
    n2g, mng = n2g.reshape(1, d), mng.reshape(1, dm)
    hv = _tile_view(h, transposed)
    hnew, v, comb = pl.pallas_call(
        functools.partial(_out_kernel, transposed=transposed, kb=kb, d=d, hd=dm // MLSTM_HEADS),
        grid=(bsz, t // tm),
        in_specs=[scan(dc), scan(dm), scan(dm), scan(dm), scan(dl), scan(dl), scan(dl, 1), tok,
                  mod_spec, mod_spec, mod_spec, whole(n2g), whole(mng), whole(w_out), whole(wrh), whole(wrl), whole(br)],
        out_specs=[tok, tok, tok_r],
        out_shape=[jax.ShapeDtypeStruct(hv.shape, F32), jax.ShapeDtypeStruct(hv.shape, BF16),
                   jax.ShapeDtypeStruct(hv.shape[:2] + (hv.shape[2] // d * ROUTER_LANES,), F32)],
        compiler_params=_cparams(("arbitrary", "arbitrary")),
        name="out_proj",
    )(a, mf, mb, o, lf, lb, lru, hv, g1, sh2, sc2, n2g, mng, w_out, wrh, wrl, br)
    return _tile_unview(hnew, t, transposed), _tile_unview(v, t, transposed), _tile_unview(comb, t, transposed)


def _moe_kernel(x_ref, comb_ref, wgu_ref, wd_ref, y_ref):
    e = pl.program_id(1)
    de = wd_ref.shape[1]

    @pl.when(e == 0)
    def _():
        y_ref[...] = jnp.zeros_like(y_ref)

    gu = _dot(x_ref[...], wgu_ref[0])
    g, u = gu[:, :de], gu[:, de:]
    hmid = (g * _sigmoid(g) * u).astype(BF16)
    lane = lax.broadcasted_iota(jnp.int32, comb_ref.shape, 1)
    cw = jnp.sum(jnp.where(lane == e + N_GROUPS, comb_ref[...], 0.0), axis=-1, keepdims=True)
    y_ref[...] += cw * _dot(hmid, wd_ref[0])


def _moe(x, comb, wgu, wd, tm):
    n, d = x.shape
    ne, _, de2 = wgu.shape
    return pl.pallas_call(
        _moe_kernel,
        grid=(n // tm, ne),
        in_specs=[pl.BlockSpec((tm, d), lambda i, e: (i, 0)), pl.BlockSpec((tm, ROUTER_LANES), lambda i, e: (i, 0)),
                  pl.BlockSpec((1, d, de2), lambda i, e: (e, 0, 0)), pl.BlockSpec((1, de2 // 2, d), lambda i, e: (e, 0, 0))],
        out_specs=pl.BlockSpec((tm, d), lambda i, e: (i, 0)),
        out_shape=jax.ShapeDtypeStruct((n, d), F32),
        compiler_params=_cparams(("arbitrary", "arbitrary")),
        name="moe",
    )(x, comb, wgu, wd)


def _final_kernel(h_ref, y_ref, g2_ref, g_ref, o_ref):
    h = h_ref[0] + g2_ref[0] * y_ref[0]
    ms = jnp.mean(h * h, axis=-1, keepdims=True)
    o_ref[0] = (h * lax.rsqrt(ms + EPS)) * g_ref[...]


def _final(h, y, g2, g, tm):
    bsz, t, d = h.shape
    tok = pl.BlockSpec((1, tm, d), lambda bi, j: (bi, j, 0))
    return pl.pallas_call(
        _final_kernel,
        grid=(bsz, t // tm),
        in_specs=[tok, tok, pl.BlockSpec((1, 1, d), lambda bi, j: (bi, 0, 0)), pl.BlockSpec((1, d), lambda bi, j: (0, 0))],
        out_specs=tok,
        out_shape=jax.ShapeDtypeStruct((bsz, t, d), F32),
        compiler_params=_cparams(("arbitrary", "arbitrary")),
        name="final_norm",
    )(h, y, g2, g.reshape(1, d))


def _block_diag(w):
    nh, bw, _ = w.shape
    eye = jnp.eye(nh, dtype=w.dtype)
    return (eye[:, None, :, None] * w[:, :, None, :]).reshape(nh * bw, nh * bw)


def kernel(x, c, ctx, c_ctx, w_mod, b_mod, norm1_g, norm2_g, w_in, conv_w, conv_b, conv_ln_g, conv_ln_b,
           mlstm_b_i, mlstm_b_f, mlstm_norm_g, lru_conv_w, lru_conv_b, lru_w_a, lru_b_a, lru_w_x, lru_b_x,
           lru_lambda, w_out, w_rg, b_rg, w_re, b_re, w_gate, w_up, w_down, final_g):
    bsz, t, d = x.shape
    tc = ctx.shape[1]
    depth = w_mod.shape[0]
    dc = conv_w.shape[2]
    dm = mlstm_norm_g.shape[1]
    dl = lru_lambda.shape[2]
    ngate = N_DIR * MLSTM_HEADS
    tm = min(TOKEN_TILE, t)
    tm_c = min(TOKEN_TILE, tc)
    tb = min(SCAN_BLOCK, t)

    rp = -(-(bsz + 1) // SUBLANES) * SUBLANES
    cvec = jnp.zeros((rp, d), F32).at[:bsz].set(c).at[bsz].set(c_ctx)
    mod = _modulation(cvec, w_mod, b_mod)

    c_q = 2 * dc
    c_o = c_q + 3 * dm
    c_g = c_o + dm
    c_l = c_g + 2 * ngate
    splits = (2 * dc, 3 * dm, dm, 2 * dl, GATE_LANES)
    out_dtypes = (F32, BF16, F32, F32, F32)

    h_lat, h_ctx = x, ctx
    prev_lat = prev_ctx = None
    for l in range(depth):
        last = l == depth - 1
        transposed = l % 2 == 1
        m_lat = [mod[l, :bsz, k * d:(k + 1) * d].reshape(bsz, 1, d) for k in range(6)]
        m_ctx = [mod[l, bsz:bsz + 1, k * d:(k + 1) * d].reshape(1, 1, d) for k in range(6)]
        wl = w_in[l]
        w_perm = jnp.concatenate(
            [wl[:, :c_o], wl[:, c_o:c_g], wl[:, c_l:], wl[:, c_g:c_l],
             jnp.zeros((d, GATE_LANES - 2 * ngate), F32)], axis=1).astype(BF16)
        gate_bias = jnp.zeros((1, GATE_LANES), F32).at[0, :ngate].set(mlstm_b_i[l].reshape(-1))
        gate_bias = gate_bias.at[0, ngate:2 * ngate].set(mlstm_b_f[l].reshape(-1))
        wax = jnp.stack([jnp.concatenate([_block_diag(lru_w_a[l, dd]), _block_diag(lru_w_x[l, dd])], axis=1)
                         for dd in range(N_DIR)]).astype(BF16)
        bax = jnp.concatenate([lru_b_a[l], lru_b_x[l]], axis=-1).reshape(N_DIR, 1, 2 * dl)
        lam = lru_lambda[l].reshape(N_DIR, 1, dl)
        lcb = lru_conv_b[l].reshape(N_DIR, 1, dl)
        wr = jnp.zeros((d, ROUTER_LANES), F32).at[:, :N_GROUPS].set(w_rg[l]).at[:, N_GROUPS:N_GROUPS + N_EXPERTS].set(w_re[l])
        wrh, wrl = _split_hi_lo(wr)
        br = jnp.zeros((1, ROUTER_LANES), F32).at[0, :N_GROUPS].set(b_rg[l]).at[0, N_GROUPS:N_GROUPS + N_EXPERTS].set(b_re[l])
        wgu = jnp.concatenate([w_gate[l], w_up[l]], axis=-1).astype(BF16)
        wd = w_down[l].astype(BF16)
        wo = w_out[l].astype(BF16)

        hcur, (cv_l, qkv_l, o_l, lru_l, gt_l) = _in_proj(h_lat, m_lat[0], m_lat[1], norm1_g[l], w_perm, splits,
                                                        out_dtypes, tm, transposed, prev_lat)
        if hcur is not None:
            h_lat = hcur
        hcur, (cv_c, qkv_c, o_c, lru_c, gt_c) = _in_proj(h_ctx, m_ctx[0], m_ctx[1], norm1_g[l], w_perm, splits,
                                                        out_dtypes, tm_c, False, prev_ctx)
        if hcur is not None:
            h_ctx = hcur

        a_l = _conformer_conv(cv_l, conv_w[l], conv_b[l], conv_ln_g[l], conv_ln_b[l], tm)
        mcf, mcb, mlf, mlb = _mlstm(qkv_c, gt_c, qkv_l, gt_l, gate_bias, tb)
        rcf, rcb, rlf, rlb = _rglru(lru_c, lru_l, lru_conv_w[l], lcb, wax, bax, lam, tb)

        h_lat, v_l, comb_l = _out_proj(a_l, mlf, mlb, o_l, rlf, rlb, lru_l, h_lat, m_lat[2], m_lat[3], m_lat[4],
                                       norm2_g[l], mlstm_norm_g[l], wo, wrh, wrl, br, tm, transposed)
        y_l = _moe(v_l.reshape(bsz * t, d), comb_l.reshape(bsz * t, ROUTER_LANES), wgu, wd,
                   min(MOE_TILE, bsz * t)).reshape(bsz, t, d)
        prev_lat = (y_l, m_lat[5])
        if not last:
            a_c = _conformer_conv(cv_c, conv_w[l], conv_b[l], conv_ln_g[l], conv_ln_b[l], tm_c)
            h_ctx, v_c, comb_c = _out_proj(a_c, mcf, mcb, o_c, rcf, rcb, lru_c, h_ctx, m_ctx[2], m_ctx[3], m_ctx[4],
                                           norm2_g[l], mlstm_norm_g[l], wo, wrh, wrl, br, tm_c, False)
            y_c = _moe(v_c.reshape(bsz * tc, d), comb_c.reshape(bsz * tc, ROUTER_LANES), wgu, wd,
                       min(MOE_TILE, bsz * tc)).reshape(bsz, tc, d)
            prev_ctx = (y_c, m_ctx[5])
    return _final(h_lat, prev_lat[0], prev_lat[1], final_g, tm)
```

```python
import functools

import jax
import jax.numpy as jnp
from jax import lax
from jax.experimental import pallas as pl
from jax.experimental.pallas import tpu as pltpu

EPS = 1e-6
GRID_W = 64
D_CONV_FRAC = 4
CONV_WIDTH = 31
CONV_HALO = 16
MLSTM_HEADS = 4
MLSTM_CHUNK = 128
LRU_HEADS = 4
LRU_CONV_WIDTH = 4
LRU_C = 8.0
N_DIR = 2
N_GROUPS = 4
EXPERTS_PER_GROUP = 4
N_EXPERTS = N_GROUPS * EXPERTS_PER_GROUP
N_PAIRS = EXPERTS_PER_GROUP * (EXPERTS_PER_GROUP - 1) // 2
N_CLASSES = N_GROUPS * N_PAIRS
ROUTER_LANES = 128
INFO_CLASS, INFO_RANK, INFO_W_LO, INFO_W_HI = 0, 1, 2, 3
GATE_LANES = 128
SUBLANES = 8

VMEM_LIMIT = 56 * 1024 * 1024
TOKEN_TILE = 512
SCAN_BLOCK = 256
MOE_TILE = 256
DISPATCH_TILE = 512
DMA_UNROLL = 8

F32 = jnp.float32
BF16 = jnp.bfloat16


def _cparams(sem):
    return pltpu.CompilerParams(dimension_semantics=sem, vmem_limit_bytes=VMEM_LIMIT)


def _sigmoid(x):
    return jax.nn.sigmoid(x)


def _log_sigmoid(x):
    return jnp.minimum(x, 0.0) - jnp.log1p(jnp.exp(-jnp.abs(x)))


def _softplus(x):
    return jnp.maximum(x, 0.0) + jnp.log1p(jnp.exp(-jnp.abs(x)))


def _gelu_tanh(x):
    return x * (0.5 * (1.0 + jnp.tanh(0.7978845608028654 * (x + 0.044715 * (x * x * x)))))


def _dot(a, b):
    return jnp.dot(a, b, preferred_element_type=F32)


def _split_hi_lo(x):
    hi = x.astype(BF16)
    lo = (x - hi.astype(F32)).astype(BF16)
    return hi, lo


def _load_tile(ref, transposed):
    if not transposed:
        return ref[0]
    return jnp.concatenate([ref[0, :, w, :] for w in range(SUBLANES)], axis=0)


def _store_tile(ref, val, transposed):
    if not transposed:
        ref[0] = val.astype(ref.dtype)
        return
    rows = val.shape[0] // SUBLANES
    for w in range(SUBLANES):
        ref[0, :, w, :] = val[w * rows:(w + 1) * rows].astype(ref.dtype)


def _tile_spec(t, tm, d, transposed):
    if not transposed:
        return pl.BlockSpec((1, tm, d), lambda b, j: (b, j, 0))
    rows = t // GRID_W
    assert tm == SUBLANES * rows
    return pl.BlockSpec((1, rows, SUBLANES, d), lambda b, j: (b, 0, j, 0))


def _tile_view(a, transposed):
    if not transposed:
        return a
    b, t, d = a.shape
    return a.reshape(b, t // GRID_W, GRID_W, d)


def _tile_unview(a, t, transposed):
    if not transposed:
        return a
    b = a.shape[0]
    return a.reshape(b, t, -1)


def _mod_kernel(c_ref, w_ref, b_ref, o_ref):
    c = c_ref[...]
    s = (c * _sigmoid(c)).astype(BF16)
    o_ref[0] = _dot(s, w_ref[0].astype(BF16)) + b_ref[0]


def _modulation(cvec, w_mod, b_mod):
    nl, d, d6 = w_mod.shape
    rp = cvec.shape[0]
    tn = d6 // 4
    return pl.pallas_call(
        _mod_kernel,
        grid=(nl, d6 // tn),
        in_specs=[pl.BlockSpec((rp, d), lambda l, j: (0, 0)),
                  pl.BlockSpec((1, d, tn), lambda l, j: (l, 0, j)),
                  pl.BlockSpec((1, 1, tn), lambda l, j: (l, 0, j))],
        out_specs=pl.BlockSpec((1, rp, tn), lambda l, j: (l, 0, j)),
        out_shape=jax.ShapeDtypeStruct((nl, rp, d6), F32),
        compiler_params=_cparams(("arbitrary", "arbitrary")),
        name="modulation",
    )(cvec, w_mod, b_mod.reshape(nl, 1, d6))


def _in_kernel(h_ref, shift_ref, scale_ref, g_ref, w_ref, *outs, transposed, splits):
    h = _load_tile(h_ref, transposed)
    ms = jnp.mean(h * h, axis=-1, keepdims=True)
    u = (h * lax.rsqrt(ms + EPS)) * g_ref[...]
    u = (u * (1.0 + scale_ref[0]) + shift_ref[0]).astype(BF16)
    c0 = 0
    for o_ref, width in zip(outs, splits):
        o_ref[0] = _dot(u, w_ref[:, c0:c0 + width]).astype(o_ref.dtype)
        c0 += width


def _in_proj(h, shift, scale, norm_g, w, splits, out_dtypes, tm, transposed):
    b, t, d = h.shape
    per_batch = shift.shape[0] == b
    mod_spec = pl.BlockSpec((1, 1, d), (lambda bi, j: (bi, 0, 0)) if per_batch else (lambda bi, j: (0, 0, 0)))
    return pl.pallas_call(
        functools.partial(_in_kernel, transposed=transposed, splits=splits),
        grid=(b, t // tm),
        in_specs=[_tile_spec(t, tm, d, transposed), mod_spec, mod_spec, pl.BlockSpec((1, d), lambda bi, j: (0, 0)),
                  pl.BlockSpec(w.shape, lambda bi, j: (0, 0))],
        out_specs=[pl.BlockSpec((1, tm, width), lambda bi, j: (bi, j, 0)) for width in splits],
        out_shape=[jax.ShapeDtypeStruct((b, t, width), dt) for width, dt in zip(splits, out_dtypes)],
        compiler_params=_cparams(("arbitrary", "arbitrary")),
        name="in_proj",
    )(_tile_view(h, transposed), shift, scale, norm_g.reshape(1, d), w)


def _conv_kernel(prev_ref, cur_ref, next_ref, w_ref, b_ref, lng_ref, lnb_ref, o_ref, *, tb, nblk, dc):
    j = pl.program_id(1)

    def glu(x):
        return x[:, :dc] * _sigmoid(x[:, dc:])

    up = jnp.where(j > 0, glu(prev_ref[0]), 0.0)
    un = jnp.where(j < nblk - 1, glu(next_ref[0]), 0.0)
    ext = jnp.concatenate([up, glu(cur_ref[0]), un], axis=0)
    base = CONV_HALO - CONV_WIDTH // 2
    acc = jnp.zeros((tb, dc), F32)
    for k in range(CONV_WIDTH):
        acc = acc + w_ref[k:k + 1, :] * ext[base + k:base + k + tb, :]
    acc = acc + b_ref[...]
    mu = jnp.mean(acc, axis=-1, keepdims=True)
    xc = acc - mu
    y = xc * lax.rsqrt(jnp.mean(xc * xc, axis=-1, keepdims=True) + EPS) * lng_ref[...] + lnb_ref[...]
    o_ref[0] = (y * _sigmoid(y)).astype(o_ref.dtype)


def _conformer_conv(cv, w, b, ln_g, ln_b, tb):
    bsz, t, c2 = cv.shape
    dc = c2 // 2
    nblk = t // tb
    hb = tb // CONV_HALO
    nh = t // CONV_HALO
    wp = jnp.zeros((CONV_WIDTH + 1, dc), F32).at[:CONV_WIDTH].set(w)
    vec = pl.BlockSpec((1, dc), lambda bi, j: (0, 0))
    return pl.pallas_call(
        functools.partial(_conv_kernel, tb=tb, nblk=nblk, dc=dc),
        grid=(bsz, nblk),
        in_specs=[pl.BlockSpec((1, CONV_HALO, c2), lambda bi, j: (bi, jnp.maximum(j * hb - 1, 0), 0)),
                  pl.BlockSpec((1, tb, c2), lambda bi, j: (bi, j, 0)),
                  pl.BlockSpec((1, CONV_HALO, c2), lambda bi, j: (bi, jnp.minimum((j + 1) * hb, nh - 1), 0)),
                  pl.BlockSpec((CONV_WIDTH + 1, dc), lambda bi, j: (0, 0)), vec, vec, vec],
        out_specs=pl.BlockSpec((1, tb, dc), lambda bi, j: (bi, j, 0)),
        out_shape=jax.ShapeDtypeStruct((bsz, t, dc), BF16),
        compiler_params=_cparams(("arbitrary", "arbitrary")),
        name="conformer_conv",
    )(cv, cv, cv, wp, b.reshape(1, dc), ln_g.reshape(1, dc), ln_b.reshape(1, dc))


def _mlstm_block(q_ref, g_ref, bias_ref, out_ref, c_ref, n_ref, m_ref, d, nchunks, hd):
    L = MLSTM_CHUNK
    nh = MLSTM_HEADS
    dm = nh * hd
    scale = hd ** -0.5
    row = lax.broadcasted_iota(jnp.int32, (L, L), 0)
    col = lax.broadcasted_iota(jnp.int32, (L, L), 1)
    if d == 0:
        mask = col <= row
        tri_cols = jnp.where(col <= row, 1.0, 0.0).astype(BF16)
        tri_rows = jnp.where(row <= col, 1.0, 0.0).astype(BF16)
        order = range(nchunks)
    else:
        mask = col >= row
        tri_cols = jnp.where(col >= row, 1.0, 0.0).astype(BF16)
        tri_rows = jnp.where(row >= col, 1.0, 0.0).astype(BF16)
        order = range(nchunks - 1, -1, -1)
    for ci in order:
        r0 = ci * L
        g = g_ref[0, r0:r0 + L, :] + bias_ref[...]
        gt = g.T[0:2 * N_DIR * nh, :]
        hi, lo = _split_hi_lo(_log_sigmoid(g))
        cum_c = _dot(tri_cols, hi) + _dot(tri_cols, lo)
        hi, lo = _split_hi_lo(_log_sigmoid(gt))
        cum_r = _dot(hi, tri_rows) + _dot(lo, tri_rows)
        for h in range(nh):
            ic = d * nh + h
            fc = N_DIR * nh + d * nh + h
            idx = d * nh + h
            b_col = cum_c[:, fc:fc + 1]
            b_row = cum_r[fc:fc + 1, :]
            ig_col = g[:, ic:ic + 1]
            ig_row = gt[ic:ic + 1, :]
            m = m_ref[idx]
            log_d = jnp.where(mask, b_col - b_row + ig_row, -jnp.inf)
            m_inter = b_col + m
            m_t = jnp.maximum(m_inter, jnp.max(log_d, axis=-1, keepdims=True))
            dmat = jnp.exp(log_d - m_t)
            inter = jnp.exp(m_inter - m_t)
            q = q_ref[0, r0:r0 + L, h * hd:(h + 1) * hd]
            k = q_ref[0, r0:r0 + L, dm + h * hd:dm + (h + 1) * hd]
            v = q_ref[0, r0:r0 + L, 2 * dm + h * hd:2 * dm + (h + 1) * hd]
            s = lax.dot_general(q, k, (((1,), (1,)), ((), ())), preferred_element_type=F32) * scale * dmat
            cmat = c_ref[idx]
            nvec = n_ref[idx]
            num = _dot(s.astype(BF16), v) + inter * _dot(q, cmat.astype(BF16))
            den = jnp.sum(s, axis=-1, keepdims=True) + inter * jnp.sum(q.astype(F32) * nvec, axis=-1, keepdims=True)
            out_ref[0, r0:r0 + L, h * hd:(h + 1) * hd] = num / jnp.maximum(jnp.abs(den), jnp.exp(-m_t))
            b_last = b_col[L - 1:L, :] if d == 0 else b_col[0:1, :]
            log_w = b_last - b_col + ig_col
            m_new = jnp.maximum(b_last + m, jnp.max(log_w, axis=0, keepdims=True))
            wgt = jnp.exp(log_w - m_new)
            decay = jnp.exp(b_last + m - m_new)
            kw = k.astype(F32) * (wgt * scale)
            c_ref[idx] = decay * cmat + lax.dot_general(kw.astype(BF16), v, (((0,), (0,)), ((), ())),
                                                        preferred_element_type=F32)
            n_ref[idx] = decay * nvec + jnp.sum(kw, axis=0, keepdims=True)
            m_ref[idx] = m_new


def _mlstm_kernel(qc_ref, gc_ref, qf_ref, gf_ref, qb_ref, gb_ref, bias_ref,
                  hcf_ref, hcb_ref, hf_ref, hb_ref, c_ref, n_ref, m_ref, *, nc_ctx, nc_lat, hd):
    s = pl.program_id(1)
    state = (c_ref, n_ref, m_ref)

    @pl.when(s == 0)
    def _():
        c_ref[...] = jnp.zeros_like(c_ref)
        n_ref[...] = jnp.zeros_like(n_ref)
        m_ref[...] = jnp.zeros_like(m_ref)
        _mlstm_block(qc_ref, gc_ref, bias_ref, hcf_ref, *state, 0, nc_ctx, hd)
        _mlstm_block(qc_ref, gc_ref, bias_ref, hcb_ref, *state, 1, nc_ctx, hd)

    @pl.when(s > 0)
    def _():
        _mlstm_block(qf_ref, gf_ref, bias_ref, hf_ref, *state, 0, nc_lat, hd)
        _mlstm_block(qb_ref, gb_ref, bias_ref, hb_ref, *state, 1, nc_lat, hd)


def _mlstm(qkv_c, gt_c, qkv_l, gt_l, bias, tb):
    bsz, tc, w3 = qkv_c.shape
    t = qkv_l.shape[1]
    dm = w3 // 3
    hd = dm // MLSTM_HEADS
    nb = t // tb
    nstate = N_DIR * MLSTM_HEADS

    def fwd(bi, s):
        return (bi, jnp.maximum(s - 1, 0), 0)

    def bwd(bi, s):
        return (bi, nb - 1 - jnp.maximum(s - 1, 0), 0)

    def ctx(bi, s):
        return (bi, 0, 0)

    return pl.pallas_call(
        functools.partial(_mlstm_kernel, nc_ctx=tc // MLSTM_CHUNK, nc_lat=tb // MLSTM_CHUNK, hd=hd),
        grid=(bsz, nb + 1),
        in_specs=[pl.BlockSpec((1, tc, w3), ctx), pl.BlockSpec((1, tc, GATE_LANES), ctx),
                  pl.BlockSpec((1, tb, w3), fwd), pl.BlockSpec((1, tb, GATE_LANES), fwd),
                  pl.BlockSpec((1, tb, w3), bwd), pl.BlockSpec((1, tb, GATE_LANES), bwd),
                  pl.BlockSpec((1, GATE_LANES), lambda bi, s: (0, 0))],
        out_specs=[pl.BlockSpec((1, tc, dm), ctx), pl.BlockSpec((1, tc, dm), ctx),
                   pl.BlockSpec((1, tb, dm), fwd), pl.BlockSpec((1, tb, dm), bwd)],
        out_shape=[jax.ShapeDtypeStruct((bsz, tc, dm), F32), jax.ShapeDtypeStruct((bsz, tc, dm), F32),
                   jax.ShapeDtypeStruct((bsz, t, dm), F32), jax.ShapeDtypeStruct((bsz, t, dm), F32)],
        scratch_shapes=[pltpu.VMEM((nstate, hd, hd), F32), pltpu.VMEM((nstate, 1, hd), F32),
                        pltpu.VMEM((nstate, 1, 1), F32)],
        compiler_params=_cparams(("arbitrary", "arbitrary")),
        name="mlstm",
    )(qkv_c, gt_c, qkv_l, gt_l, qkv_l, gt_l, bias)


def _lru_block(x, halo, d, cw, cb, wax, bax, lam, h0):
    tb, dl = x.shape
    kw = LRU_CONV_WIDTH
    if d == 0:
        ext = jnp.concatenate([halo, x], axis=0)
        taps = [ext[SUBLANES - (kw - 1) + j:SUBLANES - (kw - 1) + j + tb] for j in range(kw)]
    else:
        ext = jnp.concatenate([x, halo], axis=0)
        taps = [ext[kw - 1 - j:kw - 1 - j + tb] for j in range(kw)]
    xc = cb
    for j in range(kw):
        xc = xc + cw[j:j + 1, :] * taps[j]
    ri = _sigmoid(_dot(xc.astype(BF16), wax) + bax)
    r, i = ri[:, :dl], ri[:, dl:]
    log_a = (-LRU_C * r) * _softplus(-lam)
    a = jnp.exp(log_a)
    th = jnp.tanh(log_a)
    u = jnp.sqrt(-2.0 * th / (1.0 - th)) * (i * xc)
    sub = lax.broadcasted_iota(jnp.int32, (tb, dl), 0) & (SUBLANES - 1)
    for sh in (1, 2, 4):
        if d == 0:
            a_s, u_s, msk = pltpu.roll(a, sh, 0), pltpu.roll(u, sh, 0), sub >= sh
        else:
            a_s, u_s, msk = pltpu.roll(a, tb - sh, 0), pltpu.roll(u, tb - sh, 0), sub < SUBLANES - sh
        u = jnp.where(msk, a * u_s + u, u)
        a = jnp.where(msk, a * a_s, a)
    ngrp = tb // SUBLANES
    outs = [None] * ngrp
    carry = h0
    for j in (range(ngrp) if d == 0 else range(ngrp - 1, -1, -1)):
        hj = u[j * SUBLANES:(j + 1) * SUBLANES] + a[j * SUBLANES:(j + 1) * SUBLANES] * carry
        outs[j] = hj
        carry = hj[SUBLANES - 1:SUBLANES] if d == 0 else hj[0:1]
    return jnp.concatenate(outs, axis=0), carry


def _lru_kernel(xc_ref, xf_ref, xb_ref, cw_ref, cb_ref, wax_ref, bax_ref, lam_ref,
                hcf_ref, hcb_ref, hf_ref, hb_ref, hcar_ref, halo_ref):
    s = pl.program_id(1)

    def params(d):
        return cw_ref[d], cb_ref[d], wax_ref[d], bax_ref[d], lam_ref[d]

    @pl.when(s == 0)
    def _():
        x = xc_ref[0]
        zero_halo = jnp.zeros((SUBLANES, x.shape[1]), F32)
        zero_h = jnp.zeros((1, x.shape[1]), F32)
        for d, o_ref in ((0, hcf_ref), (1, hcb_ref)):
            h, carry = _lru_block(x, zero_halo, d, *params(d), zero_h)
            o_ref[0] = h
            hcar_ref[d] = jnp.broadcast_to(carry, hcar_ref.shape[1:])

    @pl.when(s > 0)
    def _():
        for d, x_ref, o_ref in ((0, xf_ref, hf_ref), (1, xb_ref, hb_ref)):
            x = x_ref[0]
            halo = jnp.where(s > 1, halo_ref[d], 0.0)
            h, carry = _lru_block(x, halo, d, *params(d), hcar_ref[d][0:1])
            o_ref[0] = h
            hcar_ref[d] = jnp.broadcast_to(carry, hcar_ref.shape[1:])
            halo_ref[d] = x[x.shape[0] - SUBLANES:] if d == 0 else x[:SUBLANES]


def _rglru(lru_c, lru_l, cw, cb, wax, bax, lam, tb):
    bsz, tc, w2 = lru_c.shape
    t = lru_l.shape[1]
    dl = w2 // 2
    nb = t // tb

    def fwd(bi, s):
        return (bi, jnp.maximum(s - 1, 0), 0)

    def bwd(bi, s):
        return (bi, nb - 1 - jnp.maximum(s - 1, 0), 0)

    def ctx(bi, s):
        return (bi, 0, 0)

    def whole(a):
        return pl.BlockSpec(a.shape, lambda bi, s: (0,) * a.ndim)

    return pl.pallas_call(
        _lru_kernel,
        grid=(bsz, nb + 1),
        in_specs=[pl.BlockSpec((1, tc, dl), ctx), pl.BlockSpec((1, tb, dl), fwd), pl.BlockSpec((1, tb, dl), bwd),
                  whole(cw), whole(cb), whole(wax), whole(bax), whole(lam)],
        out_specs=[pl.BlockSpec((1, tc, dl), ctx), pl.BlockSpec((1, tc, dl), ctx),
                   pl.BlockSpec((1, tb, dl), fwd), pl.BlockSpec((1, tb, dl), bwd)],
        out_shape=[jax.ShapeDtypeStruct((bsz, tc, dl), F32), jax.ShapeDtypeStruct((bsz, tc, dl), F32),
                   jax.ShapeDtypeStruct((bsz, t, dl), F32), jax.ShapeDtypeStruct((bsz, t, dl), F32)],
        scratch_shapes=[pltpu.VMEM((N_DIR, SUBLANES, dl), F32), pltpu.VMEM((N_DIR, SUBLANES, dl), F32)],
        compiler_params=_cparams(("arbitrary", "arbitrary")),
        name="rglru",
    )(lru_c, lru_l, lru_l, cw, cb, wax, bax, lam)


def _route(logits):
    ng, ne = N_GROUPS, EXPERTS_PER_GROUP
    lane = lax.broadcasted_iota(jnp.int32, logits.shape, 1).astype(F32)
    big = float(ROUTER_LANES)
    is_g = lane < ng
    gl = jnp.where(is_g, logits, -jnp.inf)
    gmax = jnp.max(gl, axis=-1, keepdims=True)
    g_sel = jnp.min(jnp.where(gl == gmax, lane, big), axis=-1, keepdims=True)
    p_g = 1.0 / jnp.sum(jnp.where(is_g, jnp.exp(logits - gmax), 0.0), axis=-1, keepdims=True)
    lo = ng + ne * g_sel
    el = jnp.where((lane >= lo) & (lane < lo + ne), logits, -jnp.inf)
    v1 = jnp.max(el, axis=-1, keepdims=True)
    i1 = jnp.min(jnp.where(el == v1, lane, big), axis=-1, keepdims=True)
    el2 = jnp.where(lane == i1, -jnp.inf, el)
    v2 = jnp.max(el2, axis=-1, keepdims=True)
    i2 = jnp.min(jnp.where(el2 == v2, lane, big), axis=-1, keepdims=True)
    e2 = jnp.exp(v2 - v1)
    w1 = p_g / (1.0 + e2)
    w2 = p_g * e2 / (1.0 + e2)
    first_lower = i1 < i2
    e_lo = jnp.minimum(i1, i2) - lo
    e_hi = jnp.maximum(i1, i2) - lo
    pair = e_lo * (2 * ne - 1 - e_lo) * 0.5 + (e_hi - e_lo - 1.0)
    return g_sel * N_PAIRS + pair, jnp.where(first_lower, w1, w2), jnp.where(first_lower, w2, w1)


def _out_kernel(a_ref, mf_ref, mb_ref, o_ref, lf_ref, lb_ref, lg_ref, h_ref, g1_ref, sh2_ref, sc2_ref,
                n2g_ref, mng_ref, wout_ref, wrh_ref, wrl_ref, br_ref,
                hnew_ref, vx_ref, cnt_ref, run_ref, *, transposed, d, hd):
    @pl.when((pl.program_id(0) == 0) & (pl.program_id(1) == 0))
    def _():
        run_ref[...] = jnp.zeros_like(run_ref)

    dc = a_ref.shape[2]
    dm = mf_ref.shape[2]
    mh = mf_ref[0] + mb_ref[0]
    parts = []
    for h in range(dm // hd):
        x = mh[:, h * hd:(h + 1) * hd]
        xc = x - jnp.mean(x, axis=-1, keepdims=True)
        parts.append(xc * lax.rsqrt(jnp.mean(xc * xc, axis=-1, keepdims=True) + EPS) * mng_ref[:, h * hd:(h + 1) * hd])
    m_out = (_sigmoid(o_ref[0]) * jnp.concatenate(parts, axis=-1)).astype(BF16)
    r_out = ((lf_ref[0] + lb_ref[0]) * _gelu_tanh(lg_ref[0])).astype(BF16)
    y = (_dot(a_ref[0], wout_ref[0:dc, :]) + _dot(m_out, wout_ref[dc:dc + dm, :])
         + _dot(r_out, wout_ref[dc + dm:, :]))
    hn = _load_tile(h_ref, transposed) + g1_ref[0] * y
    _store_tile(hnew_ref, hn, transposed)
    ms = jnp.mean(hn * hn, axis=-1, keepdims=True)
    v = (hn * lax.rsqrt(ms + EPS)) * n2g_ref[...]
    v = v * (1.0 + sc2_ref[0]) + sh2_ref[0]
    vh, vl = _split_hi_lo(v)
    logits = _dot(vh, wrh_ref[...]) + _dot(vl, wrh_ref[...]) + _dot(vh, wrl_ref[...]) + br_ref[...]
    cls, w_lo, w_hi = _route(logits)
    tm = v.shape[0]
    lane = lax.broadcasted_iota(jnp.int32, (tm, ROUTER_LANES), 1).astype(F32)
    onehot = jnp.where(lane == cls, 1.0, 0.0)
    row = lax.broadcasted_iota(jnp.int32, (tm, tm), 0)
    col = lax.broadcasted_iota(jnp.int32, (tm, tm), 1)
    before = _dot(jnp.where(col < row, 1.0, 0.0).astype(BF16), onehot.astype(BF16)) + run_ref[0:1, :]
    rank = jnp.sum(onehot * before, axis=-1, keepdims=True)
    run_ref[...] = run_ref[...] + jnp.sum(onehot, axis=0, keepdims=True)
    cnt_ref[...] = run_ref[...]
    info = (jnp.where(lane == INFO_CLASS, cls, 0.0) + jnp.where(lane == INFO_RANK, rank, 0.0)
            + jnp.where(lane == INFO_W_LO, w_lo, 0.0) + jnp.where(lane == INFO_W_HI, w_hi, 0.0))
    _store_tile(vx_ref, jnp.concatenate([v, info], axis=-1), transposed)


def _out_proj(a, mf, mb, o, lf, lb, lru, h, g1, sh2, sc2, n2g, mng, w_out, wrh, wrl, br, tm, transposed):
    bsz, t, d = h.shape
    dc, dm, dl = a.shape[2], mf.shape[2], lf.shape[2]
    per_batch = g1.shape[0] == bsz
    mod_spec = pl.BlockSpec((1, 1, d), (lambda bi, j: (bi, 0, 0)) if per_batch else (lambda bi, j: (0, 0, 0)))

    def scan(width, blk=0):
        return pl.BlockSpec((1, tm, width), lambda bi, j: (bi, j, blk))

    def whole(x):
        return pl.BlockSpec(x.shape, lambda bi, j: (0,) * x.ndim)

    dx = d + ROUTER_LANES
    n2g, mng = n2g.reshape(1, d), mng.reshape(1, dm)
    hv = _tile_view(h, transposed)
    hnew, vx, cnt = pl.pallas_call(
        functools.partial(_out_kernel, transposed=transposed, d=d, hd=dm // MLSTM_HEADS),
        grid=(bsz, t // tm),
        in_specs=[scan(dc), scan(dm), scan(dm), scan(dm), scan(dl), scan(dl), scan(dl, 1),
                  _tile_spec(t, tm, d, transposed),
                  mod_spec, mod_spec, mod_spec, whole(n2g), whole(mng), whole(w_out), whole(wrh), whole(wrl), whole(br)],
        out_specs=[_tile_spec(t, tm, d, transposed), _tile_spec(t, tm, dx, transposed),
                   pl.BlockSpec((SUBLANES, ROUTER_LANES), lambda bi, j: (0, 0))],
        out_shape=[jax.ShapeDtypeStruct(hv.shape, F32), jax.ShapeDtypeStruct(hv.shape[:-1] + (dx,), F32),
                   jax.ShapeDtypeStruct((SUBLANES, ROUTER_LANES), F32)],
        scratch_shapes=[pltpu.VMEM((SUBLANES, ROUTER_LANES), F32)],
        compiler_params=_cparams(("arbitrary", "arbitrary")),
        name="out_proj",
    )(a, mf, mb, o, lf, lb, lru, hv, g1, sh2, sc2, n2g, mng, w_out, wrh, wrl, br)
    return _tile_unview(hnew, t, transposed), _tile_unview(vx, t, transposed), cnt


def _routing_tables(vx, cnt, d, tmo, n_max):
    cls = vx[:, d + INFO_CLASS].astype(jnp.int32)
    rank = vx[:, d + INFO_RANK].astype(jnp.int32)
    counts = cnt[0, :N_CLASSES].astype(jnp.int32)
    tiles = (counts + tmo - 1) // tmo
    tile_end = jnp.cumsum(tiles)
    tile_start = tile_end - tiles
    n_real = tile_end[N_CLASSES - 1]
    slot = jnp.take(tile_start * tmo, cls) + rank
    blk = jnp.minimum(jnp.arange(n_max, dtype=jnp.int32), n_real - 1)
    tile_cls = jnp.minimum(jnp.searchsorted(tile_end, blk, side="right"), N_CLASSES - 1).astype(jnp.int32)
    pair_lo = jnp.array([a for a in range(EXPERTS_PER_GROUP) for b in range(a + 1, EXPERTS_PER_GROUP)], jnp.int32)
    pair_hi = jnp.array([b for a in range(EXPERTS_PER_GROUP) for b in range(a + 1, EXPERTS_PER_GROUP)], jnp.int32)
    grp = tile_cls // N_PAIRS
    e_lo = grp * EXPERTS_PER_GROUP + jnp.take(pair_lo, tile_cls % N_PAIRS)
    e_hi = grp * EXPERTS_PER_GROUP + jnp.take(pair_hi, tile_cls % N_PAIRS)
    return slot, blk, e_lo, e_hi, n_real.reshape(1)


def _row_copy(src_ref, src_row, dst_ref, dst_row, sem):
    return pltpu.make_async_copy(src_ref.at[pl.ds(src_row, 1)], dst_ref.at[pl.ds(dst_row, 1)], sem)


def _dispatch_kernel(slot_ref, vx_ref, xs_in_ref, xs_ref, sem):
    del xs_in_ref
    tm = vx_ref.shape[0]

    def issue(r, carry):
        _row_copy(vx_ref, r, xs_ref, slot_ref[0, 0, r], sem).start()
        return carry

    lax.fori_loop(0, tm, issue, 0, unroll=DMA_UNROLL)
    pltpu.make_async_copy(vx_ref, xs_ref.at[pl.ds(0, tm)], sem).wait()


def _dispatch(vx, slot, n_rows, tm):
    n, dx = vx.shape
    return pl.pallas_call(
        _dispatch_kernel,
        grid=(n // tm,),
        in_specs=[pl.BlockSpec((1, 1, tm), lambda i: (i, 0, 0), memory_space=pltpu.SMEM),
                  pl.BlockSpec((tm, dx), lambda i: (i, 0)),
                  pl.BlockSpec(memory_space=pl.ANY)],
        out_specs=pl.BlockSpec(memory_space=pl.ANY),
        out_shape=jax.ShapeDtypeStruct((n_rows, dx), F32),
        scratch_shapes=[pltpu.SemaphoreType.DMA(())],
        input_output_aliases={2: 0},
        compiler_params=_cparams(("arbitrary",)),
        name="moe_dispatch",
    )(slot.reshape(n // tm, 1, tm), vx, jnp.zeros((n_rows, dx), F32))


def _expert(x, wgu, wd):
    de = wd.shape[0]
    gu = _dot(x, wgu)
    g, u = gu[:, :de], gu[:, de:]
    return _dot((g * _sigmoid(g) * u).astype(BF16), wd)


def _moe_kernel(blk_ref, elo_ref, ehi_ref, nreal_ref, xs_ref, wgu_lo_ref, wd_lo_ref, wgu_hi_ref, wd_hi_ref, ys_ref):
    del blk_ref, elo_ref, ehi_ref
    d = ys_ref.shape[1]

    @pl.when(pl.program_id(0) < nreal_ref[0])
    def _():
        x = xs_ref[:, :d].astype(BF16)
        w_lo = xs_ref[:, d + INFO_W_LO:d + INFO_W_LO + 1]
        w_hi = xs_ref[:, d + INFO_W_HI:d + INFO_W_HI + 1]
        ys_ref[...] = w_lo * _expert(x, wgu_lo_ref[0], wd_lo_ref[0]) + w_hi * _expert(x, wgu_hi_ref[0], wd_hi_ref[0])

    @pl.when(pl.program_id(0) >= nreal_ref[0])
    def _():
        ys_ref[...] = jnp.zeros_like(ys_ref)


def _moe(xs, blk, e_lo, e_hi, n_real, wgu, wd, tm):
    n_rows, dx = xs.shape
    _, d, de2 = wgu.shape
    grid_spec = pltpu.PrefetchScalarGridSpec(
        num_scalar_prefetch=4,
        grid=(n_rows // tm,),
        in_specs=[pl.BlockSpec((tm, dx), lambda i, blk, lo, hi, nr: (blk[i], 0)),
                  pl.BlockSpec((1, d, de2), lambda i, blk, lo, hi, nr: (lo[i], 0, 0)),
                  pl.BlockSpec((1, de2 // 2, d), lambda i, blk, lo, hi, nr: (lo[i], 0, 0)),
                  pl.BlockSpec((1, d, de2), lambda i, blk, lo, hi, nr: (hi[i], 0, 0)),
                  pl.BlockSpec((1, de2 // 2, d), lambda i, blk, lo, hi, nr: (hi[i], 0, 0))],
        out_specs=pl.BlockSpec((tm, d), lambda i, blk, lo, hi, nr: (i, 0)))
    return pl.pallas_call(
        _moe_kernel,
        grid_spec=grid_spec,
        out_shape=jax.ShapeDtypeStruct((n_rows, d), F32),
        compiler_params=_cparams(("arbitrary",)),
        name="moe",
    )(blk, e_lo, e_hi, n_real, xs, wgu, wd, wgu, wd)


def _combine_kernel(slot_ref, slot_next_ref, ys_ref, h_ref, g2_ref, g_ref, o_ref, buf_ref, sem, *, final):
    i = pl.program_id(0)
    n = pl.num_programs(0)
    tm = h_ref.shape[0]

    def gather(s_ref, b):
        def issue(r, carry):
            _row_copy(ys_ref, s_ref[0, 0, r], buf_ref.at[b], r, sem.at[b]).start()
            return carry
        lax.fori_loop(0, tm, issue, 0, unroll=DMA_UNROLL)

    cur = i % 2

    @pl.when(i == 0)
    def _():
        gather(slot_ref, 0)

    @pl.when(i + 1 < n)
    def _():
        gather(slot_next_ref, 1 - cur)

    pltpu.make_async_copy(ys_ref.at[pl.ds(0, tm)], buf_ref.at[cur], sem.at[cur]).wait()
    h = h_ref[...] + g2_ref[0] * buf_ref[cur]
    if final:
        ms = jnp.mean(h * h, axis=-1, keepdims=True)
        h = (h * lax.rsqrt(ms + EPS)) * g_ref[...]
    o_ref[...] = h


def _combine(ys, slot, h, g2, g, tm, final):
    bsz, t, d = h.shape
    n = bsz * t
    nt = n // tm
    per_batch = g2.shape[0] == bsz
    slot3 = slot.reshape(nt, 1, tm)
    smem = functools.partial(pl.BlockSpec, (1, 1, tm), memory_space=pltpu.SMEM)
    return pl.pallas_call(
        functools.partial(_combine_kernel, final=final),
        grid=(nt,),
        in_specs=[smem(lambda i: (i, 0, 0)), smem(lambda i: (jnp.minimum(i + 1, nt - 1), 0, 0)),
                  pl.BlockSpec(memory_space=pl.ANY),
                  pl.BlockSpec((tm, d), lambda i: (i, 0)),
                  pl.BlockSpec((1, 1, d), (lambda i: (i * tm // t, 0, 0)) if per_batch else (lambda i: (0, 0, 0))),
                  pl.BlockSpec((1, d), lambda i: (0, 0))],
        out_specs=pl.BlockSpec((tm, d), lambda i: (i, 0)),
        out_shape=jax.ShapeDtypeStruct((n, d), F32),
        scratch_shapes=[pltpu.VMEM((2, tm, d), F32), pltpu.SemaphoreType.DMA((2,))],
        compiler_params=_cparams(("arbitrary",)),
        name="moe_combine",
    )(slot3, slot3, ys, h.reshape(n, d), g2, g.reshape(1, d)).reshape(bsz, t, d)


def _moe_layer(vx, cnt, wgu, wd, h, g2, g, final):
    bsz, t, d = h.shape
    n = bsz * t
    tmo = min(MOE_TILE, n)
    n_max = n // tmo + N_CLASSES
    vx = vx.reshape(n, d + ROUTER_LANES)
    slot, blk, e_lo, e_hi, n_real = _routing_tables(vx, cnt, d, tmo, n_max)
    xs = _dispatch(vx, slot, n_max * tmo, min(DISPATCH_TILE, n))
    ys = _moe(xs, blk, e_lo, e_hi, n_real, wgu, wd, tmo)
    return _combine(ys, slot, h, g2, g, min(DISPATCH_TILE, t), final)


def _block_diag(w):
    nh, bw, _ = w.shape
    eye = jnp.eye(nh, dtype=w.dtype)
    return (eye[:, None, :, None] * w[:, :, None, :]).reshape(nh * bw, nh * bw)


def kernel(x, c, ctx, c_ctx, w_mod, b_mod, norm1_g, norm2_g, w_in, conv_w, conv_b, conv_ln_g, conv_ln_b,
           mlstm_b_i, mlstm_b_f, mlstm_norm_g, lru_conv_w, lru_conv_b, lru_w_a, lru_b_a, lru_w_x, lru_b_x,
           lru_lambda, w_out, w_rg, b_rg, w_re, b_re, w_gate, w_up, w_down, final_g):
    bsz, t, d = x.shape
    tc = ctx.shape[1]
    depth = w_mod.shape[0]
    dc = conv_w.shape[2]
    dm = mlstm_norm_g.shape[1]
    dl = lru_lambda.shape[2]
    ngate = N_DIR * MLSTM_HEADS
    tm_c = min(TOKEN_TILE, tc)
    tb = min(SCAN_BLOCK, t)

    rp = -(-(bsz + 1) // SUBLANES) * SUBLANES
    cvec = jnp.zeros((rp, d), F32).at[:bsz].set(c).at[bsz].set(c_ctx)
    mod = _modulation(cvec, w_mod, b_mod)

    c_q = 2 * dc
    c_o = c_q + 3 * dm
    c_g = c_o + dm
    c_l = c_g + 2 * ngate
    splits = (2 * dc, 3 * dm, dm, 2 * dl, GATE_LANES)
    out_dtypes = (F32, BF16, F32, F32, F32)

    h_lat, h_ctx = x, ctx
    for l in range(depth):
        last = l == depth - 1
        transposed = l % 2 == 1
        tm = SUBLANES * (t // GRID_W) if transposed else min(TOKEN_TILE, t)
        m_lat = [mod[l, :bsz, k * d:(k + 1) * d].reshape(bsz, 1, d) for k in range(6)]
        m_ctx = [mod[l, bsz:bsz + 1, k * d:(k + 1) * d].reshape(1, 1, d) for k in range(6)]
        wl = w_in[l]
        w_perm = jnp.concatenate(
            [wl[:, :c_o], wl[:, c_o:c_g], wl[:, c_l:], wl[:, c_g:c_l],
             jnp.zeros((d, GATE_LANES - 2 * ngate), F32)], axis=1).astype(BF16)
        gate_bias = jnp.zeros((1, GATE_LANES), F32).at[0, :ngate].set(mlstm_b_i[l].reshape(-1))
        gate_bias = gate_bias.at[0, ngate:2 * ngate].set(mlstm_b_f[l].reshape(-1))
        wax = jnp.stack([jnp.concatenate([_block_diag(lru_w_a[l, dd]), _block_diag(lru_w_x[l, dd])], axis=1)
                         for dd in range(N_DIR)]).astype(BF16)
        bax = jnp.concatenate([lru_b_a[l], lru_b_x[l]], axis=-1).reshape(N_DIR, 1, 2 * dl)
        lam = lru_lambda[l].reshape(N_DIR, 1, dl)
        lcb = lru_conv_b[l].reshape(N_DIR, 1, dl)
        wr = jnp.zeros((d, ROUTER_LANES), F32).at[:, :N_GROUPS].set(w_rg[l]).at[:, N_GROUPS:N_GROUPS + N_EXPERTS].set(w_re[l])
        wrh, wrl = _split_hi_lo(wr)
        br = jnp.zeros((1, ROUTER_LANES), F32).at[0, :N_GROUPS].set(b_rg[l]).at[0, N_GROUPS:N_GROUPS + N_EXPERTS].set(b_re[l])
        wgu = jnp.concatenate([w_gate[l], w_up[l]], axis=-1).astype(BF16)
        wd = w_down[l].astype(BF16)
        wo = w_out[l].astype(BF16)

        cv_l, qkv_l, o_l, lru_l, gt_l = _in_proj(h_lat, m_lat[0], m_lat[1], norm1_g[l], w_perm, splits, out_dtypes,
                                                 tm, transposed)
        cv_c, qkv_c, o_c, lru_c, gt_c = _in_proj(h_ctx, m_ctx[0], m_ctx[1], norm1_g[l], w_perm, splits, out_dtypes,
                                                 tm_c, False)

        a_l = _conformer_conv(cv_l, conv_w[l], conv_b[l], conv_ln_g[l], conv_ln_b[l], tm)
        mcf, mcb, mlf, mlb = _mlstm(qkv_c, gt_c, qkv_l, gt_l, gate_bias, tb)
        rcf, rcb, rlf, rlb = _rglru(lru_c, lru_l, lru_conv_w[l], lcb, wax, bax, lam, tb)

        h_lat, vx_l, cnt_l = _out_proj(a_l, mlf, mlb, o_l, rlf, rlb, lru_l, h_lat, m_lat[2], m_lat[3], m_lat[4],
                                       norm2_g[l], mlstm_norm_g[l], wo, wrh, wrl, br, tm, transposed)
        h_lat = _moe_layer(vx_l, cnt_l, wgu, wd, h_lat, m_lat[5], final_g, last)
        if not last:
            a_c = _conformer_conv(cv_c, conv_w[l], conv_b[l], conv_ln_g[l], conv_ln_b[l], tm_c)
            h_ctx, vx_c, cnt_c = _out_proj(a_c, mcf, mcb, o_c, rcf, rcb, lru_c, h_ctx, m_ctx[2], m_ctx[3], m_ctx[4],
                                           norm2_g[l], mlstm_norm_g[l], wo, wrh, wrl, br, tm_c, False)
            h_ctx = _moe_layer(vx_c, cnt_c, wgu, wd, h_ctx, m_ctx[5], final_g, False)
    return h_lat
```

```python
import functools
import math

import jax
import jax.numpy as jnp
from jax import lax
from jax.experimental import pallas as pl
from jax.experimental.pallas import tpu as pltpu

EPS = 1e-6
GRID_W = 64
D_CONV_FRAC = 4
CONV_WIDTH = 31
CONV_HALO = 16
MLSTM_HEADS = 4
MLSTM_CHUNK = 128
LRU_HEADS = 4
LRU_CONV_WIDTH = 4
LRU_C = 8.0
N_DIR = 2
N_GROUPS = 4
EXPERTS_PER_GROUP = 4
N_EXPERTS = N_GROUPS * EXPERTS_PER_GROUP
N_PAIRS = EXPERTS_PER_GROUP * (EXPERTS_PER_GROUP - 1) // 2
N_CLASSES = N_GROUPS * N_PAIRS
ROUTER_LANES = 128
INFO_CLASS, INFO_RANK, INFO_W_LO, INFO_W_HI = 0, 1, 2, 3
SUBLANES = 8
LOG2E = 1.4426950408889634

VMEM_LIMIT = 56 * 1024 * 1024
TOKEN_TILE = 512
SCAN_BLOCK = 256
MOE_TILE = 256
DISPATCH_TILE = 512
DMA_UNROLL = 8

F32 = jnp.float32
BF16 = jnp.bfloat16


def _cparams(sem):
    return pltpu.CompilerParams(dimension_semantics=sem, vmem_limit_bytes=VMEM_LIMIT)


def _sigmoid(x):
    return jax.nn.sigmoid(x)


def _log_sigmoid(x):
    return jnp.minimum(x, 0.0) - jnp.log1p(jnp.exp(-jnp.abs(x)))


def _softplus(x):
    return jnp.maximum(x, 0.0) + jnp.log1p(jnp.exp(-jnp.abs(x)))


def _gelu_tanh(x):
    return x * (0.5 * (1.0 + jnp.tanh(0.7978845608028654 * (x + 0.044715 * (x * x * x)))))


def _dot(a, b):
    return jnp.dot(a, b, preferred_element_type=F32)


def _split_hi_lo(x):
    hi = x.astype(BF16)
    lo = (x - hi.astype(F32)).astype(BF16)
    return hi, lo


def _load_tile(ref, transposed):
    if not transposed:
        return ref[0]
    return jnp.concatenate([ref[0, :, w, :] for w in range(SUBLANES)], axis=0)


def _tile_spec(t, tm, d, transposed, to_natural=False):
    if not transposed:
        return pl.BlockSpec((1, tm, d), lambda b, j: (b, j, 0))
    major = GRID_W if to_natural else t // GRID_W
    assert tm == SUBLANES * major
    return pl.BlockSpec((1, major, SUBLANES, d), lambda b, j: (b, 0, j, 0))


def _tile_view(a, transposed, to_natural=False):
    if not transposed:
        return a
    b, t, d = a.shape
    return a.reshape(b, GRID_W, t // GRID_W, d) if to_natural else a.reshape(b, t // GRID_W, GRID_W, d)


def _mod_kernel(c_ref, w_ref, b_ref, o_ref):
    c = c_ref[...]
    s = (c * _sigmoid(c)).astype(BF16)
    o_ref[0] = _dot(s, w_ref[0].astype(BF16)) + b_ref[0]


def _modulation(cvec, w_mod, b_mod):
    nl, d, d6 = w_mod.shape
    rp = cvec.shape[0]
    tn = d6 // 4
    return pl.pallas_call(
        _mod_kernel,
        grid=(nl, d6 // tn),
        in_specs=[pl.BlockSpec((rp, d), lambda l, j: (0, 0)),
                  pl.BlockSpec((1, d, tn), lambda l, j: (l, 0, j)),
                  pl.BlockSpec((1, 1, tn), lambda l, j: (l, 0, j))],
        out_specs=pl.BlockSpec((1, rp, tn), lambda l, j: (l, 0, j)),
        out_shape=jax.ShapeDtypeStruct((nl, rp, d6), F32),
        compiler_params=_cparams(("arbitrary", "arbitrary")),
        name="modulation",
    )(cvec, w_mod, b_mod.reshape(nl, 1, d6))


def _chunk_scan(x, op, reverse):
    L = MLSTM_CHUNK
    n = x.shape[1]
    pos = lax.broadcasted_iota(jnp.int32, x.shape, 1) & (L - 1)
    sh = 1
    while sh < L:
        if reverse:
            xs, ok = pltpu.roll(x, n - sh, 1), pos < L - sh
        else:
            xs, ok = pltpu.roll(x, sh, 1), pos >= sh
        x = jnp.where(ok, op(x, xs), x)
        sh *= 2
    return x


def _gate_scans(g):
    nrow = g.shape[0] // 2
    fwd_row = lax.broadcasted_iota(jnp.int32, (nrow, g.shape[1]), 0) < nrow // N_DIR
    ig2 = g[:nrow] * LOG2E
    lf2 = _log_sigmoid(g[nrow:]) * LOG2E
    b2 = jnp.where(fwd_row, _chunk_scan(lf2, jnp.add, False), _chunk_scan(lf2, jnp.add, True))
    c2 = ig2 - b2
    cm2 = jnp.where(fwd_row, _chunk_scan(c2, jnp.maximum, False), _chunk_scan(c2, jnp.maximum, True))
    return jnp.concatenate([ig2, b2, cm2], axis=0)


def _in_kernel(h_ref, shift_ref, scale_ref, g_ref, w_ref, wt_ref, gbias_ref, *outs, transposed, splits, splits_t):
    h = _load_tile(h_ref, transposed)
    ms = jnp.mean(h * h, axis=-1, keepdims=True)
    u = (h * lax.rsqrt(ms + EPS)) * g_ref[...]
    u = (u * (1.0 + scale_ref[0]) + shift_ref[0]).astype(BF16)
    c0 = 0
    for o_ref, width in zip(outs, splits):
        o_ref[0] = _dot(u, w_ref[:, c0:c0 + width]).astype(o_ref.dtype)
        c0 += width
    r0 = 0
    outs_t = outs[len(splits):]
    for o_ref, width in zip(outs_t[:-1], splits_t[:-1]):
        o_ref[0] = lax.dot_general(wt_ref[r0:r0 + width, :], u, (((1,), (1,)), ((), ())),
                                   preferred_element_type=F32).astype(o_ref.dtype)
        r0 += width
    gates = lax.dot_general(wt_ref[r0:r0 + splits_t[-1], :], u, (((1,), (1,)), ((), ())), preferred_element_type=F32)
    tm = gates.shape[1]
    outs_t[-1][0] = _gate_scans(gates + jnp.concatenate([gbias_ref[...]] * (tm // MLSTM_CHUNK), axis=1))


def _in_proj(h, shift, scale, norm_g, w, wt, gate_bias, splits, out_dtypes, splits_t, out_dtypes_t, tm, transposed):
    b, t, d = h.shape
    per_batch = shift.shape[0] == b
    mod_spec = pl.BlockSpec((1, 1, d), (lambda bi, j: (bi, 0, 0)) if per_batch else (lambda bi, j: (0, 0, 0)))
    rows_t = splits_t[:-1] + (splits_t[-1] // 2 * 3,)
    return pl.pallas_call(
        functools.partial(_in_kernel, transposed=transposed, splits=splits, splits_t=splits_t),
        grid=(b, t // tm),
        in_specs=[_tile_spec(t, tm, d, transposed), mod_spec, mod_spec, pl.BlockSpec((1, d), lambda bi, j: (0, 0)),
                  pl.BlockSpec(w.shape, lambda bi, j: (0, 0)), pl.BlockSpec(wt.shape, lambda bi, j: (0, 0)),
                  pl.BlockSpec(gate_bias.shape, lambda bi, j: (0, 0))],
        out_specs=([pl.BlockSpec((1, tm, width), lambda bi, j: (bi, j, 0)) for width in splits]
                   + [pl.BlockSpec((1, width, tm), lambda bi, j: (bi, 0, j)) for width in rows_t]),
        out_shape=([jax.ShapeDtypeStruct((b, t, width), dt) for width, dt in zip(splits, out_dtypes)]
                   + [jax.ShapeDtypeStruct((b, width, t), dt) for width, dt in zip(rows_t, out_dtypes_t)]),
        compiler_params=_cparams(("arbitrary", "arbitrary")),
        name="in_proj",
    )(_tile_view(h, transposed), shift, scale, norm_g.reshape(1, d), w, wt, gate_bias)


def _conv_kernel(prev_ref, cur_ref, next_ref, w_ref, b_ref, lng_ref, lnb_ref, o_ref, *, tb, nblk, dc):
    j = pl.program_id(1)

    def glu(x):
        return x[:, :dc] * _sigmoid(x[:, dc:])

    up = jnp.where(j > 0, glu(prev_ref[0]), 0.0)
    un = jnp.where(j < nblk - 1, glu(next_ref[0]), 0.0)
    ext = jnp.concatenate([up, glu(cur_ref[0]), un], axis=0)
    base = CONV_HALO - CONV_WIDTH // 2
    acc = jnp.zeros((tb, dc), F32)
    for k in range(CONV_WIDTH):
        acc = acc + w_ref[k:k + 1, :] * ext[base + k:base + k + tb, :]
    acc = acc + b_ref[...]
    mu = jnp.mean(acc, axis=-1, keepdims=True)
    xc = acc - mu
    y = xc * lax.rsqrt(jnp.mean(xc * xc, axis=-1, keepdims=True) + EPS) * lng_ref[...] + lnb_ref[...]
    o_ref[0] = (y * _sigmoid(y)).astype(o_ref.dtype)


def _conformer_conv(cv, w, b, ln_g, ln_b, tb):
    bsz, t, c2 = cv.shape
    dc = c2 // 2
    nblk = t // tb
    hb = tb // CONV_HALO
    nh = t // CONV_HALO
    wp = jnp.zeros((CONV_WIDTH + 1, dc), F32).at[:CONV_WIDTH].set(w)
    vec = pl.BlockSpec((1, dc), lambda bi, j: (0, 0))
    return pl.pallas_call(
        functools.partial(_conv_kernel, tb=tb, nblk=nblk, dc=dc),
        grid=(bsz, nblk),
        in_specs=[pl.BlockSpec((1, CONV_HALO, c2), lambda bi, j: (bi, jnp.maximum(j * hb - 1, 0), 0)),
                  pl.BlockSpec((1, tb, c2), lambda bi, j: (bi, j, 0)),
                  pl.BlockSpec((1, CONV_HALO, c2), lambda bi, j: (bi, jnp.minimum((j + 1) * hb, nh - 1), 0)),
                  pl.BlockSpec((CONV_WIDTH + 1, dc), lambda bi, j: (0, 0)), vec, vec, vec],
        out_specs=pl.BlockSpec((1, tb, dc), lambda bi, j: (bi, j, 0)),
        out_shape=jax.ShapeDtypeStruct((bsz, t, dc), BF16),
        compiler_params=_cparams(("arbitrary", "arbitrary")),
        name="conformer_conv",
    )(cv, cv, cv, wp, b.reshape(1, dc), ln_g.reshape(1, dc), ln_b.reshape(1, dc))


def _split3(x):
    hi = x.astype(BF16).astype(F32)
    r1 = x - hi
    mid = r1.astype(BF16).astype(F32)
    lo = (r1 - mid).astype(BF16).astype(F32)
    return hi, mid, lo


def _mlstm_block(qt_ref, k_ref, vt_ref, gt_ref, out_ref, c_ref, n_ref, m_ref, d, nchunks, hd):
    L = MLSTM_CHUNK
    nh = MLSTM_HEADS
    assert hd == L
    nrow = N_DIR * nh
    log2_scale = math.log2(hd ** -0.5)
    ig2, b2, cm2 = gt_ref[0, 0:nrow], gt_ref[0, nrow:2 * nrow], gt_ref[0, 2 * nrow:3 * nrow]
    c2 = ig2 - b2
    row_s = lax.broadcasted_iota(jnp.int32, (L, L), 0)
    col_t = lax.broadcasted_iota(jnp.int32, (L, L), 1)
    mask = (row_s <= col_t) if d == 0 else (row_s >= col_t)
    rid = lax.broadcasted_iota(jnp.int32, (nrow, L), 0)
    ones16 = jnp.ones((2 * SUBLANES, L), BF16)
    for ci in (range(nchunks) if d == 0 else range(nchunks - 1, -1, -1)):
        sl = slice(ci * L, (ci + 1) * L)
        b2c, c2c, ig2c = b2[:, sl], c2[:, sl], ig2[:, sl]
        m2 = m_ref[d]
        a2 = -jnp.maximum(m2, cm2[:, sl])
        inter = jnp.exp2(m2 + a2)
        edn = jnp.exp2(a2 - b2c)
        b_last = jnp.broadcast_to(b2c[:, L - 1:L] if d == 0 else b2c[:, 0:1], (nrow, L))
        logw2 = b_last - b2c + ig2c
        m2_new = jnp.maximum(b_last + m2, jnp.broadcast_to(jnp.max(logw2, axis=-1, keepdims=True), (nrow, L)))
        wgt = jnp.exp2(logw2 - m2_new + log2_scale)
        decay = jnp.exp2(b_last + m2 - m2_new)
        m_ref[d] = m2_new
        c3 = jnp.concatenate(_split3(c2c + log2_scale), axis=0)
        a3 = jnp.concatenate(_split3(a2), axis=0)
        for h in range(nh):
            r = d * nh + h
            sel = jnp.where(rid == r, 1.0, 0.0)
            sel3 = jnp.concatenate([sel, sel, sel], axis=0)
            e = lax.dot_general(jnp.concatenate([c3, sel3], axis=0).astype(BF16),
                                jnp.concatenate([sel3, a3], axis=0).astype(BF16),
                                (((0,), (0,)), ((), ())), preferred_element_type=F32)
            p = jnp.where(mask, jnp.exp2(e), 0.0)
            qt = qt_ref[0, h * hd:(h + 1) * hd, sl]
            k = k_ref[0, ci * L:(ci + 1) * L, h * hd:(h + 1) * hd]
            vt = vt_ref[0, h * hd:(h + 1) * hd, sl]
            ct = c_ref[r]
            n16 = n_ref[r]
            kq = _dot(jnp.concatenate([k, ct.astype(BF16), n16.astype(BF16)], axis=0), qt)
            st = kq[:L] * p
            nd = _dot(vt, st.astype(BF16))
            inter_r = inter[r:r + 1]
            num = nd + inter_r * kq[L:L + hd]
            den = jnp.sum(st, axis=0, keepdims=True) + inter_r * kq[L + hd:L + hd + 1]
            ht = num * (1.0 / jnp.maximum(jnp.abs(den), edn[r:r + 1]))
            out_ref[0, ci * L:(ci + 1) * L, h * hd:(h + 1) * hd] = ht.T
            w_r = wgt[r:r + 1]
            vw = (vt.astype(F32) * w_r).astype(BF16)
            upd = _dot(jnp.concatenate([vw, ones16 * w_r.astype(BF16)], axis=0), k)
            dec = decay[r:r + 1]
            c_ref[r] = dec * ct + upd[:hd]
            n_ref[r] = dec * n16 + upd[hd:]


def _mlstm_kernel(qtc_ref, kc_ref, vtc_ref, gtc_ref, qtf_ref, kf_ref, vtf_ref, gtf_ref,
                  qtb_ref, kb_ref, vtb_ref, gtb_ref,
                  hcf_ref, hcb_ref, hf_ref, hb_ref, c_ref, n_ref, m_ref, *, nc_ctx, nc_lat, hd):
    s = pl.program_id(1)
    state = (c_ref, n_ref, m_ref)

    @pl.when(s == 0)
    def _():
        c_ref[...] = jnp.zeros_like(c_ref)
        n_ref[...] = jnp.zeros_like(n_ref)
        m_ref[...] = jnp.zeros_like(m_ref)
        _mlstm_block(qtc_ref, kc_ref, vtc_ref, gtc_ref, hcf_ref, *state, 0, nc_ctx, hd)
        _mlstm_block(qtc_ref, kc_ref, vtc_ref, gtc_ref, hcb_ref, *state, 1, nc_ctx, hd)

    @pl.when(s > 0)
    def _():
        _mlstm_block(qtf_ref, kf_ref, vtf_ref, gtf_ref, hf_ref, *state, 0, nc_lat, hd)
        _mlstm_block(qtb_ref, kb_ref, vtb_ref, gtb_ref, hb_ref, *state, 1, nc_lat, hd)


def _mlstm(ctx_in, lat_in, tb):
    qt_c, k_c, vt_c, gt_c = ctx_in
    qt_l, k_l, vt_l, gt_l = lat_in
    bsz, tc, dm = k_c.shape
    t = k_l.shape[1]
    ng = gt_l.shape[1]
    hd = dm // MLSTM_HEADS
    nb = t // tb
    nstate = N_DIR * MLSTM_HEADS

    def fwd(s):
        return jnp.maximum(s - 1, 0)

    def bwd(s):
        return nb - 1 - jnp.maximum(s - 1, 0)

    def specs(tlen, blk):
        return [pl.BlockSpec((1, dm, tlen), lambda bi, s: (bi, 0, blk(s))),
                pl.BlockSpec((1, tlen, dm), lambda bi, s: (bi, blk(s), 0)),
                pl.BlockSpec((1, dm, tlen), lambda bi, s: (bi, 0, blk(s))),
                pl.BlockSpec((1, ng, tlen), lambda bi, s: (bi, 0, blk(s)))]

    def out(tlen, blk):
        return pl.BlockSpec((1, tlen, dm), lambda bi, s: (bi, blk(s), 0))

    first = lambda s: 0
    return pl.pallas_call(
        functools.partial(_mlstm_kernel, nc_ctx=tc // MLSTM_CHUNK, nc_lat=tb // MLSTM_CHUNK, hd=hd),
        grid=(bsz, nb + 1),
        in_specs=specs(tc, first) + specs(tb, fwd) + specs(tb, bwd),
        out_specs=[out(tc, first), out(tc, first), out(tb, fwd), out(tb, bwd)],
        out_shape=[jax.ShapeDtypeStruct((bsz, tc, dm), F32), jax.ShapeDtypeStruct((bsz, tc, dm), F32),
                   jax.ShapeDtypeStruct((bsz, t, dm), F32), jax.ShapeDtypeStruct((bsz, t, dm), F32)],
        scratch_shapes=[pltpu.VMEM((nstate, hd, hd), F32), pltpu.VMEM((nstate, 2 * SUBLANES, hd), F32),
                        pltpu.VMEM((N_DIR, nstate, MLSTM_CHUNK), F32)],
        compiler_params=_cparams(("arbitrary", "arbitrary")),
        name="mlstm",
    )(*ctx_in, *lat_in, *lat_in)


def _lru_block(x, halo, d, cw, cb, wax, bax, lam, h0):
    tb, dl = x.shape
    kw = LRU_CONV_WIDTH
    if d == 0:
        ext = jnp.concatenate([halo, x], axis=0)
        taps = [ext[SUBLANES - (kw - 1) + j:SUBLANES - (kw - 1) + j + tb] for j in range(kw)]
    else:
        ext = jnp.concatenate([x, halo], axis=0)
        taps = [ext[kw - 1 - j:kw - 1 - j + tb] for j in range(kw)]
    xc = cb
    for j in range(kw):
        xc = xc + cw[j:j + 1, :] * taps[j]
    ri = _sigmoid(_dot(xc.astype(BF16), wax) + bax)
    r, i = ri[:, :dl], ri[:, dl:]
    log_a = (-LRU_C * r) * _softplus(-lam)
    a = jnp.exp(log_a)
    th = jnp.tanh(log_a)
    u = jnp.sqrt(-2.0 * th / (1.0 - th)) * (i * xc)
    sub = lax.broadcasted_iota(jnp.int32, (tb, dl), 0) & (SUBLANES - 1)
    for sh in (1, 2, 4):
        if d == 0:
            a_s, u_s, msk = pltpu.roll(a, sh, 0), pltpu.roll(u, sh, 0), sub >= sh
        else:
            a_s, u_s, msk = pltpu.roll(a, tb - sh, 0), pltpu.roll(u, tb - sh, 0), sub < SUBLANES - sh
        u = jnp.where(msk, a * u_s + u, u)
        a = jnp.where(msk, a * a_s, a)
    ngrp = tb // SUBLANES
    outs = [None] * ngrp
    carry = h0
    for j in (range(ngrp) if d == 0 else range(ngrp - 1, -1, -1)):
        hj = u[j * SUBLANES:(j + 1) * SUBLANES] + a[j * SUBLANES:(j + 1) * SUBLANES] * carry
        outs[j] = hj
        carry = hj[SUBLANES - 1:SUBLANES] if d == 0 else hj[0:1]
    return jnp.concatenate(outs, axis=0), carry


def _lru_kernel(xc_ref, xf_ref, xb_ref, cw_ref, cb_ref, wax_ref, bax_ref, lam_ref,
                hcf_ref, hcb_ref, hf_ref, hb_ref, hcar_ref, halo_ref):
    s = pl.program_id(1)

    def params(d):
        return cw_ref[d], cb_ref[d], wax_ref[d], bax_ref[d], lam_ref[d]

    @pl.when(s == 0)
    def _():
        x = xc_ref[0]
        zero_halo = jnp.zeros((SUBLANES, x.shape[1]), F32)
        zero_h = jnp.zeros((1, x.shape[1]), F32)
        for d, o_ref in ((0, hcf_ref), (1, hcb_ref)):
            h, carry = _lru_block(x, zero_halo, d, *params(d), zero_h)
            o_ref[0] = h
            hcar_ref[d] = jnp.broadcast_to(carry, hcar_ref.shape[1:])

    @pl.when(s > 0)
    def _():
        for d, x_ref, o_ref in ((0, xf_ref, hf_ref), (1, xb_ref, hb_ref)):
            x = x_ref[0]
            halo = jnp.where(s > 1, halo_ref[d], 0.0)
            h, carry = _lru_block(x, halo, d, *params(d), hcar_ref[d][0:1])
            o_ref[0] = h
            hcar_ref[d] = jnp.broadcast_to(carry, hcar_ref.shape[1:])
            halo_ref[d] = x[x.shape[0] - SUBLANES:] if d == 0 else x[:SUBLANES]


def _rglru(lru_c, lru_l, cw, cb, wax, bax, lam, tb):
    bsz, tc, w2 = lru_c.shape
    t = lru_l.shape[1]
    dl = w2 // 2
    nb = t // tb

    def fwd(bi, s):
        return (bi, jnp.maximum(s - 1, 0), 0)

    def bwd(bi, s):
        return (bi, nb - 1 - jnp.maximum(s - 1, 0), 0)

    def ctx(bi, s):
        return (bi, 0, 0)

    def whole(a):
        return pl.BlockSpec(a.shape, lambda bi, s: (0,) * a.ndim)

    return pl.pallas_call(
        _lru_kernel,
        grid=(bsz, nb + 1),
        in_specs=[pl.BlockSpec((1, tc, dl), ctx), pl.BlockSpec((1, tb, dl), fwd), pl.BlockSpec((1, tb, dl), bwd),
                  whole(cw), whole(cb), whole(wax), whole(bax), whole(lam)],
        out_specs=[pl.BlockSpec((1, tc, dl), ctx), pl.BlockSpec((1, tc, dl), ctx),
                   pl.BlockSpec((1, tb, dl), fwd), pl.BlockSpec((1, tb, dl), bwd)],
        out_shape=[jax.ShapeDtypeStruct((bsz, tc, dl), F32), jax.ShapeDtypeStruct((bsz, tc, dl), F32),
                   jax.ShapeDtypeStruct((bsz, t, dl), F32), jax.ShapeDtypeStruct((bsz, t, dl), F32)],
        scratch_shapes=[pltpu.VMEM((N_DIR, SUBLANES, dl), F32), pltpu.VMEM((N_DIR, SUBLANES, dl), F32)],
        compiler_params=_cparams(("arbitrary", "arbitrary")),
        name="rglru",
    )(lru_c, lru_l, lru_l, cw, cb, wax, bax, lam)


def _route(logits):
    ng, ne = N_GROUPS, EXPERTS_PER_GROUP
    lane = lax.broadcasted_iota(jnp.int32, logits.shape, 1).astype(F32)
    big = float(ROUTER_LANES)
    is_g = lane < ng
    gl = jnp.where(is_g, logits, -jnp.inf)
    gmax = jnp.max(gl, axis=-1, keepdims=True)
    g_sel = jnp.min(jnp.where(gl == gmax, lane, big), axis=-1, keepdims=True)
    p_g = 1.0 / jnp.sum(jnp.where(is_g, jnp.exp(logits - gmax), 0.0), axis=-1, keepdims=True)
    lo = ng + ne * g_sel
    el = jnp.where((lane >= lo) & (lane < lo + ne), logits, -jnp.inf)
    v1 = jnp.max(el, axis=-1, keepdims=True)
    i1 = jnp.min(jnp.where(el == v1, lane, big), axis=-1, keepdims=True)
    el2 = jnp.where(lane == i1, -jnp.inf, el)
    v2 = jnp.max(el2, axis=-1, keepdims=True)
    i2 = jnp.min(jnp.where(el2 == v2, lane, big), axis=-1, keepdims=True)
    e2 = jnp.exp(v2 - v1)
    w1 = p_g / (1.0 + e2)
    w2 = p_g * e2 / (1.0 + e2)
    first_lower = i1 < i2
    e_lo = jnp.minimum(i1, i2) - lo
    e_hi = jnp.maximum(i1, i2) - lo
    pair = e_lo * (2 * ne - 1 - e_lo) * 0.5 + (e_hi - e_lo - 1.0)
    return g_sel * N_PAIRS + pair, jnp.where(first_lower, w1, w2), jnp.where(first_lower, w2, w1)


def _out_kernel(a_ref, mf_ref, mb_ref, o_ref, lf_ref, lb_ref, lg_ref, h_ref, g1_ref, sh2_ref, sc2_ref,
                n2g_ref, mng_ref, wout_ref, wrh_ref, wrl_ref, br_ref,
                hnew_ref, vx_ref, cnt_ref, run_ref, *, transposed, d, hd):
    @pl.when((pl.program_id(0) == 0) & (pl.program_id(1) == 0))
    def _():
        run_ref[...] = jnp.zeros_like(run_ref)

    dc = a_ref.shape[2]
    dm = mf_ref.shape[2]
    mh = mf_ref[0] + mb_ref[0]
    parts = []
    for h in range(dm // hd):
        x = mh[:, h * hd:(h + 1) * hd]
        xc = x - jnp.mean(x, axis=-1, keepdims=True)
        parts.append(xc * lax.rsqrt(jnp.mean(xc * xc, axis=-1, keepdims=True) + EPS) * mng_ref[:, h * hd:(h + 1) * hd])
    m_out = (_sigmoid(o_ref[0]) * jnp.concatenate(parts, axis=-1)).astype(BF16)
    r_out = ((lf_ref[0] + lb_ref[0]) * _gelu_tanh(lg_ref[0])).astype(BF16)
    y = (_dot(a_ref[0], wout_ref[0:dc, :]) + _dot(m_out, wout_ref[dc:dc + dm, :])
         + _dot(r_out, wout_ref[dc + dm:, :]))
    hn = _load_tile(h_ref, transposed) + g1_ref[0] * y
    hnew_ref[0] = hn
    ms = jnp.mean(hn * hn, axis=-1, keepdims=True)
    v = (hn * lax.rsqrt(ms + EPS)) * n2g_ref[...]
    v = v * (1.0 + sc2_ref[0]) + sh2_ref[0]
    vh, vl = _split_hi_lo(v)
    logits = _dot(vh, wrh_ref[...]) + _dot(vl, wrh_ref[...]) + _dot(vh, wrl_ref[...]) + br_ref[...]
    cls, w_lo, w_hi = _route(logits)
    tm = v.shape[0]
    lane = lax.broadcasted_iota(jnp.int32, (tm, ROUTER_LANES), 1).astype(F32)
    onehot = jnp.where(lane == cls, 1.0, 0.0)
    row = lax.broadcasted_iota(jnp.int32, (tm, tm), 0)
    col = lax.broadcasted_iota(jnp.int32, (tm, tm), 1)
    before = _dot(jnp.where(col < row, 1.0, 0.0).astype(BF16), onehot.astype(BF16)) + run_ref[0:1, :]
    rank = jnp.sum(onehot * before, axis=-1, keepdims=True)
    run_ref[...] = run_ref[...] + jnp.sum(onehot, axis=0, keepdims=True)
    cnt_ref[...] = run_ref[...]
    info = (jnp.where(lane == INFO_CLASS, cls, 0.0) + jnp.where(lane == INFO_RANK, rank, 0.0)
            + jnp.where(lane == INFO_W_LO, w_lo, 0.0) + jnp.where(lane == INFO_W_HI, w_hi, 0.0))
    vx_ref[0, :, :d] = v
    vx_ref[0, :, d:] = info


def _out_proj(a, mf, mb, o, lf, lb, lru, h, g1, sh2, sc2, n2g, mng, w_out, wrh, wrl, br, tm, transposed):
    bsz, t, d = h.shape
    dc, dm, dl = a.shape[2], mf.shape[2], lf.shape[2]
    per_batch = g1.shape[0] == bsz
    mod_spec = pl.BlockSpec((1, 1, d), (lambda bi, j: (bi, 0, 0)) if per_batch else (lambda bi, j: (0, 0, 0)))

    def scan(width, blk=0):
        return pl.BlockSpec((1, tm, width), lambda bi, j: (bi, j, blk))

    def whole(x):
        return pl.BlockSpec(x.shape, lambda bi, j: (0,) * x.ndim)

    dx = d + ROUTER_LANES
    n2g, mng = n2g.reshape(1, d), mng.reshape(1, dm)
    hv = _tile_view(h, transposed)
    hnew, vx, cnt = pl.pallas_call(
        functools.partial(_out_kernel, transposed=transposed, d=d, hd=dm // MLSTM_HEADS),
        grid=(bsz, t // tm),
        in_specs=[scan(dc), scan(dm), scan(dm), scan(dm), scan(dl), scan(dl), scan(dl, 1),
                  _tile_spec(t, tm, d, transposed),
                  mod_spec, mod_spec, mod_spec, whole(n2g), whole(mng), whole(w_out), whole(wrh), whole(wrl), whole(br)],
        out_specs=[scan(d), scan(dx), pl.BlockSpec((SUBLANES, ROUTER_LANES), lambda bi, j: (0, 0))],
        out_shape=[jax.ShapeDtypeStruct((bsz, t, d), F32), jax.ShapeDtypeStruct((bsz, t, dx), F32),
                   jax.ShapeDtypeStruct((SUBLANES, ROUTER_LANES), F32)],
        scratch_shapes=[pltpu.VMEM((SUBLANES, ROUTER_LANES), F32)],
        compiler_params=_cparams(("arbitrary", "arbitrary")),
        name="out_proj",
    )(a, mf, mb, o, lf, lb, lru, hv, g1, sh2, sc2, n2g, mng, w_out, wrh, wrl, br)
    return hnew, vx, cnt


def _routing_tables(vx, cnt, d, tmo, n_max):
    cls = vx[:, d + INFO_CLASS].astype(jnp.int32)
    rank = vx[:, d + INFO_RANK].astype(jnp.int32)
    counts = cnt[0, :N_CLASSES].astype(jnp.int32)
    tiles = (counts + tmo - 1) // tmo
    tile_end = jnp.cumsum(tiles)
    tile_start = tile_end - tiles
    n_real = tile_end[N_CLASSES - 1]
    slot = jnp.take(tile_start * tmo, cls) + rank
    blk = jnp.minimum(jnp.arange(n_max, dtype=jnp.int32), n_real - 1)
    tile_cls = jnp.minimum(jnp.sum((tile_end[None, :] <= blk[:, None]).astype(jnp.int32), axis=1), N_CLASSES - 1)
    pair_lo = jnp.array([a for a in range(EXPERTS_PER_GROUP) for b in range(a + 1, EXPERTS_PER_GROUP)], jnp.int32)
    pair_hi = jnp.array([b for a in range(EXPERTS_PER_GROUP) for b in range(a + 1, EXPERTS_PER_GROUP)], jnp.int32)
    grp = tile_cls // N_PAIRS
    e_lo = grp * EXPERTS_PER_GROUP + jnp.take(pair_lo, tile_cls % N_PAIRS)
    e_hi = grp * EXPERTS_PER_GROUP + jnp.take(pair_hi, tile_cls % N_PAIRS)
    return slot, blk, e_lo, e_hi, n_real.reshape(1)


def _row_copy(src_ref, src_row, dst_ref, dst_row, sem):
    return pltpu.make_async_copy(src_ref.at[pl.ds(src_row, 1)], dst_ref.at[pl.ds(dst_row, 1)], sem)


def _dispatch_kernel(slot_ref, vx_ref, xs_in_ref, xs_ref, sem):
    del xs_in_ref
    tm = vx_ref.shape[0]

    def issue(r, carry):
        _row_copy(vx_ref, r, xs_ref, slot_ref[0, 0, r], sem).start()
        return carry

    lax.fori_loop(0, tm, issue, 0, unroll=DMA_UNROLL)
    pltpu.make_async_copy(vx_ref, xs_ref.at[pl.ds(0, tm)], sem).wait()


def _dispatch(vx, slot, n_rows, tm):
    n, dx = vx.shape
    return pl.pallas_call(
        _dispatch_kernel,
        grid=(n // tm,),
        in_specs=[pl.BlockSpec((1, 1, tm), lambda i: (i, 0, 0), memory_space=pltpu.SMEM),
                  pl.BlockSpec((tm, dx), lambda i: (i, 0)),
                  pl.BlockSpec(memory_space=pl.ANY)],
        out_specs=pl.BlockSpec(memory_space=pl.ANY),
        out_shape=jax.ShapeDtypeStruct((n_rows, dx), F32),
        scratch_shapes=[pltpu.SemaphoreType.DMA(())],
        input_output_aliases={2: 0},
        compiler_params=_cparams(("arbitrary",)),
        name="moe_dispatch",
    )(slot.reshape(n // tm, 1, tm), vx, jnp.zeros((n_rows, dx), F32))


def _expert(x, wgu, wd):
    de = wd.shape[0]
    gu = _dot(x, wgu)
    g, u = gu[:, :de], gu[:, de:]
    return _dot((g * _sigmoid(g) * u).astype(BF16), wd)


def _moe_kernel(blk_ref, elo_ref, ehi_ref, nreal_ref, xs_ref, wgu_lo_ref, wd_lo_ref, wgu_hi_ref, wd_hi_ref, ys_ref):
    del blk_ref, elo_ref, ehi_ref
    d = ys_ref.shape[1]

    @pl.when(pl.program_id(0) < nreal_ref[0])
    def _():
        x = xs_ref[:, :d].astype(BF16)
        w_lo = xs_ref[:, d + INFO_W_LO:d + INFO_W_LO + 1]
        w_hi = xs_ref[:, d + INFO_W_HI:d + INFO_W_HI + 1]
        ys_ref[...] = w_lo * _expert(x, wgu_lo_ref[0], wd_lo_ref[0]) + w_hi * _expert(x, wgu_hi_ref[0], wd_hi_ref[0])

    @pl.when(pl.program_id(0) >= nreal_ref[0])
    def _():
        ys_ref[...] = jnp.zeros_like(ys_ref)


def _moe(xs, blk, e_lo, e_hi, n_real, wgu, wd, tm):
    n_rows, dx = xs.shape
    _, d, de2 = wgu.shape
    grid_spec = pltpu.PrefetchScalarGridSpec(
        num_scalar_prefetch=4,
        grid=(n_rows // tm,),
        in_specs=[pl.BlockSpec((tm, dx), lambda i, blk, lo, hi, nr: (blk[i], 0)),
                  pl.BlockSpec((1, d, de2), lambda i, blk, lo, hi, nr: (lo[i], 0, 0)),
                  pl.BlockSpec((1, de2 // 2, d), lambda i, blk, lo, hi, nr: (lo[i], 0, 0)),
                  pl.BlockSpec((1, d, de2), lambda i, blk, lo, hi, nr: (hi[i], 0, 0)),
                  pl.BlockSpec((1, de2 // 2, d), lambda i, blk, lo, hi, nr: (hi[i], 0, 0))],
        out_specs=pl.BlockSpec((tm, d), lambda i, blk, lo, hi, nr: (i, 0)))
    return pl.pallas_call(
        _moe_kernel,
        grid_spec=grid_spec,
        out_shape=jax.ShapeDtypeStruct((n_rows, d), F32),
        compiler_params=_cparams(("arbitrary",)),
        name="moe",
    )(blk, e_lo, e_hi, n_real, xs, wgu, wd, wgu, wd)


def _combine_kernel(slot_ref, slot_next_ref, ys_ref, h_ref, g2_ref, g_ref, o_ref, buf_ref, sem, *, final, transposed):
    i = pl.program_id(0) * pl.num_programs(1) + pl.program_id(1)
    n = pl.num_programs(0) * pl.num_programs(1)
    tm = o_ref.shape[1]

    def gather(s_ref, b):
        def issue(r, carry):
            _row_copy(ys_ref, s_ref[0, 0, r], buf_ref.at[b], r, sem.at[b]).start()
            return carry
        lax.fori_loop(0, tm, issue, 0, unroll=DMA_UNROLL)

    cur = i % 2

    @pl.when(i == 0)
    def _():
        gather(slot_ref, 0)

    @pl.when(i + 1 < n)
    def _():
        gather(slot_next_ref, 1 - cur)

    pltpu.make_async_copy(ys_ref.at[pl.ds(0, tm)], buf_ref.at[cur], sem.at[cur]).wait()
    h = _load_tile(h_ref, transposed) + g2_ref[0] * buf_ref[cur]
    if final:
        ms = jnp.mean(h * h, axis=-1, keepdims=True)
        h = (h * lax.rsqrt(ms + EPS)) * g_ref[...]
    o_ref[0] = h


def _combine(ys, slot, h, g2, g, tm, final, transposed):
    bsz, t, d = h.shape
    nj = t // tm
    nt = bsz * nj
    per_batch = g2.shape[0] == bsz
    slot3 = slot.reshape(nt, 1, tm)
    smem = functools.partial(pl.BlockSpec, (1, 1, tm), memory_space=pltpu.SMEM)
    return pl.pallas_call(
        functools.partial(_combine_kernel, final=final, transposed=transposed),
        grid=(bsz, nj),
        in_specs=[smem(lambda bi, j: (bi * nj + j, 0, 0)),
                  smem(lambda bi, j: (jnp.minimum(bi * nj + j + 1, nt - 1), 0, 0)),
                  pl.BlockSpec(memory_space=pl.ANY),
                  _tile_spec(t, tm, d, transposed, to_natural=True),
                  pl.BlockSpec((1, 1, d), (lambda bi, j: (bi, 0, 0)) if per_batch else (lambda bi, j: (0, 0, 0))),
                  pl.BlockSpec((1, d), lambda bi, j: (0, 0))],
        out_specs=pl.BlockSpec((1, tm, d), lambda bi, j: (bi, j, 0)),
        out_shape=jax.ShapeDtypeStruct((bsz, t, d), F32),
        scratch_shapes=[pltpu.VMEM((2, tm, d), F32), pltpu.SemaphoreType.DMA((2,))],
        compiler_params=_cparams(("arbitrary", "arbitrary")),
        name="moe_combine",
    )(slot3, slot3, ys, _tile_view(h, transposed, to_natural=True), g2, g.reshape(1, d))


def _moe_layer(vx, cnt, wgu, wd, h, g2, g, final, transposed):
    bsz, t, d = h.shape
    n = bsz * t
    tmo = min(MOE_TILE, n)
    n_max = n // tmo + N_CLASSES
    vx = vx.reshape(n, d + ROUTER_LANES)
    slot, blk, e_lo, e_hi, n_real = _routing_tables(vx, cnt, d, tmo, n_max)
    xs = _dispatch(vx, slot, n_max * tmo, min(DISPATCH_TILE, n))
    ys = _moe(xs, blk, e_lo, e_hi, n_real, wgu, wd, tmo)
    if transposed:
        slot = slot.reshape(bsz, GRID_W, t // GRID_W).transpose(0, 2, 1)
        tm = SUBLANES * GRID_W
    else:
        tm = min(DISPATCH_TILE, t)
    return _combine(ys, slot, h, g2, g, tm, final, transposed)


def _block_diag(w):
    nh, bw, _ = w.shape
    eye = jnp.eye(nh, dtype=w.dtype)
    return (eye[:, None, :, None] * w[:, :, None, :]).reshape(nh * bw, nh * bw)


def kernel(x, c, ctx, c_ctx, w_mod, b_mod, norm1_g, norm2_g, w_in, conv_w, conv_b, conv_ln_g, conv_ln_b,
           mlstm_b_i, mlstm_b_f, mlstm_norm_g, lru_conv_w, lru_conv_b, lru_w_a, lru_b_a, lru_w_x, lru_b_x,
           lru_lambda, w_out, w_rg, b_rg, w_re, b_re, w_gate, w_up, w_down, final_g):
    bsz, t, d = x.shape
    tc = ctx.shape[1]
    depth = w_mod.shape[0]
    dc = conv_w.shape[2]
    dm = mlstm_norm_g.shape[1]
    dl = lru_lambda.shape[2]
    ngate = N_DIR * MLSTM_HEADS
    tm_c = min(TOKEN_TILE, tc)
    tb = min(SCAN_BLOCK, t)

    rp = -(-(bsz + 1) // SUBLANES) * SUBLANES
    cvec = jnp.zeros((rp, d), F32).at[:bsz].set(c).at[bsz].set(c_ctx)
    mod = _modulation(cvec, w_mod, b_mod)

    c_q = 2 * dc
    c_k = c_q + dm
    c_v = c_k + dm
    c_o = c_v + dm
    c_g = c_o + dm
    c_l = c_g + 2 * ngate
    splits = (2 * dc, dm, dm, 2 * dl)
    out_dtypes = (F32, BF16, F32, F32)
    splits_t = (dm, dm, 2 * ngate)
    out_dtypes_t = (BF16, BF16, F32)

    h_lat, h_ctx = x, ctx
    for l in range(depth):
        last = l == depth - 1
        transposed = l % 2 == 1
        tm = SUBLANES * (t // GRID_W) if transposed else min(TOKEN_TILE, t)
        m_lat = [mod[l, :bsz, k * d:(k + 1) * d].reshape(bsz, 1, d) for k in range(6)]
        m_ctx = [mod[l, bsz:bsz + 1, k * d:(k + 1) * d].reshape(1, 1, d) for k in range(6)]
        wl = w_in[l]
        w_tok = jnp.concatenate([wl[:, :c_q], wl[:, c_k:c_v], wl[:, c_o:c_g], wl[:, c_l:]], axis=1).astype(BF16)
        w_chn = jnp.concatenate([wl[:, c_q:c_k], wl[:, c_v:c_o], wl[:, c_g:c_l]], axis=1).T.astype(BF16)
        gate_bias = jnp.broadcast_to(
            jnp.concatenate([mlstm_b_i[l].reshape(-1), mlstm_b_f[l].reshape(-1)])[:, None], (2 * ngate, MLSTM_CHUNK))
        wax = jnp.stack([jnp.concatenate([_block_diag(lru_w_a[l, dd]), _block_diag(lru_w_x[l, dd])], axis=1)
                         for dd in range(N_DIR)]).astype(BF16)
        bax = jnp.concatenate([lru_b_a[l], lru_b_x[l]], axis=-1).reshape(N_DIR, 1, 2 * dl)
        lam = lru_lambda[l].reshape(N_DIR, 1, dl)
        lcb = lru_conv_b[l].reshape(N_DIR, 1, dl)
        wr = jnp.zeros((d, ROUTER_LANES), F32).at[:, :N_GROUPS].set(w_rg[l]).at[:, N_GROUPS:N_GROUPS + N_EXPERTS].set(w_re[l])
        wrh, wrl = _split_hi_lo(wr)
        br = jnp.zeros((1, ROUTER_LANES), F32).at[0, :N_GROUPS].set(b_rg[l]).at[0, N_GROUPS:N_GROUPS + N_EXPERTS].set(b_re[l])
        wgu = jnp.concatenate([w_gate[l], w_up[l]], axis=-1).astype(BF16)
        wd = w_down[l].astype(BF16)
        wo = w_out[l].astype(BF16)

        proj = functools.partial(_in_proj, norm_g=norm1_g[l], w=w_tok, wt=w_chn, gate_bias=gate_bias, splits=splits,
                                 out_dtypes=out_dtypes, splits_t=splits_t, out_dtypes_t=out_dtypes_t)
        cv_l, k_l, o_l, lru_l, qt_l, vt_l, gt_l = proj(h_lat, m_lat[0], m_lat[1], tm=tm, transposed=transposed)
        cv_c, k_c, o_c, lru_c, qt_c, vt_c, gt_c = proj(h_ctx, m_ctx[0], m_ctx[1], tm=tm_c, transposed=False)

        a_l = _conformer_conv(cv_l, conv_w[l], conv_b[l], conv_ln_g[l], conv_ln_b[l], tm)
        mcf, mcb, mlf, mlb = _mlstm((qt_c, k_c, vt_c, gt_c), (qt_l, k_l, vt_l, gt_l), tb)
        rcf, rcb, rlf, rlb = _rglru(lru_c, lru_l, lru_conv_w[l], lcb, wax, bax, lam, tb)

        h_lat, vx_l, cnt_l = _out_proj(a_l, mlf, mlb, o_l, rlf, rlb, lru_l, h_lat, m_lat[2], m_lat[3], m_lat[4],
                                       norm2_g[l], mlstm_norm_g[l], wo, wrh, wrl, br, tm, transposed)
        h_lat = _moe_layer(vx_l, cnt_l, wgu, wd, h_lat, m_lat[5], final_g, last, transposed)
        if not last:
            a_c = _conformer_conv(cv_c, conv_w[l], conv_b[l], conv_ln_g[l], conv_ln_b[l], tm_c)
            h_ctx, vx_c, cnt_c = _out_proj(a_c, mcf, mcb, o_c, rcf, rcb, lru_c, h_ctx, m_ctx[2], m_ctx[3], m_ctx[4],
                                           norm2_g[l], mlstm_norm_g[l], wo, wrh, wrl, br, tm_c, False)
            h_ctx = _moe_layer(vx_c, cnt_c, wgu, wd, h_ctx, m_ctx[5], final_g, False, False)
    return h_lat
```

```python
import functools
import math

import jax
import jax.numpy as jnp
from jax import lax
from jax.experimental import pallas as pl
from jax.experimental.pallas import tpu as pltpu

EPS = 1e-6
GRID_W = 64
D_CONV_FRAC = 4
CONV_WIDTH = 31
CONV_HALO = 16
MLSTM_HEADS = 4
MLSTM_CHUNK = 128
LRU_HEADS = 4
LRU_CONV_WIDTH = 4
LRU_C = 8.0
N_DIR = 2
N_GROUPS = 4
EXPERTS_PER_GROUP = 4
N_EXPERTS = N_GROUPS * EXPERTS_PER_GROUP
N_PAIRS = EXPERTS_PER_GROUP * (EXPERTS_PER_GROUP - 1) // 2
N_CLASSES = N_GROUPS * N_PAIRS
ROUTER_LANES = 128
INFO_CLASS, INFO_RANK, INFO_W_LO, INFO_W_HI = 0, 1, 2, 3
SUBLANES = 8
LOG2E = 1.4426950408889634

VMEM_LIMIT = 56 * 1024 * 1024
TOKEN_TILE = 512
SCAN_BLOCK = 256
MOE_TILE = 512
MOE_TILE_MIN = 128
DISPATCH_TILE = 512
N_DMA_PRIORITIES = 2
DMA_UNROLL = 4

F32 = jnp.float32
BF16 = jnp.bfloat16


def _cparams(sem):
    return pltpu.CompilerParams(dimension_semantics=sem, vmem_limit_bytes=VMEM_LIMIT)


def _sigmoid(x):
    return jax.nn.sigmoid(x)


def _log_sigmoid(x):
    return jnp.minimum(x, 0.0) - jnp.log1p(jnp.exp(-jnp.abs(x)))


def _softplus(x):
    return jnp.maximum(x, 0.0) + jnp.log1p(jnp.exp(-jnp.abs(x)))


def _gelu_tanh(x):
    return x * (0.5 * (1.0 + jnp.tanh(0.7978845608028654 * (x + 0.044715 * (x * x * x)))))


def _dot(a, b):
    return jnp.dot(a, b, preferred_element_type=F32)


def _split_hi_lo(x):
    hi = x.astype(BF16)
    lo = (x - hi.astype(F32)).astype(BF16)
    return hi, lo


def _load_tile(ref, transposed):
    if not transposed:
        return ref[0]
    return jnp.concatenate([ref[0, :, w, :] for w in range(SUBLANES)], axis=0)


def _tile_spec(t, tm, d, transposed, to_natural=False):
    if not transposed:
        return pl.BlockSpec((1, tm, d), lambda b, j: (b, j, 0))
    major = GRID_W if to_natural else t // GRID_W
    assert tm == SUBLANES * major
    return pl.BlockSpec((1, major, SUBLANES, d), lambda b, j: (b, 0, j, 0))


def _tile_view(a, transposed, to_natural=False):
    if not transposed:
        return a
    b, t, d = a.shape
    return a.reshape(b, GRID_W, t // GRID_W, d) if to_natural else a.reshape(b, t // GRID_W, GRID_W, d)


def _mod_kernel(c_ref, w_ref, b_ref, o_ref):
    c = c_ref[...]
    s = (c * _sigmoid(c)).astype(BF16)
    o_ref[0] = _dot(s, w_ref[0].astype(BF16)) + b_ref[0]


def _modulation(cvec, w_mod, b_mod):
    nl, d, d6 = w_mod.shape
    rp = cvec.shape[0]
    tn = d6 // 4
    return pl.pallas_call(
        _mod_kernel,
        grid=(nl, d6 // tn),
        in_specs=[pl.BlockSpec((rp, d), lambda l, j: (0, 0)),
                  pl.BlockSpec((1, d, tn), lambda l, j: (l, 0, j)),
                  pl.BlockSpec((1, 1, tn), lambda l, j: (l, 0, j))],
        out_specs=pl.BlockSpec((1, rp, tn), lambda l, j: (l, 0, j)),
        out_shape=jax.ShapeDtypeStruct((nl, rp, d6), F32),
        compiler_params=_cparams(("arbitrary", "arbitrary")),
        name="modulation",
    )(cvec, w_mod, b_mod.reshape(nl, 1, d6))


def _chunk_scan(x, op, reverse):
    L = MLSTM_CHUNK
    n = x.shape[1]
    pos = lax.broadcasted_iota(jnp.int32, x.shape, 1) & (L - 1)
    sh = 1
    while sh < L:
        if reverse:
            xs, ok = pltpu.roll(x, n - sh, 1), pos < L - sh
        else:
            xs, ok = pltpu.roll(x, sh, 1), pos >= sh
        x = jnp.where(ok, op(x, xs), x)
        sh *= 2
    return x


def _gate_scans(g):
    nrow = g.shape[0] // 2
    fwd_row = lax.broadcasted_iota(jnp.int32, (nrow, g.shape[1]), 0) < nrow // N_DIR
    ig2 = g[:nrow] * LOG2E
    lf2 = _log_sigmoid(g[nrow:]) * LOG2E
    b2 = jnp.where(fwd_row, _chunk_scan(lf2, jnp.add, False), _chunk_scan(lf2, jnp.add, True))
    c2 = ig2 - b2
    cm2 = jnp.where(fwd_row, _chunk_scan(c2, jnp.maximum, False), _chunk_scan(c2, jnp.maximum, True))
    return jnp.concatenate([ig2, b2, cm2], axis=0)


def _in_kernel(h_ref, shift_ref, scale_ref, g_ref, w_ref, wt_ref, gbias_ref, *outs, transposed, splits, splits_t):
    h = _load_tile(h_ref, transposed)
    ms = jnp.mean(h * h, axis=-1, keepdims=True)
    u = (h * lax.rsqrt(ms + EPS)) * g_ref[...]
    u = (u * (1.0 + scale_ref[0]) + shift_ref[0]).astype(BF16)
    c0 = 0
    for o_ref, width in zip(outs, splits):
        o_ref[0] = _dot(u, w_ref[:, c0:c0 + width]).astype(o_ref.dtype)
        c0 += width
    r0 = 0
    outs_t = outs[len(splits):]
    for o_ref, width in zip(outs_t[:-1], splits_t[:-1]):
        o_ref[0] = lax.dot_general(wt_ref[r0:r0 + width, :], u, (((1,), (1,)), ((), ())),
                                   preferred_element_type=F32).astype(o_ref.dtype)
        r0 += width
    gates = lax.dot_general(wt_ref[r0:r0 + splits_t[-1], :], u, (((1,), (1,)), ((), ())), preferred_element_type=F32)
    tm = gates.shape[1]
    outs_t[-1][0] = _gate_scans(gates + jnp.concatenate([gbias_ref[...]] * (tm // MLSTM_CHUNK), axis=1))


def _in_proj(h, shift, scale, norm_g, w, wt, gate_bias, splits, out_dtypes, splits_t, out_dtypes_t, tm, transposed):
    b, t, d = h.shape
    per_batch = shift.shape[0] == b
    mod_spec = pl.BlockSpec((1, 1, d), (lambda bi, j: (bi, 0, 0)) if per_batch else (lambda bi, j: (0, 0, 0)))
    rows_t = splits_t[:-1] + (splits_t[-1] // 2 * 3,)
    return pl.pallas_call(
        functools.partial(_in_kernel, transposed=transposed, splits=splits, splits_t=splits_t),
        grid=(b, t // tm),
        in_specs=[_tile_spec(t, tm, d, transposed), mod_spec, mod_spec, pl.BlockSpec((1, d), lambda bi, j: (0, 0)),
                  pl.BlockSpec(w.shape, lambda bi, j: (0, 0)), pl.BlockSpec(wt.shape, lambda bi, j: (0, 0)),
                  pl.BlockSpec(gate_bias.shape, lambda bi, j: (0, 0))],
        out_specs=([pl.BlockSpec((1, tm, width), lambda bi, j: (bi, j, 0)) for width in splits]
                   + [pl.BlockSpec((1, width, tm), lambda bi, j: (bi, 0, j)) for width in rows_t]),
        out_shape=([jax.ShapeDtypeStruct((b, t, width), dt) for width, dt in zip(splits, out_dtypes)]
                   + [jax.ShapeDtypeStruct((b, width, t), dt) for width, dt in zip(rows_t, out_dtypes_t)]),
        compiler_params=_cparams(("arbitrary", "arbitrary")),
        name="in_proj",
    )(_tile_view(h, transposed), shift, scale, norm_g.reshape(1, d), w, wt, gate_bias)


def _conv_kernel(prev_ref, cur_ref, next_ref, w_ref, b_ref, lng_ref, lnb_ref, o_ref, *, tb, nblk, dc):
    j = pl.program_id(1)

    def glu(x):
        return x[:, :dc] * _sigmoid(x[:, dc:])

    up = jnp.where(j > 0, glu(prev_ref[0]), 0.0)
    un = jnp.where(j < nblk - 1, glu(next_ref[0]), 0.0)
    ext = jnp.concatenate([up, glu(cur_ref[0]), un], axis=0)
    base = CONV_HALO - CONV_WIDTH // 2
    nrows = tb + 2 * CONV_HALO
    acc = jnp.zeros((tb, dc), F32)
    for b in range(SUBLANES):
        shifted = ext if b == 0 else pltpu.roll(ext, nrows - b, 0)
        for a in range((base + CONV_WIDTH - 1 - b) // SUBLANES + 1):
            k = SUBLANES * a + b - base
            if 0 <= k < CONV_WIDTH:
                acc = acc + w_ref[k:k + 1, :] * shifted[SUBLANES * a:SUBLANES * a + tb, :]
    acc = acc + b_ref[...]
    mu = jnp.mean(acc, axis=-1, keepdims=True)
    xc = acc - mu
    y = xc * lax.rsqrt(jnp.mean(xc * xc, axis=-1, keepdims=True) + EPS) * lng_ref[...] + lnb_ref[...]
    o_ref[0] = (y * _sigmoid(y)).astype(o_ref.dtype)


def _conformer_conv(cv, w, b, ln_g, ln_b, tb):
    bsz, t, c2 = cv.shape
    dc = c2 // 2
    nblk = t // tb
    hb = tb // CONV_HALO
    nh = t // CONV_HALO
    wp = jnp.zeros((CONV_WIDTH + 1, dc), F32).at[:CONV_WIDTH].set(w)
    vec = pl.BlockSpec((1, dc), lambda bi, j: (0, 0))
    return pl.pallas_call(
        functools.partial(_conv_kernel, tb=tb, nblk=nblk, dc=dc),
        grid=(bsz, nblk),
        in_specs=[pl.BlockSpec((1, CONV_HALO, c2), lambda bi, j: (bi, jnp.maximum(j * hb - 1, 0), 0)),
                  pl.BlockSpec((1, tb, c2), lambda bi, j: (bi, j, 0)),
                  pl.BlockSpec((1, CONV_HALO, c2), lambda bi, j: (bi, jnp.minimum((j + 1) * hb, nh - 1), 0)),
                  pl.BlockSpec((CONV_WIDTH + 1, dc), lambda bi, j: (0, 0)), vec, vec, vec],
        out_specs=pl.BlockSpec((1, tb, dc), lambda bi, j: (bi, j, 0)),
        out_shape=jax.ShapeDtypeStruct((bsz, t, dc), BF16),
        compiler_params=_cparams(("arbitrary", "arbitrary")),
        name="conformer_conv",
    )(cv, cv, cv, wp, b.reshape(1, dc), ln_g.reshape(1, dc), ln_b.reshape(1, dc))


def _mlstm_block(qt_ref, k_ref, vt_ref, gt_ref, out_ref, c_ref, n_ref, m_ref, d, nchunks, hd):
    L = MLSTM_CHUNK
    nh = MLSTM_HEADS
    assert hd == L
    nrow = N_DIR * nh
    log2_scale = math.log2(hd ** -0.5)
    ig2, b2, cm2 = gt_ref[0, 0:nrow], gt_ref[0, nrow:2 * nrow], gt_ref[0, 2 * nrow:3 * nrow]
    c2 = ig2 - b2
    row_s = lax.broadcasted_iota(jnp.int32, (L, L), 0)
    col_t = lax.broadcasted_iota(jnp.int32, (L, L), 1)
    mask = (row_s <= col_t) if d == 0 else (row_s >= col_t)
    ones16 = jnp.ones((2 * SUBLANES, L), BF16)
    for ci in (range(nchunks) if d == 0 else range(nchunks - 1, -1, -1)):
        sl = slice(ci * L, (ci + 1) * L)
        b2c, c2c, ig2c = b2[:, sl], c2[:, sl], ig2[:, sl]
        m2 = m_ref[d]
        a2 = -jnp.maximum(m2, cm2[:, sl])
        inter = jnp.exp2(m2 + a2)
        edn = jnp.exp2(a2 - b2c)
        b_last = jnp.broadcast_to(b2c[:, L - 1:L] if d == 0 else b2c[:, 0:1], (nrow, L))
        logw2 = b_last - b2c + ig2c
        m2_new = jnp.maximum(b_last + m2, jnp.broadcast_to(jnp.max(logw2, axis=-1, keepdims=True), (nrow, L)))
        wgt = jnp.exp2(logw2 - m2_new + log2_scale)
        decay = jnp.exp2(b_last + m2 - m2_new)
        m_ref[d] = m2_new
        c_cols = (c2c + log2_scale).T
        for h in range(nh):
            r = d * nh + h
            e = c_cols[:, r:r + 1] + a2[r:r + 1]
            p = jnp.where(mask, jnp.exp2(e), 0.0)
            qt = qt_ref[0, h * hd:(h + 1) * hd, sl]
            k = k_ref[0, ci * L:(ci + 1) * L, h * hd:(h + 1) * hd]
            vt = vt_ref[0, h * hd:(h + 1) * hd, sl]
            ct = c_ref[r]
            n16 = n_ref[r]
            kq = _dot(jnp.concatenate([k, ct.astype(BF16)], axis=0), qt)
            nq = _dot(n16.astype(BF16), qt)
            st = kq[:L] * p
            nd = _dot(vt, st.astype(BF16))
            inter_r = inter[r:r + 1]
            num = nd + inter_r * kq[L:]
            den = jnp.sum(st, axis=0, keepdims=True) + inter_r * nq[0:1]
            ht = num * (1.0 / jnp.maximum(jnp.abs(den), edn[r:r + 1]))
            out_ref[0, ci * L:(ci + 1) * L, h * hd:(h + 1) * hd] = ht.T
            w_r = wgt[r:r + 1]
            vw = (vt.astype(F32) * w_r).astype(BF16)
            upd = _dot(jnp.concatenate([vw, ones16 * w_r.astype(BF16)], axis=0), k)
            dec = decay[r:r + 1]
            c_ref[r] = dec * ct + upd[:hd]
            n_ref[r] = dec * n16 + upd[hd:]


def _mlstm_kernel(qtc_ref, kc_ref, vtc_ref, gtc_ref, qtf_ref, kf_ref, vtf_ref, gtf_ref,
                  qtb_ref, kb_ref, vtb_ref, gtb_ref,
                  hcf_ref, hcb_ref, hf_ref, hb_ref, c_ref, n_ref, m_ref, *, nc_ctx, nc_lat, hd):
    s = pl.program_id(1)
    state = (c_ref, n_ref, m_ref)

    @pl.when(s == 0)
    def _():
        c_ref[...] = jnp.zeros_like(c_ref)
        n_ref[...] = jnp.zeros_like(n_ref)
        m_ref[...] = jnp.zeros_like(m_ref)
        _mlstm_block(qtc_ref, kc_ref, vtc_ref, gtc_ref, hcf_ref, *state, 0, nc_ctx, hd)
        _mlstm_block(qtc_ref, kc_ref, vtc_ref, gtc_ref, hcb_ref, *state, 1, nc_ctx, hd)

    @pl.when(s > 0)
    def _():
        _mlstm_block(qtf_ref, kf_ref, vtf_ref, gtf_ref, hf_ref, *state, 0, nc_lat, hd)
        _mlstm_block(qtb_ref, kb_ref, vtb_ref, gtb_ref, hb_ref, *state, 1, nc_lat, hd)


def _mlstm(ctx_in, lat_in, tb):
    qt_c, k_c, vt_c, gt_c = ctx_in
    qt_l, k_l, vt_l, gt_l = lat_in
    bsz, tc, dm = k_c.shape
    t = k_l.shape[1]
    ng = gt_l.shape[1]
    hd = dm // MLSTM_HEADS
    nb = t // tb
    nstate = N_DIR * MLSTM_HEADS

    def fwd(s):
        return jnp.maximum(s - 1, 0)

    def bwd(s):
        return nb - 1 - jnp.maximum(s - 1, 0)

    def specs(tlen, blk):
        return [pl.BlockSpec((1, dm, tlen), lambda bi, s: (bi, 0, blk(s))),
                pl.BlockSpec((1, tlen, dm), lambda bi, s: (bi, blk(s), 0)),
                pl.BlockSpec((1, dm, tlen), lambda bi, s: (bi, 0, blk(s))),
                pl.BlockSpec((1, ng, tlen), lambda bi, s: (bi, 0, blk(s)))]

    def out(tlen, blk):
        return pl.BlockSpec((1, tlen, dm), lambda bi, s: (bi, blk(s), 0))

    first = lambda s: 0
    return pl.pallas_call(
        functools.partial(_mlstm_kernel, nc_ctx=tc // MLSTM_CHUNK, nc_lat=tb // MLSTM_CHUNK, hd=hd),
        grid=(bsz, nb + 1),
        in_specs=specs(tc, first) + specs(tb, fwd) + specs(tb, bwd),
        out_specs=[out(tc, first), out(tc, first), out(tb, fwd), out(tb, bwd)],
        out_shape=[jax.ShapeDtypeStruct((bsz, tc, dm), F32), jax.ShapeDtypeStruct((bsz, tc, dm), F32),
                   jax.ShapeDtypeStruct((bsz, t, dm), F32), jax.ShapeDtypeStruct((bsz, t, dm), F32)],
        scratch_shapes=[pltpu.VMEM((nstate, hd, hd), F32), pltpu.VMEM((nstate, 2 * SUBLANES, hd), F32),
                        pltpu.VMEM((N_DIR, nstate, MLSTM_CHUNK), F32)],
        compiler_params=_cparams(("arbitrary", "arbitrary")),
        name="mlstm",
    )(*ctx_in, *lat_in, *lat_in)


def _lru_block(x, halo, d, cw, cb, wax, bax, lam, h0):
    tb, dl = x.shape
    kw = LRU_CONV_WIDTH
    if d == 0:
        ext = jnp.concatenate([halo, x], axis=0)
        taps = [ext[SUBLANES - (kw - 1) + j:SUBLANES - (kw - 1) + j + tb] for j in range(kw)]
    else:
        ext = jnp.concatenate([x, halo], axis=0)
        taps = [ext[kw - 1 - j:kw - 1 - j + tb] for j in range(kw)]
    xc = cb
    for j in range(kw):
        xc = xc + cw[j:j + 1, :] * taps[j]
    ri = _sigmoid(_dot(xc.astype(BF16), wax) + bax)
    r, i = ri[:, :dl], ri[:, dl:]
    log_a = (-LRU_C * r) * _softplus(-lam)
    a = jnp.exp(log_a)
    th = jnp.tanh(log_a)
    u = jnp.sqrt(-2.0 * th / (1.0 - th)) * (i * xc)
    sub = lax.broadcasted_iota(jnp.int32, (tb, dl), 0) & (SUBLANES - 1)
    for sh in (1, 2, 4):
        if d == 0:
            a_s, u_s, msk = pltpu.roll(a, sh, 0), pltpu.roll(u, sh, 0), sub >= sh
        else:
            a_s, u_s, msk = pltpu.roll(a, tb - sh, 0), pltpu.roll(u, tb - sh, 0), sub < SUBLANES - sh
        u = jnp.where(msk, a * u_s + u, u)
        a = jnp.where(msk, a * a_s, a)
    ngrp = tb // SUBLANES
    outs = [None] * ngrp
    carry = h0
    for j in (range(ngrp) if d == 0 else range(ngrp - 1, -1, -1)):
        hj = u[j * SUBLANES:(j + 1) * SUBLANES] + a[j * SUBLANES:(j + 1) * SUBLANES] * carry
        outs[j] = hj
        carry = hj[SUBLANES - 1:SUBLANES] if d == 0 else hj[0:1]
    return jnp.concatenate(outs, axis=0), carry


def _lru_kernel(xc_ref, xf_ref, xb_ref, cw_ref, cb_ref, wax_ref, bax_ref, lam_ref,
                hcf_ref, hcb_ref, hf_ref, hb_ref, hcar_ref, halo_ref):
    s = pl.program_id(1)

    def params(d):
        return cw_ref[d], cb_ref[d], wax_ref[d], bax_ref[d], lam_ref[d]

    @pl.when(s == 0)
    def _():
        x = xc_ref[0]
        zero_halo = jnp.zeros((SUBLANES, x.shape[1]), F32)
        zero_h = jnp.zeros((1, x.shape[1]), F32)
        for d, o_ref in ((0, hcf_ref), (1, hcb_ref)):
            h, carry = _lru_block(x, zero_halo, d, *params(d), zero_h)
            o_ref[0] = h
            hcar_ref[d] = jnp.broadcast_to(carry, hcar_ref.shape[1:])

    @pl.when(s > 0)
    def _():
        for d, x_ref, o_ref in ((0, xf_ref, hf_ref), (1, xb_ref, hb_ref)):
            x = x_ref[0]
            halo = jnp.where(s > 1, halo_ref[d], 0.0)
            h, carry = _lru_block(x, halo, d, *params(d), hcar_ref[d][0:1])
            o_ref[0] = h
            hcar_ref[d] = jnp.broadcast_to(carry, hcar_ref.shape[1:])
            halo_ref[d] = x[x.shape[0] - SUBLANES:] if d == 0 else x[:SUBLANES]


def _rglru(lru_c, lru_l, cw, cb, wax, bax, lam, tb):
    bsz, tc, w2 = lru_c.shape
    t = lru_l.shape[1]
    dl = w2 // 2
    nb = t // tb

    def fwd(bi, s):
        return (bi, jnp.maximum(s - 1, 0), 0)

    def bwd(bi, s):
        return (bi, nb - 1 - jnp.maximum(s - 1, 0), 0)

    def ctx(bi, s):
        return (bi, 0, 0)

    def whole(a):
        return pl.BlockSpec(a.shape, lambda bi, s: (0,) * a.ndim)

    return pl.pallas_call(
        _lru_kernel,
        grid=(bsz, nb + 1),
        in_specs=[pl.BlockSpec((1, tc, dl), ctx), pl.BlockSpec((1, tb, dl), fwd), pl.BlockSpec((1, tb, dl), bwd),
                  whole(cw), whole(cb), whole(wax), whole(bax), whole(lam)],
        out_specs=[pl.BlockSpec((1, tc, dl), ctx), pl.BlockSpec((1, tc, dl), ctx),
                   pl.BlockSpec((1, tb, dl), fwd), pl.BlockSpec((1, tb, dl), bwd)],
        out_shape=[jax.ShapeDtypeStruct((bsz, tc, dl), F32), jax.ShapeDtypeStruct((bsz, tc, dl), F32),
                   jax.ShapeDtypeStruct((bsz, t, dl), F32), jax.ShapeDtypeStruct((bsz, t, dl), F32)],
        scratch_shapes=[pltpu.VMEM((N_DIR, SUBLANES, dl), F32), pltpu.VMEM((N_DIR, SUBLANES, dl), F32)],
        compiler_params=_cparams(("arbitrary", "arbitrary")),
        name="rglru",
    )(lru_c, lru_l, lru_l, cw, cb, wax, bax, lam)


def _route(logits):
    ng, ne = N_GROUPS, EXPERTS_PER_GROUP
    lane = lax.broadcasted_iota(jnp.int32, logits.shape, 1).astype(F32)
    big = float(ROUTER_LANES)
    is_g = lane < ng
    gl = jnp.where(is_g, logits, -jnp.inf)
    gmax = jnp.max(gl, axis=-1, keepdims=True)
    g_sel = jnp.min(jnp.where(gl == gmax, lane, big), axis=-1, keepdims=True)
    p_g = 1.0 / jnp.sum(jnp.where(is_g, jnp.exp(logits - gmax), 0.0), axis=-1, keepdims=True)
    lo = ng + ne * g_sel
    el = jnp.where((lane >= lo) & (lane < lo + ne), logits, -jnp.inf)
    v1 = jnp.max(el, axis=-1, keepdims=True)
    i1 = jnp.min(jnp.where(el == v1, lane, big), axis=-1, keepdims=True)
    el2 = jnp.where(lane == i1, -jnp.inf, el)
    v2 = jnp.max(el2, axis=-1, keepdims=True)
    i2 = jnp.min(jnp.where(el2 == v2, lane, big), axis=-1, keepdims=True)
    e2 = jnp.exp(v2 - v1)
    w1 = p_g / (1.0 + e2)
    w2 = p_g * e2 / (1.0 + e2)
    first_lower = i1 < i2
    e_lo = jnp.minimum(i1, i2) - lo
    e_hi = jnp.maximum(i1, i2) - lo
    pair = e_lo * (2 * ne - 1 - e_lo) * 0.5 + (e_hi - e_lo - 1.0)
    return g_sel * N_PAIRS + pair, jnp.where(first_lower, w1, w2), jnp.where(first_lower, w2, w1)


def _out_kernel(a_ref, mf_ref, mb_ref, o_ref, lf_ref, lb_ref, lg_ref, h_ref, g1_ref, sh2_ref, sc2_ref,
                n2g_ref, mng_ref, wout_ref, wrh_ref, wrl_ref, br_ref,
                hnew_ref, vx_ref, cnt_ref, run_ref, *, transposed, d, hd):
    @pl.when((pl.program_id(0) == 0) & (pl.program_id(1) == 0))
    def _():
        run_ref[...] = jnp.zeros_like(run_ref)

    dc = a_ref.shape[2]
    dm = mf_ref.shape[2]
    mh = mf_ref[0] + mb_ref[0]
    parts = []
    for h in range(dm // hd):
        x = mh[:, h * hd:(h + 1) * hd]
        xc = x - jnp.mean(x, axis=-1, keepdims=True)
        parts.append(xc * lax.rsqrt(jnp.mean(xc * xc, axis=-1, keepdims=True) + EPS) * mng_ref[:, h * hd:(h + 1) * hd])
    m_out = (_sigmoid(o_ref[0]) * jnp.concatenate(parts, axis=-1)).astype(BF16)
    r_out = ((lf_ref[0] + lb_ref[0]) * _gelu_tanh(lg_ref[0])).astype(BF16)
    y = (_dot(a_ref[0], wout_ref[0:dc, :]) + _dot(m_out, wout_ref[dc:dc + dm, :])
         + _dot(r_out, wout_ref[dc + dm:, :]))
    hn = _load_tile(h_ref, transposed) + g1_ref[0] * y
    hnew_ref[0] = hn
    ms = jnp.mean(hn * hn, axis=-1, keepdims=True)
    v = (hn * lax.rsqrt(ms + EPS)) * n2g_ref[...]
    v = v * (1.0 + sc2_ref[0]) + sh2_ref[0]
    vh, vl = _split_hi_lo(v)
    logits = _dot(vh, wrh_ref[...]) + _dot(vl, wrh_ref[...]) + _dot(vh, wrl_ref[...]) + br_ref[...]
    cls, w_lo, w_hi = _route(logits)
    tm = v.shape[0]
    lane = lax.broadcasted_iota(jnp.int32, (tm, ROUTER_LANES), 1).astype(F32)
    onehot = jnp.where(lane == cls, 1.0, 0.0)
    row = lax.broadcasted_iota(jnp.int32, (tm, tm), 0)
    col = lax.broadcasted_iota(jnp.int32, (tm, tm), 1)
    before = _dot(jnp.where(col < row, 1.0, 0.0).astype(BF16), onehot.astype(BF16)) + run_ref[0:1, :]
    rank = jnp.sum(onehot * before, axis=-1, keepdims=True)
    run_ref[...] = run_ref[...] + jnp.sum(onehot, axis=0, keepdims=True)
    cnt_ref[...] = run_ref[...]
    info = (jnp.where(lane == INFO_CLASS, cls, 0.0) + jnp.where(lane == INFO_RANK, rank, 0.0)
            + jnp.where(lane == INFO_W_LO, w_lo, 0.0) + jnp.where(lane == INFO_W_HI, w_hi, 0.0))
    vx_ref[0, :, :d] = v
    vx_ref[0, :, d:] = info


def _out_proj(a, mf, mb, o, lf, lb, lru, h, g1, sh2, sc2, n2g, mng, w_out, wrh, wrl, br, tm, transposed):
    bsz, t, d = h.shape
    dc, dm, dl = a.shape[2], mf.shape[2], lf.shape[2]
    per_batch = g1.shape[0] == bsz
    mod_spec = pl.BlockSpec((1, 1, d), (lambda bi, j: (bi, 0, 0)) if per_batch else (lambda bi, j: (0, 0, 0)))

    def scan(width, blk=0):
        return pl.BlockSpec((1, tm, width), lambda bi, j: (bi, j, blk))

    def whole(x):
        return pl.BlockSpec(x.shape, lambda bi, j: (0,) * x.ndim)

    dx = d + ROUTER_LANES
    n2g, mng = n2g.reshape(1, d), mng.reshape(1, dm)
    hv = _tile_view(h, transposed)
    hnew, vx, cnt = pl.pallas_call(
        functools.partial(_out_kernel, transposed=transposed, d=d, hd=dm // MLSTM_HEADS),
        grid=(bsz, t // tm),
        in_specs=[scan(dc), scan(dm), scan(dm), scan(dm), scan(dl), scan(dl), scan(dl, 1),
                  _tile_spec(t, tm, d, transposed),
                  mod_spec, mod_spec, mod_spec, whole(n2g), whole(mng), whole(w_out), whole(wrh), whole(wrl), whole(br)],
        out_specs=[scan(d), scan(dx), pl.BlockSpec((SUBLANES, ROUTER_LANES), lambda bi, j: (0, 0))],
        out_shape=[jax.ShapeDtypeStruct((bsz, t, d), F32), jax.ShapeDtypeStruct((bsz, t, dx), F32),
                   jax.ShapeDtypeStruct((SUBLANES, ROUTER_LANES), F32)],
        scratch_shapes=[pltpu.VMEM((SUBLANES, ROUTER_LANES), F32)],
        compiler_params=_cparams(("arbitrary", "arbitrary")),
        name="out_proj",
    )(a, mf, mb, o, lf, lb, lru, hv, g1, sh2, sc2, n2g, mng, w_out, wrh, wrl, br)
    return hnew, vx, cnt


def _routing_tables(vx, cnt, d, tmo, n_max):
    cls = vx[:, d + INFO_CLASS].astype(jnp.int32)
    rank = vx[:, d + INFO_RANK].astype(jnp.int32)
    counts = cnt[0, :N_CLASSES].astype(jnp.int32)
    tiles = (counts + tmo - 1) // tmo
    tile_end = jnp.cumsum(tiles)
    tile_start = tile_end - tiles
    n_real = tile_end[N_CLASSES - 1]
    slot = jnp.take(tile_start * tmo, cls) + rank
    blk = jnp.minimum(jnp.arange(n_max, dtype=jnp.int32), n_real - 1)
    tile_cls = jnp.minimum(jnp.sum((tile_end[None, :] <= blk[:, None]).astype(jnp.int32), axis=1), N_CLASSES - 1)
    pair_lo = jnp.array([a for a in range(EXPERTS_PER_GROUP) for b in range(a + 1, EXPERTS_PER_GROUP)], jnp.int32)
    pair_hi = jnp.array([b for a in range(EXPERTS_PER_GROUP) for b in range(a + 1, EXPERTS_PER_GROUP)], jnp.int32)
    grp = tile_cls // N_PAIRS
    e_lo = grp * EXPERTS_PER_GROUP + jnp.take(pair_lo, tile_cls % N_PAIRS)
    e_hi = grp * EXPERTS_PER_GROUP + jnp.take(pair_hi, tile_cls % N_PAIRS)
    tail = n_real + jnp.arange(N_CLASSES, dtype=jnp.int32)
    fill = jnp.concatenate([jnp.where(tiles > 0, tile_end - 1, -1), jnp.where(tail < n_max, tail, -1)])
    return slot, blk, e_lo, e_hi, n_real.reshape(1), fill.astype(jnp.int32)


def _row_copy(src_ref, src_row, dst_ref, dst_row, sem):
    return pltpu.make_async_copy(src_ref.at[pl.ds(src_row, 1)], dst_ref.at[pl.ds(dst_row, 1)], sem)


def _dispatch_kernel(fill_ref, slot_ref, vx_ref, xs_ref, zero_ref, zsem, sem):
    tm = vx_ref.shape[0]
    tmo = zero_ref.shape[0]

    @pl.when(pl.program_id(0) == 0)
    def _():
        zero_ref[...] = jnp.zeros_like(zero_ref)

        def fill(j):
            return pltpu.make_async_copy(zero_ref, xs_ref.at[pl.ds(pl.multiple_of(fill_ref[j] * tmo, tmo), tmo)], zsem)

        for j in range(fill_ref.shape[0]):
            pl.when(fill_ref[j] >= 0)(lambda j=j: fill(j).start())
        for j in range(fill_ref.shape[0]):
            pl.when(fill_ref[j] >= 0)(lambda j=j: fill(j).wait())

    def issue(g, carry):
        for p in range(N_DMA_PRIORITIES):
            r = g * N_DMA_PRIORITIES + p
            _row_copy(vx_ref, r, xs_ref, slot_ref[0, 0, r], sem).start(priority=p)
        return carry

    lax.fori_loop(0, tm // N_DMA_PRIORITIES, issue, 0, unroll=DMA_UNROLL)
    pltpu.make_async_copy(vx_ref, xs_ref.at[pl.ds(0, tm)], sem).wait()


def _dispatch(vx, slot, fill, n_rows, tmo, tm):
    n, dx = vx.shape
    grid_spec = pltpu.PrefetchScalarGridSpec(
        num_scalar_prefetch=1,
        grid=(n // tm,),
        in_specs=[pl.BlockSpec((1, 1, tm), lambda i, pad: (i, 0, 0), memory_space=pltpu.SMEM),
                  pl.BlockSpec((tm, dx), lambda i, pad: (i, 0))],
        out_specs=pl.BlockSpec(memory_space=pl.ANY),
        scratch_shapes=[pltpu.VMEM((tmo, dx), F32), pltpu.SemaphoreType.DMA(()), pltpu.SemaphoreType.DMA(())])
    return pl.pallas_call(
        _dispatch_kernel,
        grid_spec=grid_spec,
        out_shape=jax.ShapeDtypeStruct((n_rows, dx), F32),
        compiler_params=_cparams(("arbitrary",)),
        name="moe_dispatch",
    )(fill, slot.reshape(n // tm, 1, tm), vx)


def _expert(x, wgu, wd):
    de = wd.shape[0]
    gu = _dot(x, wgu)
    g, u = gu[:, :de], gu[:, de:]
    return _dot((g * _sigmoid(g) * u).astype(BF16), wd)


def _moe_kernel(blk_ref, elo_ref, ehi_ref, nreal_ref, xs_ref, wgu_lo_ref, wd_lo_ref, wgu_hi_ref, wd_hi_ref, ys_ref):
    del blk_ref, elo_ref, ehi_ref
    d = ys_ref.shape[1]

    @pl.when(pl.program_id(0) < nreal_ref[0])
    def _():
        x = xs_ref[:, :d].astype(BF16)
        w_lo = xs_ref[:, d + INFO_W_LO:d + INFO_W_LO + 1]
        w_hi = xs_ref[:, d + INFO_W_HI:d + INFO_W_HI + 1]
        ys_ref[...] = w_lo * _expert(x, wgu_lo_ref[0], wd_lo_ref[0]) + w_hi * _expert(x, wgu_hi_ref[0], wd_hi_ref[0])

    @pl.when(pl.program_id(0) >= nreal_ref[0])
    def _():
        ys_ref[...] = jnp.zeros_like(ys_ref)


def _moe(xs, blk, e_lo, e_hi, n_real, wgu, wd, tm):
    dx = xs.shape[1]
    n_tiles = blk.shape[0]
    n_rows = n_tiles * tm
    _, d, de2 = wgu.shape
    grid_spec = pltpu.PrefetchScalarGridSpec(
        num_scalar_prefetch=4,
        grid=(n_tiles,),
        in_specs=[pl.BlockSpec((tm, dx), lambda i, blk, lo, hi, nr: (blk[i], 0)),
                  pl.BlockSpec((1, d, de2), lambda i, blk, lo, hi, nr: (lo[i], 0, 0)),
                  pl.BlockSpec((1, de2 // 2, d), lambda i, blk, lo, hi, nr: (lo[i], 0, 0)),
                  pl.BlockSpec((1, d, de2), lambda i, blk, lo, hi, nr: (hi[i], 0, 0)),
                  pl.BlockSpec((1, de2 // 2, d), lambda i, blk, lo, hi, nr: (hi[i], 0, 0))],
        out_specs=pl.BlockSpec((tm, d), lambda i, blk, lo, hi, nr: (i, 0)))
    return pl.pallas_call(
        _moe_kernel,
        grid_spec=grid_spec,
        out_shape=jax.ShapeDtypeStruct((n_rows, d), F32),
        compiler_params=_cparams(("arbitrary",)),
        name="moe",
    )(blk, e_lo, e_hi, n_real, xs, wgu, wd, wgu, wd)


def _combine_kernel(slot_ref, slot_next_ref, ys_ref, h_ref, g2_ref, g_ref, o_ref, buf_ref, sem, *, final, transposed):
    i = pl.program_id(0) * pl.num_programs(1) + pl.program_id(1)
    n = pl.num_programs(0) * pl.num_programs(1)
    tm = o_ref.shape[1]

    def gather(s_ref, b):
        def issue(g, carry):
            for p in range(N_DMA_PRIORITIES):
                r = g * N_DMA_PRIORITIES + p
                _row_copy(ys_ref, s_ref[0, 0, r], buf_ref.at[b], r, sem.at[b]).start(priority=p)
            return carry
        lax.fori_loop(0, tm // N_DMA_PRIORITIES, issue, 0, unroll=DMA_UNROLL)

    cur = i % 2

    @pl.when(i == 0)
    def _():
        gather(slot_ref, 0)

    @pl.when(i + 1 < n)
    def _():
        gather(slot_next_ref, 1 - cur)

    pltpu.make_async_copy(ys_ref.at[pl.ds(0, tm)], buf_ref.at[cur], sem.at[cur]).wait()
    h = _load_tile(h_ref, transposed) + g2_ref[0] * buf_ref[cur]
    if final:
        ms = jnp.mean(h * h, axis=-1, keepdims=True)
        h = (h * lax.rsqrt(ms + EPS)) * g_ref[...]
    o_ref[0] = h


def _combine(ys, slot, h, g2, g, tm, final, transposed):
    bsz, t, d = h.shape
    nj = t // tm
    nt = bsz * nj
    per_batch = g2.shape[0] == bsz
    slot3 = slot.reshape(nt, 1, tm)
    smem = functools.partial(pl.BlockSpec, (1, 1, tm), memory_space=pltpu.SMEM)
    return pl.pallas_call(
        functools.partial(_combine_kernel, final=final, transposed=transposed),
        grid=(bsz, nj),
        in_specs=[smem(lambda bi, j: (bi * nj + j, 0, 0)),
                  smem(lambda bi, j: (jnp.minimum(bi * nj + j + 1, nt - 1), 0, 0)),
                  pl.BlockSpec(memory_space=pl.ANY),
                  _tile_spec(t, tm, d, transposed, to_natural=True),
                  pl.BlockSpec((1, 1, d), (lambda bi, j: (bi, 0, 0)) if per_batch else (lambda bi, j: (0, 0, 0))),
                  pl.BlockSpec((1, d), lambda bi, j: (0, 0))],
        out_specs=pl.BlockSpec((1, tm, d), lambda bi, j: (bi, j, 0)),
        out_shape=jax.ShapeDtypeStruct((bsz, t, d), F32),
        scratch_shapes=[pltpu.VMEM((2, tm, d), F32), pltpu.SemaphoreType.DMA((2,))],
        compiler_params=_cparams(("arbitrary", "arbitrary")),
        name="moe_combine",
    )(slot3, slot3, ys, _tile_view(h, transposed, to_natural=True), g2, g.reshape(1, d))


def _moe_layer(vx, cnt, wgu, wd, h, g2, g, final, transposed):
    bsz, t, d = h.shape
    n = bsz * t
    tmo = MOE_TILE
    while tmo > MOE_TILE_MIN and n < 2 * N_CLASSES * tmo:
        tmo //= 2
    n_max = n // tmo + N_CLASSES
    vx = vx.reshape(n, d + ROUTER_LANES)
    slot, blk, e_lo, e_hi, n_real, fill = _routing_tables(vx, cnt, d, tmo, n_max)
    xs = _dispatch(vx, slot, fill, n_max * tmo, tmo, min(DISPATCH_TILE, n))
    ys = _moe(xs, blk, e_lo, e_hi, n_real, wgu, wd, tmo)
    if transposed:
        slot = slot.reshape(bsz, GRID_W, t // GRID_W).transpose(0, 2, 1)
        tm = SUBLANES * GRID_W
    else:
        tm = min(DISPATCH_TILE, t)
    return _combine(ys, slot, h, g2, g, tm, final, transposed)


def _block_diag(w):
    nh, bw, _ = w.shape
    eye = jnp.eye(nh, dtype=w.dtype)
    return (eye[:, None, :, None] * w[:, :, None, :]).reshape(nh * bw, nh * bw)


def kernel(x, c, ctx, c_ctx, w_mod, b_mod, norm1_g, norm2_g, w_in, conv_w, conv_b, conv_ln_g, conv_ln_b,
           mlstm_b_i, mlstm_b_f, mlstm_norm_g, lru_conv_w, lru_conv_b, lru_w_a, lru_b_a, lru_w_x, lru_b_x,
           lru_lambda, w_out, w_rg, b_rg, w_re, b_re, w_gate, w_up, w_down, final_g):
    bsz, t, d = x.shape
    tc = ctx.shape[1]
    depth = w_mod.shape[0]
    dc = conv_w.shape[2]
    dm = mlstm_norm_g.shape[1]
    dl = lru_lambda.shape[2]
    ngate = N_DIR * MLSTM_HEADS
    tm_c = min(TOKEN_TILE, tc)
    tb = min(SCAN_BLOCK, t)

    rp = -(-(bsz + 1) // SUBLANES) * SUBLANES
    cvec = jnp.zeros((rp, d), F32).at[:bsz].set(c).at[bsz].set(c_ctx)
    mod = _modulation(cvec, w_mod, b_mod)

    c_q = 2 * dc
    c_k = c_q + dm
    c_v = c_k + dm
    c_o = c_v + dm
    c_g = c_o + dm
    c_l = c_g + 2 * ngate
    splits = (2 * dc, dm, dm, 2 * dl)
    out_dtypes = (F32, BF16, F32, F32)
    splits_t = (dm, dm, 2 * ngate)
    out_dtypes_t = (BF16, BF16, F32)

    h_lat, h_ctx = x, ctx
    for l in range(depth):
        last = l == depth - 1
        transposed = l % 2 == 1
        tm = SUBLANES * (t // GRID_W) if transposed else min(TOKEN_TILE, t)
        m_lat = [mod[l, :bsz, k * d:(k + 1) * d].reshape(bsz, 1, d) for k in range(6)]
        m_ctx = [mod[l, bsz:bsz + 1, k * d:(k + 1) * d].reshape(1, 1, d) for k in range(6)]
        wl = w_in[l]
        w_tok = jnp.concatenate([wl[:, :c_q], wl[:, c_k:c_v], wl[:, c_o:c_g], wl[:, c_l:]], axis=1).astype(BF16)
        w_chn = jnp.concatenate([wl[:, c_q:c_k], wl[:, c_v:c_o], wl[:, c_g:c_l]], axis=1).T.astype(BF16)
        gate_bias = jnp.broadcast_to(
            jnp.concatenate([mlstm_b_i[l].reshape(-1), mlstm_b_f[l].reshape(-1)])[:, None], (2 * ngate, MLSTM_CHUNK))
        wax = jnp.stack([jnp.concatenate([_block_diag(lru_w_a[l, dd]), _block_diag(lru_w_x[l, dd])], axis=1)
                         for dd in range(N_DIR)]).astype(BF16)
        bax = jnp.concatenate([lru_b_a[l], lru_b_x[l]], axis=-1).reshape(N_DIR, 1, 2 * dl)
        lam = lru_lambda[l].reshape(N_DIR, 1, dl)
        lcb = lru_conv_b[l].reshape(N_DIR, 1, dl)
        wr = jnp.zeros((d, ROUTER_LANES), F32).at[:, :N_GROUPS].set(w_rg[l]).at[:, N_GROUPS:N_GROUPS + N_EXPERTS].set(w_re[l])
        wrh, wrl = _split_hi_lo(wr)
        br = jnp.zeros((1, ROUTER_LANES), F32).at[0, :N_GROUPS].set(b_rg[l]).at[0, N_GROUPS:N_GROUPS + N_EXPERTS].set(b_re[l])
        wgu = jnp.concatenate([w_gate[l], w_up[l]], axis=-1).astype(BF16)
        wd = w_down[l].astype(BF16)
        wo = w_out[l].astype(BF16)

        proj = functools.partial(_in_proj, norm_g=norm1_g[l], w=w_tok, wt=w_chn, gate_bias=gate_bias, splits=splits,
                                 out_dtypes=out_dtypes, splits_t=splits_t, out_dtypes_t=out_dtypes_t)
        cv_l, k_l, o_l, lru_l, qt_l, vt_l, gt_l = proj(h_lat, m_lat[0], m_lat[1], tm=tm, transposed=transposed)
        cv_c, k_c, o_c, lru_c, qt_c, vt_c, gt_c = proj(h_ctx, m_ctx[0], m_ctx[1], tm=tm_c, transposed=False)

        a_l = _conformer_conv(cv_l, conv_w[l], conv_b[l], conv_ln_g[l], conv_ln_b[l], tm)
        mcf, mcb, mlf, mlb = _mlstm((qt_c, k_c, vt_c, gt_c), (qt_l, k_l, vt_l, gt_l), tb)
        rcf, rcb, rlf, rlb = _rglru(lru_c, lru_l, lru_conv_w[l], lcb, wax, bax, lam, tb)

        h_lat, vx_l, cnt_l = _out_proj(a_l, mlf, mlb, o_l, rlf, rlb, lru_l, h_lat, m_lat[2], m_lat[3], m_lat[4],
                                       norm2_g[l], mlstm_norm_g[l], wo, wrh, wrl, br, tm, transposed)
        h_lat = _moe_layer(vx_l, cnt_l, wgu, wd, h_lat, m_lat[5], final_g, last, transposed)
        if not last:
            a_c = _conformer_conv(cv_c, conv_w[l], conv_b[l], conv_ln_g[l], conv_ln_b[l], tm_c)
            h_ctx, vx_c, cnt_c = _out_proj(a_c, mcf, mcb, o_c, rcf, rcb, lru_c, h_ctx, m_ctx[2], m_ctx[3], m_ctx[4],
                                           norm2_g[l], mlstm_norm_g[l], wo, wrh, wrl, br, tm_c, False)
            h_ctx = _moe_layer(vx_c, cnt_c, wgu, wd, h_ctx, m_ctx[5], final_g, False, False)
    return h_lat
```

```python
import functools
import math

import jax
import jax.numpy as jnp
from jax import lax
from jax.experimental import pallas as pl
from jax.experimental.pallas import tpu as pltpu

EPS = 1e-6
GRID_W = 64
D_CONV_FRAC = 4
CONV_WIDTH = 31
CONV_HALO = 16
MLSTM_HEADS = 4
MLSTM_CHUNK = 128
LRU_HEADS = 4
LRU_CONV_WIDTH = 4
LRU_C = 8.0
N_DIR = 2
N_GROUPS = 4
EXPERTS_PER_GROUP = 4
N_EXPERTS = N_GROUPS * EXPERTS_PER_GROUP
N_PAIRS = EXPERTS_PER_GROUP * (EXPERTS_PER_GROUP - 1) // 2
N_CLASSES = N_GROUPS * N_PAIRS
ROUTER_LANES = 128
INFO_CLASS, INFO_RANK, INFO_W_LO, INFO_W_HI = 0, 1, 2, 3
SUBLANES = 8
LOG2E = 1.4426950408889634

VMEM_LIMIT = 56 * 1024 * 1024
IN_TILE = 1024
TOKEN_TILE = 512
SCAN_BLOCK = 256
MOE_TILE = 512
MOE_TILE_MIN = 128
DISPATCH_TILE = 512
DMA_UNROLL = 8

F32 = jnp.float32
BF16 = jnp.bfloat16


def _cparams(sem):
    return pltpu.CompilerParams(dimension_semantics=sem, vmem_limit_bytes=VMEM_LIMIT)


def _sigmoid(x):
    return jax.nn.sigmoid(x)


def _log_sigmoid(x):
    return jnp.minimum(x, 0.0) - jnp.log1p(jnp.exp(-jnp.abs(x)))


def _softplus(x):
    return jnp.maximum(x, 0.0) + jnp.log1p(jnp.exp(-jnp.abs(x)))


def _gelu_tanh(x):
    return x * (0.5 * (1.0 + jnp.tanh(0.7978845608028654 * (x + 0.044715 * (x * x * x)))))


def _dot(a, b):
    return jnp.dot(a, b, preferred_element_type=F32)


def _split_hi_lo(x):
    hi = x.astype(BF16)
    lo = (x - hi.astype(F32)).astype(BF16)
    return hi, lo


def _load_tile(ref, transposed):
    if not transposed:
        return ref[0]
    return jnp.concatenate([ref[0, :, w, :] for w in range(ref.shape[2])], axis=0)


def _tile_spec(t, tm, d, transposed, to_natural=False):
    if not transposed:
        return pl.BlockSpec((1, tm, d), lambda b, j: (b, j, 0))
    major = GRID_W if to_natural else t // GRID_W
    assert tm % (SUBLANES * major) == 0
    return pl.BlockSpec((1, major, tm // major, d), lambda b, j: (b, 0, j, 0))


def _tile_view(a, transposed, to_natural=False):
    if not transposed:
        return a
    b, t, d = a.shape
    return a.reshape(b, GRID_W, t // GRID_W, d) if to_natural else a.reshape(b, t // GRID_W, GRID_W, d)


def _mod_kernel(c_ref, w_ref, b_ref, o_ref):
    c = c_ref[...]
    s = (c * _sigmoid(c)).astype(BF16)
    o_ref[0] = _dot(s, w_ref[0].astype(BF16)) + b_ref[0]


def _modulation(cvec, w_mod, b_mod):
    nl, d, d6 = w_mod.shape
    rp = cvec.shape[0]
    tn = d6 // 4
    return pl.pallas_call(
        _mod_kernel,
        grid=(nl, d6 // tn),
        in_specs=[pl.BlockSpec((rp, d), lambda l, j: (0, 0)),
                  pl.BlockSpec((1, d, tn), lambda l, j: (l, 0, j)),
                  pl.BlockSpec((1, 1, tn), lambda l, j: (l, 0, j))],
        out_specs=pl.BlockSpec((1, rp, tn), lambda l, j: (l, 0, j)),
        out_shape=jax.ShapeDtypeStruct((nl, rp, d6), F32),
        compiler_params=_cparams(("arbitrary", "arbitrary")),
        name="modulation",
    )(cvec, w_mod, b_mod.reshape(nl, 1, d6))


def _chunk_scan(x, op, reverse):
    L = MLSTM_CHUNK
    n = x.shape[1]
    pos = lax.broadcasted_iota(jnp.int32, x.shape, 1) & (L - 1)
    sh = 1
    while sh < L:
        if reverse:
            xs, ok = pltpu.roll(x, n - sh, 1), pos < L - sh
        else:
            xs, ok = pltpu.roll(x, sh, 1), pos >= sh
        x = jnp.where(ok, op(x, xs), x)
        sh *= 2
    return x


def _gate_scans(g):
    nrow = g.shape[0] // 2
    fwd_row = lax.broadcasted_iota(jnp.int32, (nrow, g.shape[1]), 0) < nrow // N_DIR
    ig2 = g[:nrow] * LOG2E
    lf2 = _log_sigmoid(g[nrow:]) * LOG2E
    b2 = jnp.where(fwd_row, _chunk_scan(lf2, jnp.add, False), _chunk_scan(lf2, jnp.add, True))
    c2 = ig2 - b2
    cm2 = jnp.where(fwd_row, _chunk_scan(c2, jnp.maximum, False), _chunk_scan(c2, jnp.maximum, True))
    return jnp.concatenate([ig2, b2, cm2], axis=0)


def _in_kernel(h_ref, shift_ref, scale_ref, g_ref, w_ref, wt_ref, gbias_ref, *outs, transposed, splits, splits_t):
    h = _load_tile(h_ref, transposed)
    ms = jnp.mean(h * h, axis=-1, keepdims=True)
    u = (h * lax.rsqrt(ms + EPS)) * g_ref[...]
    u = (u * (1.0 + scale_ref[0]) + shift_ref[0]).astype(BF16)
    c0 = 0
    for o_ref, width in zip(outs, splits):
        o_ref[0] = _dot(u, w_ref[:, c0:c0 + width]).astype(o_ref.dtype)
        c0 += width
    r0 = 0
    outs_t = outs[len(splits):]
    for o_ref, width in zip(outs_t[:-1], splits_t[:-1]):
        o_ref[0] = lax.dot_general(wt_ref[r0:r0 + width, :], u, (((1,), (1,)), ((), ())),
                                   preferred_element_type=F32).astype(o_ref.dtype)
        r0 += width
    gates = lax.dot_general(wt_ref[r0:r0 + splits_t[-1], :], u, (((1,), (1,)), ((), ())), preferred_element_type=F32)
    tm = gates.shape[1]
    outs_t[-1][0] = _gate_scans(gates + jnp.concatenate([gbias_ref[...]] * (tm // MLSTM_CHUNK), axis=1))


def _in_proj(h, shift, scale, norm_g, w, wt, gate_bias, splits, out_dtypes, splits_t, out_dtypes_t, tm, transposed):
    b, t, d = h.shape
    per_batch = shift.shape[0] == b
    mod_spec = pl.BlockSpec((1, 1, d), (lambda bi, j: (bi, 0, 0)) if per_batch else (lambda bi, j: (0, 0, 0)))
    rows_t = splits_t[:-1] + (splits_t[-1] // 2 * 3,)
    return pl.pallas_call(
        functools.partial(_in_kernel, transposed=transposed, splits=splits, splits_t=splits_t),
        grid=(b, t // tm),
        in_specs=[_tile_spec(t, tm, d, transposed), mod_spec, mod_spec, pl.BlockSpec((1, d), lambda bi, j: (0, 0)),
                  pl.BlockSpec(w.shape, lambda bi, j: (0, 0)), pl.BlockSpec(wt.shape, lambda bi, j: (0, 0)),
                  pl.BlockSpec(gate_bias.shape, lambda bi, j: (0, 0))],
        out_specs=([pl.BlockSpec((1, tm, width), lambda bi, j: (bi, j, 0)) for width in splits]
                   + [pl.BlockSpec((1, width, tm), lambda bi, j: (bi, 0, j)) for width in rows_t]),
        out_shape=([jax.ShapeDtypeStruct((b, t, width), dt) for width, dt in zip(splits, out_dtypes)]
                   + [jax.ShapeDtypeStruct((b, width, t), dt) for width, dt in zip(rows_t, out_dtypes_t)]),
        compiler_params=_cparams(("arbitrary", "arbitrary")),
        name="in_proj",
    )(_tile_view(h, transposed), shift, scale, norm_g.reshape(1, d), w, wt, gate_bias)


def _conv_kernel(prev_ref, cur_ref, next_ref, w_ref, b_ref, lng_ref, lnb_ref, o_ref, *, tb, nblk, dc):
    j = pl.program_id(1)

    def glu(x):
        return x[:, :dc] * _sigmoid(x[:, dc:])

    up = jnp.where(j > 0, glu(prev_ref[0]), 0.0)
    un = jnp.where(j < nblk - 1, glu(next_ref[0]), 0.0)
    ext = jnp.concatenate([up, glu(cur_ref[0]), un], axis=0)
    base = CONV_HALO - CONV_WIDTH // 2
    nrows = tb + 2 * CONV_HALO
    acc = jnp.zeros((tb, dc), F32)
    for b in range(SUBLANES):
        shifted = ext if b == 0 else pltpu.roll(ext, nrows - b, 0)
        for a in range((base + CONV_WIDTH - 1 - b) // SUBLANES + 1):
            k = SUBLANES * a + b - base
            if 0 <= k < CONV_WIDTH:
                acc = acc + w_ref[k:k + 1, :] * shifted[SUBLANES * a:SUBLANES * a + tb, :]
    acc = acc + b_ref[...]
    mu = jnp.mean(acc, axis=-1, keepdims=True)
    xc = acc - mu
    y = xc * lax.rsqrt(jnp.mean(xc * xc, axis=-1, keepdims=True) + EPS) * lng_ref[...] + lnb_ref[...]
    o_ref[0] = (y * _sigmoid(y)).astype(o_ref.dtype)


def _conformer_conv(cv, w, b, ln_g, ln_b, tb):
    bsz, t, c2 = cv.shape
    dc = c2 // 2
    nblk = t // tb
    hb = tb // CONV_HALO
    nh = t // CONV_HALO
    wp = jnp.zeros((CONV_WIDTH + 1, dc), F32).at[:CONV_WIDTH].set(w)
    vec = pl.BlockSpec((1, dc), lambda bi, j: (0, 0))
    return pl.pallas_call(
        functools.partial(_conv_kernel, tb=tb, nblk=nblk, dc=dc),
        grid=(bsz, nblk),
        in_specs=[pl.BlockSpec((1, CONV_HALO, c2), lambda bi, j: (bi, jnp.maximum(j * hb - 1, 0), 0)),
                  pl.BlockSpec((1, tb, c2), lambda bi, j: (bi, j, 0)),
                  pl.BlockSpec((1, CONV_HALO, c2), lambda bi, j: (bi, jnp.minimum((j + 1) * hb, nh - 1), 0)),
                  pl.BlockSpec((CONV_WIDTH + 1, dc), lambda bi, j: (0, 0)), vec, vec, vec],
        out_specs=pl.BlockSpec((1, tb, dc), lambda bi, j: (bi, j, 0)),
        out_shape=jax.ShapeDtypeStruct((bsz, t, dc), BF16),
        compiler_params=_cparams(("arbitrary", "arbitrary")),
        name="conformer_conv",
    )(cv, cv, cv, wp, b.reshape(1, dc), ln_g.reshape(1, dc), ln_b.reshape(1, dc))


def _mlstm_block(qt_ref, k_ref, vt_ref, gt_ref, out_ref, c_ref, n_ref, m_ref, d, nchunks, hd):
    L = MLSTM_CHUNK
    nh = MLSTM_HEADS
    assert hd == L
    nrow = N_DIR * nh
    log2_scale = math.log2(hd ** -0.5)
    ig2, b2, cm2 = gt_ref[0, 0:nrow], gt_ref[0, nrow:2 * nrow], gt_ref[0, 2 * nrow:3 * nrow]
    c2 = ig2 - b2
    row_s = lax.broadcasted_iota(jnp.int32, (L, L), 0)
    col_t = lax.broadcasted_iota(jnp.int32, (L, L), 1)
    mask = (row_s <= col_t) if d == 0 else (row_s >= col_t)
    ones16 = jnp.ones((2 * SUBLANES, L), BF16)
    for ci in (range(nchunks) if d == 0 else range(nchunks - 1, -1, -1)):
        sl = slice(ci * L, (ci + 1) * L)
        b2c, c2c, ig2c = b2[:, sl], c2[:, sl], ig2[:, sl]
        m2 = m_ref[d]
        a2 = -jnp.maximum(m2, cm2[:, sl])
        inter = jnp.exp2(m2 + a2)
        edn = jnp.exp2(a2 - b2c)
        b_last = jnp.broadcast_to(b2c[:, L - 1:L] if d == 0 else b2c[:, 0:1], (nrow, L))
        logw2 = b_last - b2c + ig2c
        m2_new = jnp.maximum(b_last + m2, jnp.broadcast_to(jnp.max(logw2, axis=-1, keepdims=True), (nrow, L)))
        wgt = jnp.exp2(logw2 - m2_new + log2_scale)
        decay = jnp.exp2(b_last + m2 - m2_new)
        m_ref[d] = m2_new
        c_cols = (c2c + log2_scale).T
        for h in range(nh):
            r = d * nh + h
            e = c_cols[:, r:r + 1] + a2[r:r + 1]
            p = jnp.where(mask, jnp.exp2(e), 0.0)
            qt = qt_ref[0, h * hd:(h + 1) * hd, sl]
            k = k_ref[0, ci * L:(ci + 1) * L, h * hd:(h + 1) * hd]
            vt = vt_ref[0, h * hd:(h + 1) * hd, sl]
            ct = c_ref[r]
            n16 = n_ref[r]
            st = _dot(k, qt) * p
            nq = _dot(n16.astype(BF16), qt)
            nd = _dot(vt, st.astype(BF16))
            inter_r = inter[r:r + 1]
            num = nd + inter_r * _dot(ct.astype(BF16), qt)
            den = jnp.sum(st, axis=0, keepdims=True) + inter_r * nq[0:1]
            ht = num * (1.0 / jnp.maximum(jnp.abs(den), edn[r:r + 1]))
            out_ref[0, ci * L:(ci + 1) * L, h * hd:(h + 1) * hd] = ht.T
            w_r = wgt[r:r + 1]
            vw = (vt.astype(F32) * w_r).astype(BF16)
            upd = _dot(jnp.concatenate([vw, ones16 * w_r.astype(BF16)], axis=0), k)
            dec = decay[r:r + 1]
            c_ref[r] = dec * ct + upd[:hd]
            n_ref[r] = dec * n16 + upd[hd:]


def _mlstm_kernel(qtc_ref, kc_ref, vtc_ref, gtc_ref, qtf_ref, kf_ref, vtf_ref, gtf_ref,
                  qtb_ref, kb_ref, vtb_ref, gtb_ref,
                  hcf_ref, hcb_ref, hf_ref, hb_ref, c_ref, n_ref, m_ref, *, nc_ctx, nc_lat, hd):
    s = pl.program_id(1)
    state = (c_ref, n_ref, m_ref)

    @pl.when(s == 0)
    def _():
        c_ref[...] = jnp.zeros_like(c_ref)
        n_ref[...] = jnp.zeros_like(n_ref)
        m_ref[...] = jnp.zeros_like(m_ref)
        _mlstm_block(qtc_ref, kc_ref, vtc_ref, gtc_ref, hcf_ref, *state, 0, nc_ctx, hd)
        _mlstm_block(qtc_ref, kc_ref, vtc_ref, gtc_ref, hcb_ref, *state, 1, nc_ctx, hd)

    @pl.when(s > 0)
    def _():
        _mlstm_block(qtf_ref, kf_ref, vtf_ref, gtf_ref, hf_ref, *state, 0, nc_lat, hd)
        _mlstm_block(qtb_ref, kb_ref, vtb_ref, gtb_ref, hb_ref, *state, 1, nc_lat, hd)


def _mlstm(ctx_in, lat_in, tb):
    qt_c, k_c, vt_c, gt_c = ctx_in
    qt_l, k_l, vt_l, gt_l = lat_in
    bsz, tc, dm = k_c.shape
    t = k_l.shape[1]
    ng = gt_l.shape[1]
    hd = dm // MLSTM_HEADS
    nb = t // tb
    nstate = N_DIR * MLSTM_HEADS

    def fwd(s):
        return jnp.maximum(s - 1, 0)

    def bwd(s):
        return nb - 1 - jnp.maximum(s - 1, 0)

    def specs(tlen, blk):
        return [pl.BlockSpec((1, dm, tlen), lambda bi, s: (bi, 0, blk(s))),
                pl.BlockSpec((1, tlen, dm), lambda bi, s: (bi, blk(s), 0)),
                pl.BlockSpec((1, dm, tlen), lambda bi, s: (bi, 0, blk(s))),
                pl.BlockSpec((1, ng, tlen), lambda bi, s: (bi, 0, blk(s)))]

    def out(tlen, blk):
        return pl.BlockSpec((1, tlen, dm), lambda bi, s: (bi, blk(s), 0))

    first = lambda s: 0
    return pl.pallas_call(
        functools.partial(_mlstm_kernel, nc_ctx=tc // MLSTM_CHUNK, nc_lat=tb // MLSTM_CHUNK, hd=hd),
        grid=(bsz, nb + 1),
        in_specs=specs(tc, first) + specs(tb, fwd) + specs(tb, bwd),
        out_specs=[out(tc, first), out(tc, first), out(tb, fwd), out(tb, bwd)],
        out_shape=[jax.ShapeDtypeStruct((bsz, tc, dm), F32), jax.ShapeDtypeStruct((bsz, tc, dm), F32),
                   jax.ShapeDtypeStruct((bsz, t, dm), F32), jax.ShapeDtypeStruct((bsz, t, dm), F32)],
        scratch_shapes=[pltpu.VMEM((nstate, hd, hd), F32), pltpu.VMEM((nstate, 2 * SUBLANES, hd), F32),
                        pltpu.VMEM((N_DIR, nstate, MLSTM_CHUNK), F32)],
        compiler_params=_cparams(("arbitrary", "arbitrary")),
        name="mlstm",
    )(*ctx_in, *lat_in, *lat_in)


def _lru_block(x, halo, d, cw, cb, wax, bax, lam, h0):
    tb, dl = x.shape
    kw = LRU_CONV_WIDTH
    if d == 0:
        ext = jnp.concatenate([halo, x], axis=0)
        taps = [ext[SUBLANES - (kw - 1) + j:SUBLANES - (kw - 1) + j + tb] for j in range(kw)]
    else:
        ext = jnp.concatenate([x, halo], axis=0)
        taps = [ext[kw - 1 - j:kw - 1 - j + tb] for j in range(kw)]
    xc = cb
    for j in range(kw):
        xc = xc + cw[j:j + 1, :] * taps[j]
    ri = _sigmoid(_dot(xc.astype(BF16), wax) + bax)
    r, i = ri[:, :dl], ri[:, dl:]
    log_a = (-LRU_C * r) * _softplus(-lam)
    a = jnp.exp(log_a)
    th = jnp.tanh(log_a)
    u = jnp.sqrt(-2.0 * th / (1.0 - th)) * (i * xc)
    ngrp = tb // SUBLANES
    a = a.reshape(ngrp, SUBLANES, dl)
    u = u.reshape(ngrp, SUBLANES, dl)
    sub = lax.broadcasted_iota(jnp.int32, a.shape, 1)
    for sh in (1, 2, 4):
        if d == 0:
            a_s, u_s, msk = pltpu.roll(a, sh, 1), pltpu.roll(u, sh, 1), sub >= sh
        else:
            a_s, u_s, msk = pltpu.roll(a, SUBLANES - sh, 1), pltpu.roll(u, SUBLANES - sh, 1), sub < SUBLANES - sh
        u = jnp.where(msk, a * u_s + u, u)
        a = jnp.where(msk, a * a_s, a)
    a = a.reshape(tb, dl)
    u = u.reshape(tb, dl)
    outs = [None] * ngrp
    carry = h0
    for j in (range(ngrp) if d == 0 else range(ngrp - 1, -1, -1)):
        hj = u[j * SUBLANES:(j + 1) * SUBLANES] + a[j * SUBLANES:(j + 1) * SUBLANES] * carry
        outs[j] = hj
        carry = hj[SUBLANES - 1:SUBLANES] if d == 0 else hj[0:1]
    return jnp.concatenate(outs, axis=0), carry


def _lru_kernel(xc_ref, xf_ref, xb_ref, cw_ref, cb_ref, wax_ref, bax_ref, lam_ref,
                hcf_ref, hcb_ref, hf_ref, hb_ref, hcar_ref, halo_ref):
    s = pl.program_id(1)

    def params(d):
        return cw_ref[d], cb_ref[d], wax_ref[d], bax_ref[d], lam_ref[d]

    @pl.when(s == 0)
    def _():
        x = xc_ref[0]
        zero_halo = jnp.zeros((SUBLANES, x.shape[1]), F32)
        zero_h = jnp.zeros((1, x.shape[1]), F32)
        for d, o_ref in ((0, hcf_ref), (1, hcb_ref)):
            h, carry = _lru_block(x, zero_halo, d, *params(d), zero_h)
            o_ref[0] = h
            hcar_ref[d] = jnp.broadcast_to(carry, hcar_ref.shape[1:])

    @pl.when(s > 0)
    def _():
        for d, x_ref, o_ref in ((0, xf_ref, hf_ref), (1, xb_ref, hb_ref)):
            x = x_ref[0]
            halo = jnp.where(s > 1, halo_ref[d], 0.0)
            h, carry = _lru_block(x, halo, d, *params(d), hcar_ref[d][0:1])
            o_ref[0] = h
            hcar_ref[d] = jnp.broadcast_to(carry, hcar_ref.shape[1:])
            halo_ref[d] = x[x.shape[0] - SUBLANES:] if d == 0 else x[:SUBLANES]


def _rglru(lru_c, lru_l, cw, cb, wax, bax, lam, tb):
    bsz, tc, w2 = lru_c.shape
    t = lru_l.shape[1]
    dl = w2 // 2
    nb = t // tb

    def fwd(bi, s):
        return (bi, jnp.maximum(s - 1, 0), 0)

    def bwd(bi, s):
        return (bi, nb - 1 - jnp.maximum(s - 1, 0), 0)

    def ctx(bi, s):
        return (bi, 0, 0)

    def whole(a):
        return pl.BlockSpec(a.shape, lambda bi, s: (0,) * a.ndim)

    return pl.pallas_call(
        _lru_kernel,
        grid=(bsz, nb + 1),
        in_specs=[pl.BlockSpec((1, tc, dl), ctx), pl.BlockSpec((1, tb, dl), fwd), pl.BlockSpec((1, tb, dl), bwd),
                  whole(cw), whole(cb), whole(wax), whole(bax), whole(lam)],
        out_specs=[pl.BlockSpec((1, tc, dl), ctx), pl.BlockSpec((1, tc, dl), ctx),
                   pl.BlockSpec((1, tb, dl), fwd), pl.BlockSpec((1, tb, dl), bwd)],
        out_shape=[jax.ShapeDtypeStruct((bsz, tc, dl), F32), jax.ShapeDtypeStruct((bsz, tc, dl), F32),
                   jax.ShapeDtypeStruct((bsz, t, dl), F32), jax.ShapeDtypeStruct((bsz, t, dl), F32)],
        scratch_shapes=[pltpu.VMEM((N_DIR, SUBLANES, dl), F32), pltpu.VMEM((N_DIR, SUBLANES, dl), F32)],
        compiler_params=_cparams(("arbitrary", "arbitrary")),
        name="rglru",
    )(lru_c, lru_l, lru_l, cw, cb, wax, bax, lam)


def _route(logits):
    ng, ne = N_GROUPS, EXPERTS_PER_GROUP
    lane = lax.broadcasted_iota(jnp.int32, logits.shape, 1).astype(F32)
    big = float(ROUTER_LANES)
    is_g = lane < ng
    gl = jnp.where(is_g, logits, -jnp.inf)
    gmax = jnp.max(gl, axis=-1, keepdims=True)
    g_sel = jnp.min(jnp.where(gl == gmax, lane, big), axis=-1, keepdims=True)
    p_g = 1.0 / jnp.sum(jnp.where(is_g, jnp.exp(logits - gmax), 0.0), axis=-1, keepdims=True)
    lo = ng + ne * g_sel
    el = jnp.where((lane >= lo) & (lane < lo + ne), logits, -jnp.inf)
    v1 = jnp.max(el, axis=-1, keepdims=True)
    i1 = jnp.min(jnp.where(el == v1, lane, big), axis=-1, keepdims=True)
    el2 = jnp.where(lane == i1, -jnp.inf, el)
    v2 = jnp.max(el2, axis=-1, keepdims=True)
    i2 = jnp.min(jnp.where(el2 == v2, lane, big), axis=-1, keepdims=True)
    e2 = jnp.exp(v2 - v1)
    w1 = p_g / (1.0 + e2)
    w2 = p_g * e2 / (1.0 + e2)
    first_lower = i1 < i2
    e_lo = jnp.minimum(i1, i2) - lo
    e_hi = jnp.maximum(i1, i2) - lo
    pair = e_lo * (2 * ne - 1 - e_lo) * 0.5 + (e_hi - e_lo - 1.0)
    return g_sel * N_PAIRS + pair, jnp.where(first_lower, w1, w2), jnp.where(first_lower, w2, w1)


def _out_kernel(a_ref, mf_ref, mb_ref, o_ref, lf_ref, lb_ref, lg_ref, h_ref, g1_ref, sh2_ref, sc2_ref,
                n2g_ref, mng_ref, wout_ref, wrh_ref, wrl_ref, br_ref,
                hnew_ref, vx_ref, cnt_ref, info_ref, run_ref, *, transposed, d, hd):
    @pl.when((pl.program_id(0) == 0) & (pl.program_id(1) == 0))
    def _():
        run_ref[...] = jnp.zeros_like(run_ref)

    dc = a_ref.shape[2]
    dm = mf_ref.shape[2]
    mh = mf_ref[0] + mb_ref[0]
    parts = []
    for h in range(dm // hd):
        x = mh[:, h * hd:(h + 1) * hd]
        xc = x - jnp.mean(x, axis=-1, keepdims=True)
        parts.append(xc * lax.rsqrt(jnp.mean(xc * xc, axis=-1, keepdims=True) + EPS) * mng_ref[:, h * hd:(h + 1) * hd])
    m_out = (_sigmoid(o_ref[0]) * jnp.concatenate(parts, axis=-1)).astype(BF16)
    r_out = ((lf_ref[0] + lb_ref[0]) * _gelu_tanh(lg_ref[0])).astype(BF16)
    y = (_dot(a_ref[0], wout_ref[0:dc, :]) + _dot(m_out, wout_ref[dc:dc + dm, :])
         + _dot(r_out, wout_ref[dc + dm:, :]))
    hn = _load_tile(h_ref, transposed) + g1_ref[0] * y
    hnew_ref[0] = hn
    ms = jnp.mean(hn * hn, axis=-1, keepdims=True)
    v = (hn * lax.rsqrt(ms + EPS)) * n2g_ref[...]
    v = v * (1.0 + sc2_ref[0]) + sh2_ref[0]
    vh, vl = _split_hi_lo(v)
    logits = _dot(vh, wrh_ref[...]) + _dot(vl, wrh_ref[...]) + _dot(vh, wrl_ref[...]) + br_ref[...]
    cls, w_lo, w_hi = _route(logits)
    tm = v.shape[0]
    lane = lax.broadcasted_iota(jnp.int32, (tm, ROUTER_LANES), 1).astype(F32)
    onehot = jnp.where(lane == cls, 1.0, 0.0)
    row = lax.broadcasted_iota(jnp.int32, (tm, tm), 0)
    col = lax.broadcasted_iota(jnp.int32, (tm, tm), 1)
    before = _dot(jnp.where(col < row, 1.0, 0.0).astype(BF16), onehot.astype(BF16)) + run_ref[0:1, :]
    rank = jnp.sum(onehot * before, axis=-1, keepdims=True)
    run_ref[...] = run_ref[...] + jnp.sum(onehot, axis=0, keepdims=True)
    cnt_ref[...] = run_ref[...]
    info = (jnp.where(lane == INFO_CLASS, cls, 0.0) + jnp.where(lane == INFO_RANK, rank, 0.0)
            + jnp.where(lane == INFO_W_LO, w_lo, 0.0) + jnp.where(lane == INFO_W_HI, w_hi, 0.0))
    vx_ref[0, :, :d] = v
    vx_ref[0, :, d:] = info
    info_ref[0] = info.T[:SUBLANES]


def _out_proj(a, mf, mb, o, lf, lb, lru, h, g1, sh2, sc2, n2g, mng, w_out, wrh, wrl, br, tm, transposed):
    bsz, t, d = h.shape
    dc, dm, dl = a.shape[2], mf.shape[2], lf.shape[2]
    per_batch = g1.shape[0] == bsz
    mod_spec = pl.BlockSpec((1, 1, d), (lambda bi, j: (bi, 0, 0)) if per_batch else (lambda bi, j: (0, 0, 0)))

    def scan(width, blk=0):
        return pl.BlockSpec((1, tm, width), lambda bi, j: (bi, j, blk))

    def whole(x):
        return pl.BlockSpec(x.shape, lambda bi, j: (0,) * x.ndim)

    dx = d + ROUTER_LANES
    n2g, mng = n2g.reshape(1, d), mng.reshape(1, dm)
    hv = _tile_view(h, transposed)
    hnew, vx, cnt, info_t = pl.pallas_call(
        functools.partial(_out_kernel, transposed=transposed, d=d, hd=dm // MLSTM_HEADS),
        grid=(bsz, t // tm),
        in_specs=[scan(dc), scan(dm), scan(dm), scan(dm), scan(dl), scan(dl), scan(dl, 1),
                  _tile_spec(t, tm, d, transposed),
                  mod_spec, mod_spec, mod_spec, whole(n2g), whole(mng), whole(w_out), whole(wrh), whole(wrl), whole(br)],
        out_specs=[scan(d), scan(dx), pl.BlockSpec((SUBLANES, ROUTER_LANES), lambda bi, j: (0, 0)),
                   pl.BlockSpec((1, SUBLANES, tm), lambda bi, j: (bi, 0, j))],
        out_shape=[jax.ShapeDtypeStruct((bsz, t, d), F32), jax.ShapeDtypeStruct((bsz, t, dx), F32),
                   jax.ShapeDtypeStruct((SUBLANES, ROUTER_LANES), F32), jax.ShapeDtypeStruct((bsz, SUBLANES, t), F32)],
        scratch_shapes=[pltpu.VMEM((SUBLANES, ROUTER_LANES), F32)],
        compiler_params=_cparams(("arbitrary", "arbitrary")),
        name="out_proj",
    )(a, mf, mb, o, lf, lb, lru, hv, g1, sh2, sc2, n2g, mng, w_out, wrh, wrl, br)
    return hnew, vx, (cnt, info_t)


def _routing_tables(route, tmo, n_max):
    cnt, info_t = route
    cls = info_t[:, INFO_CLASS, :].reshape(-1).astype(jnp.int32)
    rank = info_t[:, INFO_RANK, :].reshape(-1).astype(jnp.int32)
    counts = cnt[0, :N_CLASSES].astype(jnp.int32)
    tiles = (counts + tmo - 1) // tmo
    tile_end = jnp.cumsum(tiles)
    tile_start = tile_end - tiles
    n_real = tile_end[N_CLASSES - 1]
    slot = jnp.take(tile_start * tmo, cls) + rank
    blk = jnp.minimum(jnp.arange(n_max, dtype=jnp.int32), n_real - 1)
    tile_cls = jnp.minimum(jnp.sum((tile_end[None, :] <= blk[:, None]).astype(jnp.int32), axis=1), N_CLASSES - 1)
    pair_lo = jnp.array([a for a in range(EXPERTS_PER_GROUP) for b in range(a + 1, EXPERTS_PER_GROUP)], jnp.int32)
    pair_hi = jnp.array([b for a in range(EXPERTS_PER_GROUP) for b in range(a + 1, EXPERTS_PER_GROUP)], jnp.int32)
    grp = tile_cls // N_PAIRS
    e_lo = grp * EXPERTS_PER_GROUP + jnp.take(pair_lo, tile_cls % N_PAIRS)
    e_hi = grp * EXPERTS_PER_GROUP + jnp.take(pair_hi, tile_cls % N_PAIRS)
    tail = n_real + jnp.arange(N_CLASSES, dtype=jnp.int32)
    fill = jnp.concatenate([jnp.where(tiles > 0, tile_end - 1, -1), jnp.where(tail < n_max, tail, -1)])
    return slot, blk, e_lo, e_hi, n_real.reshape(1), fill.astype(jnp.int32)


def _row_copy(src_ref, src_row, dst_ref, dst_row, sem):
    return pltpu.make_async_copy(src_ref.at[pl.ds(src_row, 1)], dst_ref.at[pl.ds(dst_row, 1)], sem)


def _dispatch_kernel(fill_ref, slot_ref, vx_ref, xs_ref, zero_ref, zsem, sem):
    tm = vx_ref.shape[0]
    tmo = zero_ref.shape[0]

    @pl.when(pl.program_id(0) == 0)
    def _():
        zero_ref[...] = jnp.zeros_like(zero_ref)

        def fill(j):
            return pltpu.make_async_copy(zero_ref, xs_ref.at[pl.ds(pl.multiple_of(fill_ref[j] * tmo, tmo), tmo)], zsem)

        for j in range(fill_ref.shape[0]):
            pl.when(fill_ref[j] >= 0)(lambda j=j: fill(j).start())
        for j in range(fill_ref.shape[0]):
            pl.when(fill_ref[j] >= 0)(lambda j=j: fill(j).wait())

    def issue(r, carry):
        _row_copy(vx_ref, r, xs_ref, slot_ref[0, 0, r], sem).start()
        return carry

    lax.fori_loop(0, tm, issue, 0, unroll=DMA_UNROLL)
    pltpu.make_async_copy(vx_ref, xs_ref.at[pl.ds(0, tm)], sem).wait()


def _dispatch(vx, slot, fill, n_rows, tmo, tm):
    n, dx = vx.shape
    grid_spec = pltpu.PrefetchScalarGridSpec(
        num_scalar_prefetch=1,
        grid=(n // tm,),
        in_specs=[pl.BlockSpec((1, 1, tm), lambda i, pad: (i, 0, 0), memory_space=pltpu.SMEM),
                  pl.BlockSpec((tm, dx), lambda i, pad: (i, 0))],
        out_specs=pl.BlockSpec(memory_space=pl.ANY),
        scratch_shapes=[pltpu.VMEM((tmo, dx), F32), pltpu.SemaphoreType.DMA(()), pltpu.SemaphoreType.DMA(())])
    return pl.pallas_call(
        _dispatch_kernel,
        grid_spec=grid_spec,
        out_shape=jax.ShapeDtypeStruct((n_rows, dx), F32),
        compiler_params=_cparams(("arbitrary",)),
        name="moe_dispatch",
    )(fill, slot.reshape(n // tm, 1, tm), vx)


def _expert(x, wgu, wd):
    de = wd.shape[0]
    gu = _dot(x, wgu)
    g, u = gu[:, :de], gu[:, de:]
    return _dot((g * _sigmoid(g) * u).astype(BF16), wd)


def _moe_kernel(blk_ref, elo_ref, ehi_ref, nreal_ref, xs_ref, wgu_lo_ref, wd_lo_ref, wgu_hi_ref, wd_hi_ref, ys_ref):
    del blk_ref, elo_ref, ehi_ref
    d = ys_ref.shape[1]

    @pl.when(pl.program_id(0) < nreal_ref[0])
    def _():
        x = xs_ref[:, :d].astype(BF16)
        w_lo = xs_ref[:, d + INFO_W_LO:d + INFO_W_LO + 1]
        w_hi = xs_ref[:, d + INFO_W_HI:d + INFO_W_HI + 1]
        ys_ref[...] = w_lo * _expert(x, wgu_lo_ref[0], wd_lo_ref[0]) + w_hi * _expert(x, wgu_hi_ref[0], wd_hi_ref[0])

    @pl.when(pl.program_id(0) >= nreal_ref[0])
    def _():
        ys_ref[...] = jnp.zeros_like(ys_ref)


def _moe(xs, blk, e_lo, e_hi, n_real, wgu, wd, tm):
    dx = xs.shape[1]
    n_tiles = blk.shape[0]
    n_rows = n_tiles * tm
    _, d, de2 = wgu.shape
    grid_spec = pltpu.PrefetchScalarGridSpec(
        num_scalar_prefetch=4,
        grid=(n_tiles,),
        in_specs=[pl.BlockSpec((tm, dx), lambda i, blk, lo, hi, nr: (blk[i], 0)),
                  pl.BlockSpec((1, d, de2), lambda i, blk, lo, hi, nr: (lo[i], 0, 0)),
                  pl.BlockSpec((1, de2 // 2, d), lambda i, blk, lo, hi, nr: (lo[i], 0, 0)),
                  pl.BlockSpec((1, d, de2), lambda i, blk, lo, hi, nr: (hi[i], 0, 0)),
                  pl.BlockSpec((1, de2 // 2, d), lambda i, blk, lo, hi, nr: (hi[i], 0, 0))],
        out_specs=pl.BlockSpec((tm, d), lambda i, blk, lo, hi, nr: (i, 0)))
    return pl.pallas_call(
        _moe_kernel,
        grid_spec=grid_spec,
        out_shape=jax.ShapeDtypeStruct((n_rows, d), F32),
        compiler_params=_cparams(("arbitrary",)),
        name="moe",
    )(blk, e_lo, e_hi, n_real, xs, wgu, wd, wgu, wd)


def _combine_kernel(slot_ref, slot_next_ref, ys_ref, h_ref, g2_ref, g_ref, o_ref, buf_ref, sem, *, final, transposed):
    i = pl.program_id(0) * pl.num_programs(1) + pl.program_id(1)
    n = pl.num_programs(0) * pl.num_programs(1)
    tm = o_ref.shape[1]

    def gather(s_ref, b):
        def issue(r, carry):
            _row_copy(ys_ref, s_ref[0, 0, r], buf_ref.at[b], r, sem.at[b]).start()
            return carry
        lax.fori_loop(0, tm, issue, 0, unroll=DMA_UNROLL)

    cur = i % 2

    @pl.when(i == 0)
    def _():
        gather(slot_ref, 0)

    @pl.when(i + 1 < n)
    def _():
        gather(slot_next_ref, 1 - cur)

    pltpu.make_async_copy(ys_ref.at[pl.ds(0, tm)], buf_ref.at[cur], sem.at[cur]).wait()
    h = _load_tile(h_ref, transposed) + g2_ref[0] * buf_ref[cur]
    if final:
        ms = jnp.mean(h * h, axis=-1, keepdims=True)
        h = (h * lax.rsqrt(ms + EPS)) * g_ref[...]
    o_ref[0] = h


def _combine(ys, slot, h, g2, g, tm, final, transposed):
    bsz, t, d = h.shape
    nj = t // tm
    nt = bsz * nj
    per_batch = g2.shape[0] == bsz
    slot3 = slot.reshape(nt, 1, tm)
    smem = functools.partial(pl.BlockSpec, (1, 1, tm), memory_space=pltpu.SMEM)
    return pl.pallas_call(
        functools.partial(_combine_kernel, final=final, transposed=transposed),
        grid=(bsz, nj),
        in_specs=[smem(lambda bi, j: (bi * nj + j, 0, 0)),
                  smem(lambda bi, j: (jnp.minimum(bi * nj + j + 1, nt - 1), 0, 0)),
                  pl.BlockSpec(memory_space=pl.ANY),
                  _tile_spec(t, tm, d, transposed, to_natural=True),
                  pl.BlockSpec((1, 1, d), (lambda bi, j: (bi, 0, 0)) if per_batch else (lambda bi, j: (0, 0, 0))),
                  pl.BlockSpec((1, d), lambda bi, j: (0, 0))],
        out_specs=pl.BlockSpec((1, tm, d), lambda bi, j: (bi, j, 0)),
        out_shape=jax.ShapeDtypeStruct((bsz, t, d), F32),
        scratch_shapes=[pltpu.VMEM((2, tm, d), F32), pltpu.SemaphoreType.DMA((2,))],
        compiler_params=_cparams(("arbitrary", "arbitrary")),
        name="moe_combine",
    )(slot3, slot3, ys, _tile_view(h, transposed, to_natural=True), g2, g.reshape(1, d))


def _moe_layer(vx, route, wgu, wd, h, g2, g, final, transposed):
    bsz, t, d = h.shape
    n = bsz * t
    tmo = MOE_TILE
    while tmo > MOE_TILE_MIN and n < 2 * N_CLASSES * tmo:
        tmo //= 2
    n_max = n // tmo + N_CLASSES
    vx = vx.reshape(n, d + ROUTER_LANES)
    slot, blk, e_lo, e_hi, n_real, fill = _routing_tables(route, tmo, n_max)
    xs = _dispatch(vx, slot, fill, n_max * tmo, tmo, min(DISPATCH_TILE, n))
    ys = _moe(xs, blk, e_lo, e_hi, n_real, wgu, wd, tmo)
    if transposed:
        slot = slot.reshape(bsz, GRID_W, t // GRID_W).transpose(0, 2, 1)
        tm = SUBLANES * GRID_W
    else:
        tm = min(DISPATCH_TILE, t)
    return _combine(ys, slot, h, g2, g, tm, final, transposed)


def _block_diag(w):
    nh, bw, _ = w.shape
    eye = jnp.eye(nh, dtype=w.dtype)
    return (eye[:, None, :, None] * w[:, :, None, :]).reshape(nh * bw, nh * bw)


def kernel(x, c, ctx, c_ctx, w_mod, b_mod, norm1_g, norm2_g, w_in, conv_w, conv_b, conv_ln_g, conv_ln_b,
           mlstm_b_i, mlstm_b_f, mlstm_norm_g, lru_conv_w, lru_conv_b, lru_w_a, lru_b_a, lru_w_x, lru_b_x,
           lru_lambda, w_out, w_rg, b_rg, w_re, b_re, w_gate, w_up, w_down, final_g):
    bsz, t, d = x.shape
    tc = ctx.shape[1]
    depth = w_mod.shape[0]
    dc = conv_w.shape[2]
    dm = mlstm_norm_g.shape[1]
    dl = lru_lambda.shape[2]
    ngate = N_DIR * MLSTM_HEADS
    tm_c = min(TOKEN_TILE, tc)
    tb = min(SCAN_BLOCK, t)

    rp = -(-(bsz + 1) // SUBLANES) * SUBLANES
    cvec = jnp.zeros((rp, d), F32).at[:bsz].set(c).at[bsz].set(c_ctx)
    mod = _modulation(cvec, w_mod, b_mod)

    c_q = 2 * dc
    c_k = c_q + dm
    c_v = c_k + dm
    c_o = c_v + dm
    c_g = c_o + dm
    c_l = c_g + 2 * ngate
    splits = (2 * dc, dm, dm, 2 * dl)
    out_dtypes = (F32, BF16, F32, F32)
    splits_t = (dm, dm, 2 * ngate)
    out_dtypes_t = (BF16, BF16, F32)

    h_lat, h_ctx = x, ctx
    for l in range(depth):
        last = l == depth - 1
        transposed = l % 2 == 1
        tm = SUBLANES * (t // GRID_W) if transposed else min(TOKEN_TILE, t)
        tm_in = min(IN_TILE, t)
        m_lat = [mod[l, :bsz, k * d:(k + 1) * d].reshape(bsz, 1, d) for k in range(6)]
        m_ctx = [mod[l, bsz:bsz + 1, k * d:(k + 1) * d].reshape(1, 1, d) for k in range(6)]
        wl = w_in[l]
        w_tok = jnp.concatenate([wl[:, :c_q], wl[:, c_k:c_v], wl[:, c_o:c_g], wl[:, c_l:]], axis=1).astype(BF16)
        w_chn = jnp.concatenate([wl[:, c_q:c_k], wl[:, c_v:c_o], wl[:, c_g:c_l]], axis=1).T.astype(BF16)
        gate_bias = jnp.broadcast_to(
            jnp.concatenate([mlstm_b_i[l].reshape(-1), mlstm_b_f[l].reshape(-1)])[:, None], (2 * ngate, MLSTM_CHUNK))
        wax = jnp.stack([jnp.concatenate([_block_diag(lru_w_a[l, dd]), _block_diag(lru_w_x[l, dd])], axis=1)
                         for dd in range(N_DIR)]).astype(BF16)
        bax = jnp.concatenate([lru_b_a[l], lru_b_x[l]], axis=-1).reshape(N_DIR, 1, 2 * dl)
        lam = lru_lambda[l].reshape(N_DIR, 1, dl)
        lcb = lru_conv_b[l].reshape(N_DIR, 1, dl)
        wr = jnp.zeros((d, ROUTER_LANES), F32).at[:, :N_GROUPS].set(w_rg[l]).at[:, N_GROUPS:N_GROUPS + N_EXPERTS].set(w_re[l])
        wrh, wrl = _split_hi_lo(wr)
        br = jnp.zeros((1, ROUTER_LANES), F32).at[0, :N_GROUPS].set(b_rg[l]).at[0, N_GROUPS:N_GROUPS + N_EXPERTS].set(b_re[l])
        wgu = jnp.concatenate([w_gate[l], w_up[l]], axis=-1).astype(BF16)
        wd = w_down[l].astype(BF16)
        wo = w_out[l].astype(BF16)

        proj = functools.partial(_in_proj, norm_g=norm1_g[l], w=w_tok, wt=w_chn, gate_bias=gate_bias, splits=splits,
                                 out_dtypes=out_dtypes, splits_t=splits_t, out_dtypes_t=out_dtypes_t)
        cv_l, k_l, o_l, lru_l, qt_l, vt_l, gt_l = proj(h_lat, m_lat[0], m_lat[1], tm=tm_in, transposed=transposed)
        cv_c, k_c, o_c, lru_c, qt_c, vt_c, gt_c = proj(h_ctx, m_ctx[0], m_ctx[1], tm=tm_c, transposed=False)

        a_l = _conformer_conv(cv_l, conv_w[l], conv_b[l], conv_ln_g[l], conv_ln_b[l], tm_in)
        mcf, mcb, mlf, mlb = _mlstm((qt_c, k_c, vt_c, gt_c), (qt_l, k_l, vt_l, gt_l), tb)
        rcf, rcb, rlf, rlb = _rglru(lru_c, lru_l, lru_conv_w[l], lcb, wax, bax, lam, tb)

        h_lat, vx_l, cnt_l = _out_proj(a_l, mlf, mlb, o_l, rlf, rlb, lru_l, h_lat, m_lat[2], m_lat[3], m_lat[4],
                                       norm2_g[l], mlstm_norm_g[l], wo, wrh, wrl, br, tm, transposed)
        h_lat = _moe_layer(vx_l, cnt_l, wgu, wd, h_lat, m_lat[5], final_g, last, transposed)
        if not last:
            a_c = _conformer_conv(cv_c, conv_w[l], conv_b[l], conv_ln_g[l], conv_ln_b[l], tm_c)
            h_ctx, vx_c, cnt_c = _out_proj(a_c, mcf, mcb, o_c, rcf, rcb, lru_c, h_ctx, m_ctx[2], m_ctx[3], m_ctx[4],
                                           norm2_g[l], mlstm_norm_g[l], wo, wrh, wrl, br, tm_c, False)
            h_ctx = _moe_layer(vx_c, cnt_c, wgu, wd, h_ctx, m_ctx[5], final_g, False, False)
    return h_lat
```

```python
import functools
import math

import jax
import jax.numpy as jnp
from jax import lax
from jax.experimental import pallas as pl
from jax.experimental.pallas import tpu as pltpu

EPS = 1e-6
GRID_W = 64
D_CONV_FRAC = 4
CONV_WIDTH = 31
CONV_HALO = 16
MLSTM_HEADS = 4
MLSTM_CHUNK = 128
LRU_HEADS = 4
LRU_CONV_WIDTH = 4
LRU_C = 8.0
N_DIR = 2
N_GROUPS = 4
EXPERTS_PER_GROUP = 4
N_EXPERTS = N_GROUPS * EXPERTS_PER_GROUP
N_PAIRS = EXPERTS_PER_GROUP * (EXPERTS_PER_GROUP - 1) // 2
N_CLASSES = N_GROUPS * N_PAIRS
ROUTER_LANES = 128
INFO_CLASS, INFO_RANK, INFO_W_LO, INFO_W_HI = 0, 1, 2, 3
SUBLANES = 8
LOG2E = 1.4426950408889634

VMEM_LIMIT = 56 * 1024 * 1024
IN_TILE = 1024
TOKEN_TILE = 512
SCAN_BLOCK = 256
MOE_TILE = 512
MOE_TILE_MIN = 128
DISPATCH_TILE = 512
DMA_UNROLL = 8

F32 = jnp.float32
BF16 = jnp.bfloat16


def _cparams(sem):
    return pltpu.CompilerParams(dimension_semantics=sem, vmem_limit_bytes=VMEM_LIMIT)


def _sigmoid(x):
    return jax.nn.sigmoid(x)


def _log_sigmoid(x):
    return jnp.minimum(x, 0.0) - jnp.log1p(jnp.exp(-jnp.abs(x)))


def _softplus(x):
    return jnp.maximum(x, 0.0) + jnp.log1p(jnp.exp(-jnp.abs(x)))


def _gelu_tanh(x):
    return x * (0.5 * (1.0 + jnp.tanh(0.7978845608028654 * (x + 0.044715 * (x * x * x)))))


def _dot(a, b):
    return jnp.dot(a, b, preferred_element_type=F32)


def _split_hi_lo(x):
    hi = x.astype(BF16)
    lo = (x - hi.astype(F32)).astype(BF16)
    return hi, lo


def _load_tile(ref, transposed):
    if not transposed:
        return ref[0]
    return jnp.concatenate([ref[0, :, w, :] for w in range(ref.shape[2])], axis=0)


def _tile_spec(t, tm, d, transposed, to_natural=False):
    if not transposed:
        return pl.BlockSpec((1, tm, d), lambda b, j: (b, j, 0))
    major = GRID_W if to_natural else t // GRID_W
    assert tm % (SUBLANES * major) == 0
    return pl.BlockSpec((1, major, tm // major, d), lambda b, j: (b, 0, j, 0))


def _tile_view(a, transposed, to_natural=False):
    if not transposed:
        return a
    b, t, d = a.shape
    return a.reshape(b, GRID_W, t // GRID_W, d) if to_natural else a.reshape(b, t // GRID_W, GRID_W, d)


def _mod_kernel(c_ref, w_ref, b_ref, o_ref):
    c = c_ref[...]
    s = (c * _sigmoid(c)).astype(BF16)
    o_ref[0] = _dot(s, w_ref[0].astype(BF16)) + b_ref[0]


def _modulation(cvec, w_mod, b_mod):
    nl, d, d6 = w_mod.shape
    rp = cvec.shape[0]
    tn = d6 // 4
    return pl.pallas_call(
        _mod_kernel,
        grid=(nl, d6 // tn),
        in_specs=[pl.BlockSpec((rp, d), lambda l, j: (0, 0)),
                  pl.BlockSpec((1, d, tn), lambda l, j: (l, 0, j)),
                  pl.BlockSpec((1, 1, tn), lambda l, j: (l, 0, j))],
        out_specs=pl.BlockSpec((1, rp, tn), lambda l, j: (l, 0, j)),
        out_shape=jax.ShapeDtypeStruct((nl, rp, d6), F32),
        compiler_params=_cparams(("arbitrary", "arbitrary")),
        name="modulation",
    )(cvec, w_mod, b_mod.reshape(nl, 1, d6))


def _chunk_scan(x, op, reverse):
    L = MLSTM_CHUNK
    n = x.shape[1]
    pos = lax.broadcasted_iota(jnp.int32, x.shape, 1) & (L - 1)
    sh = 1
    while sh < L:
        if reverse:
            xs, ok = pltpu.roll(x, n - sh, 1), pos < L - sh
        else:
            xs, ok = pltpu.roll(x, sh, 1), pos >= sh
        x = jnp.where(ok, op(x, xs), x)
        sh *= 2
    return x


def _gate_scans(g):
    nrow = g.shape[0] // 2
    fwd_row = lax.broadcasted_iota(jnp.int32, (nrow, g.shape[1]), 0) < nrow // N_DIR
    ig2 = g[:nrow] * LOG2E
    lf2 = _log_sigmoid(g[nrow:]) * LOG2E
    b2 = jnp.where(fwd_row, _chunk_scan(lf2, jnp.add, False), _chunk_scan(lf2, jnp.add, True))
    c2 = ig2 - b2
    cm2 = jnp.where(fwd_row, _chunk_scan(c2, jnp.maximum, False), _chunk_scan(c2, jnp.maximum, True))
    return jnp.concatenate([ig2, b2, cm2], axis=0)


def _in_kernel(h_ref, shift_ref, scale_ref, g_ref, w_ref, wt_ref, gbias_ref, *outs, transposed, splits, splits_t):
    h = _load_tile(h_ref, transposed)
    ms = jnp.mean(h * h, axis=-1, keepdims=True)
    u = (h * lax.rsqrt(ms + EPS)) * g_ref[...]
    u = (u * (1.0 + scale_ref[0]) + shift_ref[0]).astype(BF16)
    c0 = 0
    for o_ref, width in zip(outs, splits):
        o_ref[0] = _dot(u, w_ref[:, c0:c0 + width]).astype(o_ref.dtype)
        c0 += width
    r0 = 0
    outs_t = outs[len(splits):]
    for o_ref, width in zip(outs_t[:-1], splits_t[:-1]):
        o_ref[0] = lax.dot_general(wt_ref[r0:r0 + width, :], u, (((1,), (1,)), ((), ())),
                                   preferred_element_type=F32).astype(o_ref.dtype)
        r0 += width
    gates = lax.dot_general(wt_ref[r0:r0 + splits_t[-1], :], u, (((1,), (1,)), ((), ())), preferred_element_type=F32)
    tm = gates.shape[1]
    outs_t[-1][0] = _gate_scans(gates + jnp.concatenate([gbias_ref[...]] * (tm // MLSTM_CHUNK), axis=1))


def _in_proj(h, shift, scale, norm_g, w, wt, gate_bias, splits, out_dtypes, splits_t, out_dtypes_t, tm, transposed):
    b, t, d = h.shape
    per_batch = shift.shape[0] == b
    mod_spec = pl.BlockSpec((1, 1, d), (lambda bi, j: (bi, 0, 0)) if per_batch else (lambda bi, j: (0, 0, 0)))
    rows_t = splits_t[:-1] + (splits_t[-1] // 2 * 3,)
    return pl.pallas_call(
        functools.partial(_in_kernel, transposed=transposed, splits=splits, splits_t=splits_t),
        grid=(b, t // tm),
        in_specs=[_tile_spec(t, tm, d, transposed), mod_spec, mod_spec, pl.BlockSpec((1, d), lambda bi, j: (0, 0)),
                  pl.BlockSpec(w.shape, lambda bi, j: (0, 0)), pl.BlockSpec(wt.shape, lambda bi, j: (0, 0)),
                  pl.BlockSpec(gate_bias.shape, lambda bi, j: (0, 0))],
        out_specs=([pl.BlockSpec((1, tm, width), lambda bi, j: (bi, j, 0)) for width in splits]
                   + [pl.BlockSpec((1, width, tm), lambda bi, j: (bi, 0, j)) for width in rows_t]),
        out_shape=([jax.ShapeDtypeStruct((b, t, width), dt) for width, dt in zip(splits, out_dtypes)]
                   + [jax.ShapeDtypeStruct((b, width, t), dt) for width, dt in zip(rows_t, out_dtypes_t)]),
        compiler_params=_cparams(("arbitrary", "arbitrary")),
        name="in_proj",
    )(_tile_view(h, transposed), shift, scale, norm_g.reshape(1, d), w, wt, gate_bias)


def _conv_kernel(prev_ref, cur_ref, next_ref, w_ref, b_ref, lng_ref, lnb_ref, o_ref, *, tb, nblk, dc):
    j = pl.program_id(1)

    def glu(x):
        return x[:, :dc] * _sigmoid(x[:, dc:])

    up = jnp.where(j > 0, glu(prev_ref[0]), 0.0)
    un = jnp.where(j < nblk - 1, glu(next_ref[0]), 0.0)
    ext = jnp.concatenate([up, glu(cur_ref[0]), un], axis=0)
    base = CONV_HALO - CONV_WIDTH // 2
    nrows = tb + 2 * CONV_HALO
    acc = jnp.zeros((tb, dc), F32)
    for b in range(SUBLANES):
        shifted = ext if b == 0 else pltpu.roll(ext, nrows - b, 0)
        for a in range((base + CONV_WIDTH - 1 - b) // SUBLANES + 1):
            k = SUBLANES * a + b - base
            if 0 <= k < CONV_WIDTH:
                acc = acc + w_ref[k:k + 1, :] * shifted[SUBLANES * a:SUBLANES * a + tb, :]
    acc = acc + b_ref[...]
    mu = jnp.mean(acc, axis=-1, keepdims=True)
    xc = acc - mu
    y = xc * lax.rsqrt(jnp.mean(xc * xc, axis=-1, keepdims=True) + EPS) * lng_ref[...] + lnb_ref[...]
    o_ref[0] = (y * _sigmoid(y)).astype(o_ref.dtype)


def _conformer_conv(cv, w, b, ln_g, ln_b, tb):
    bsz, t, c2 = cv.shape
    dc = c2 // 2
    nblk = t // tb
    hb = tb // CONV_HALO
    nh = t // CONV_HALO
    wp = jnp.zeros((CONV_WIDTH + 1, dc), F32).at[:CONV_WIDTH].set(w)
    vec = pl.BlockSpec((1, dc), lambda bi, j: (0, 0))
    return pl.pallas_call(
        functools.partial(_conv_kernel, tb=tb, nblk=nblk, dc=dc),
        grid=(bsz, nblk),
        in_specs=[pl.BlockSpec((1, CONV_HALO, c2), lambda bi, j: (bi, jnp.maximum(j * hb - 1, 0), 0)),
                  pl.BlockSpec((1, tb, c2), lambda bi, j: (bi, j, 0)),
                  pl.BlockSpec((1, CONV_HALO, c2), lambda bi, j: (bi, jnp.minimum((j + 1) * hb, nh - 1), 0)),
                  pl.BlockSpec((CONV_WIDTH + 1, dc), lambda bi, j: (0, 0)), vec, vec, vec],
        out_specs=pl.BlockSpec((1, tb, dc), lambda bi, j: (bi, j, 0)),
        out_shape=jax.ShapeDtypeStruct((bsz, t, dc), BF16),
        compiler_params=_cparams(("arbitrary", "arbitrary")),
        name="conformer_conv",
    )(cv, cv, cv, wp, b.reshape(1, dc), ln_g.reshape(1, dc), ln_b.reshape(1, dc))


def _mlstm_block(qt_ref, k_ref, vt_ref, gt_ref, out_ref, c_ref, n_ref, m_ref, d, nchunks, hd):
    L = MLSTM_CHUNK
    nh = MLSTM_HEADS
    assert hd == L
    nrow = N_DIR * nh
    log2_scale = math.log2(hd ** -0.5)
    ig2, b2, cm2 = gt_ref[0, 0:nrow], gt_ref[0, nrow:2 * nrow], gt_ref[0, 2 * nrow:3 * nrow]
    c2 = ig2 - b2
    row_s = lax.broadcasted_iota(jnp.int32, (L, L), 0)
    col_t = lax.broadcasted_iota(jnp.int32, (L, L), 1)
    mask = (row_s <= col_t) if d == 0 else (row_s >= col_t)
    ones16 = jnp.ones((2 * SUBLANES, L), BF16)
    for ci in (range(nchunks) if d == 0 else range(nchunks - 1, -1, -1)):
        sl = slice(ci * L, (ci + 1) * L)
        b2c, c2c, ig2c = b2[:, sl], c2[:, sl], ig2[:, sl]
        m2 = m_ref[d]
        a2 = -jnp.maximum(m2, cm2[:, sl])
        inter = jnp.exp2(m2 + a2)
        edn = jnp.exp2(a2 - b2c)
        b_last = jnp.broadcast_to(b2c[:, L - 1:L] if d == 0 else b2c[:, 0:1], (nrow, L))
        logw2 = b_last - b2c + ig2c
        m2_new = jnp.maximum(b_last + m2, jnp.broadcast_to(jnp.max(logw2, axis=-1, keepdims=True), (nrow, L)))
        wgt = jnp.exp2(logw2 - m2_new + log2_scale)
        decay = jnp.exp2(b_last + m2 - m2_new)
        m_ref[d] = m2_new
        c_cols = (c2c + log2_scale).T
        for h in range(nh):
            r = d * nh + h
            e = c_cols[:, r:r + 1] + a2[r:r + 1]
            p = jnp.where(mask, jnp.exp2(e), 0.0)
            qt = qt_ref[0, h * hd:(h + 1) * hd, sl]
            k = k_ref[0, ci * L:(ci + 1) * L, h * hd:(h + 1) * hd]
            vt = vt_ref[0, h * hd:(h + 1) * hd, sl]
            ct = c_ref[r]
            n16 = n_ref[r]
            st = _dot(k, qt) * p
            nq = _dot(n16.astype(BF16), qt)
            nd = _dot(vt, st.astype(BF16))
            inter_r = inter[r:r + 1]
            num = nd + inter_r * _dot(ct.astype(BF16), qt)
            den = jnp.sum(st, axis=0, keepdims=True) + inter_r * nq[0:1]
            ht = num * (1.0 / jnp.maximum(jnp.abs(den), edn[r:r + 1]))
            out_ref[0, ci * L:(ci + 1) * L, h * hd:(h + 1) * hd] = ht.T
            w_r = wgt[r:r + 1]
            vw = (vt.astype(F32) * w_r).astype(BF16)
            upd = _dot(jnp.concatenate([vw, ones16 * w_r.astype(BF16)], axis=0), k)
            dec = decay[r:r + 1]
            c_ref[r] = dec * ct + upd[:hd]
            n_ref[r] = dec * n16 + upd[hd:]


def _mlstm_kernel(qtc_ref, kc_ref, vtc_ref, gtc_ref, qtf_ref, kf_ref, vtf_ref, gtf_ref,
                  qtb_ref, kb_ref, vtb_ref, gtb_ref,
                  hcf_ref, hcb_ref, hf_ref, hb_ref, c_ref, n_ref, m_ref, *, nc_ctx, nc_lat, hd):
    s = pl.program_id(1)
    state = (c_ref, n_ref, m_ref)

    @pl.when(s == 0)
    def _():
        c_ref[...] = jnp.zeros_like(c_ref)
        n_ref[...] = jnp.zeros_like(n_ref)
        m_ref[...] = jnp.zeros_like(m_ref)
        _mlstm_block(qtc_ref, kc_ref, vtc_ref, gtc_ref, hcf_ref, *state, 0, nc_ctx, hd)
        _mlstm_block(qtc_ref, kc_ref, vtc_ref, gtc_ref, hcb_ref, *state, 1, nc_ctx, hd)

    @pl.when(s > 0)
    def _():
        _mlstm_block(qtf_ref, kf_ref, vtf_ref, gtf_ref, hf_ref, *state, 0, nc_lat, hd)
        _mlstm_block(qtb_ref, kb_ref, vtb_ref, gtb_ref, hb_ref, *state, 1, nc_lat, hd)


def _mlstm(ctx_in, lat_in, tb):
    qt_c, k_c, vt_c, gt_c = ctx_in
    qt_l, k_l, vt_l, gt_l = lat_in
    bsz, tc, dm = k_c.shape
    t = k_l.shape[1]
    ng = gt_l.shape[1]
    hd = dm // MLSTM_HEADS
    nb = t // tb
    nstate = N_DIR * MLSTM_HEADS

    def fwd(s):
        return jnp.maximum(s - 1, 0)

    def bwd(s):
        return nb - 1 - jnp.maximum(s - 1, 0)

    def specs(tlen, blk):
        return [pl.BlockSpec((1, dm, tlen), lambda bi, s: (bi, 0, blk(s))),
                pl.BlockSpec((1, tlen, dm), lambda bi, s: (bi, blk(s), 0)),
                pl.BlockSpec((1, dm, tlen), lambda bi, s: (bi, 0, blk(s))),
                pl.BlockSpec((1, ng, tlen), lambda bi, s: (bi, 0, blk(s)))]

    def out(tlen, blk):
        return pl.BlockSpec((1, tlen, dm), lambda bi, s: (bi, blk(s), 0))

    first = lambda s: 0
    return pl.pallas_call(
        functools.partial(_mlstm_kernel, nc_ctx=tc // MLSTM_CHUNK, nc_lat=tb // MLSTM_CHUNK, hd=hd),
        grid=(bsz, nb + 1),
        in_specs=specs(tc, first) + specs(tb, fwd) + specs(tb, bwd),
        out_specs=[out(tc, first), out(tc, first), out(tb, fwd), out(tb, bwd)],
        out_shape=[jax.ShapeDtypeStruct((bsz, tc, dm), F32), jax.ShapeDtypeStruct((bsz, tc, dm), F32),
                   jax.ShapeDtypeStruct((bsz, t, dm), F32), jax.ShapeDtypeStruct((bsz, t, dm), F32)],
        scratch_shapes=[pltpu.VMEM((nstate, hd, hd), F32), pltpu.VMEM((nstate, 2 * SUBLANES, hd), F32),
                        pltpu.VMEM((N_DIR, nstate, MLSTM_CHUNK), F32)],
        compiler_params=_cparams(("arbitrary", "arbitrary")),
        name="mlstm",
    )(*ctx_in, *lat_in, *lat_in)


def _lru_block(x, halo, d, cw, cb, wax, bax, lam, h0):
    tb, dl = x.shape
    kw = LRU_CONV_WIDTH
    if d == 0:
        ext = jnp.concatenate([halo, x], axis=0)
        taps = [ext[SUBLANES - (kw - 1) + j:SUBLANES - (kw - 1) + j + tb] for j in range(kw)]
    else:
        ext = jnp.concatenate([x, halo], axis=0)
        taps = [ext[kw - 1 - j:kw - 1 - j + tb] for j in range(kw)]
    xc = cb
    for j in range(kw):
        xc = xc + cw[j:j + 1, :] * taps[j]
    ri = _sigmoid(_dot(xc.astype(BF16), wax) + bax)
    r, i = ri[:, :dl], ri[:, dl:]
    log_a = (-LRU_C * r) * _softplus(-lam)
    a = jnp.exp(log_a)
    th = jnp.tanh(log_a)
    u = jnp.sqrt(-2.0 * th / (1.0 - th)) * (i * xc)
    ngrp = tb // SUBLANES
    a = a.reshape(ngrp, SUBLANES, dl)
    u = u.reshape(ngrp, SUBLANES, dl)
    sub = lax.broadcasted_iota(jnp.int32, a.shape, 1)
    for sh in (1, 2, 4):
        if d == 0:
            a_s, u_s, msk = pltpu.roll(a, sh, 1), pltpu.roll(u, sh, 1), sub >= sh
        else:
            a_s, u_s, msk = pltpu.roll(a, SUBLANES - sh, 1), pltpu.roll(u, SUBLANES - sh, 1), sub < SUBLANES - sh
        u = jnp.where(msk, a * u_s + u, u)
        a = jnp.where(msk, a * a_s, a)
    a = a.reshape(tb, dl)
    u = u.reshape(tb, dl)
    outs = [None] * ngrp
    carry = h0
    for j in (range(ngrp) if d == 0 else range(ngrp - 1, -1, -1)):
        hj = u[j * SUBLANES:(j + 1) * SUBLANES] + a[j * SUBLANES:(j + 1) * SUBLANES] * carry
        outs[j] = hj
        carry = hj[SUBLANES - 1:SUBLANES] if d == 0 else hj[0:1]
    return jnp.concatenate(outs, axis=0), carry


def _lru_kernel(xc_ref, xf_ref, xb_ref, cw_ref, cb_ref, wax_ref, bax_ref, lam_ref,
                hcf_ref, hcb_ref, hf_ref, hb_ref, hcar_ref, halo_ref):
    s = pl.program_id(1)

    def params(d):
        return cw_ref[d], cb_ref[d], wax_ref[d], bax_ref[d], lam_ref[d]

    @pl.when(s == 0)
    def _():
        x = xc_ref[0]
        zero_halo = jnp.zeros((SUBLANES, x.shape[1]), F32)
        zero_h = jnp.zeros((1, x.shape[1]), F32)
        for d, o_ref in ((0, hcf_ref), (1, hcb_ref)):
            h, carry = _lru_block(x, zero_halo, d, *params(d), zero_h)
            o_ref[0] = h
            hcar_ref[d] = jnp.broadcast_to(carry, hcar_ref.shape[1:])

    @pl.when(s > 0)
    def _():
        for d, x_ref, o_ref in ((0, xf_ref, hf_ref), (1, xb_ref, hb_ref)):
            x = x_ref[0]
            halo = jnp.where(s > 1, halo_ref[d], 0.0)
            h, carry = _lru_block(x, halo, d, *params(d), hcar_ref[d][0:1])
            o_ref[0] = h
            hcar_ref[d] = jnp.broadcast_to(carry, hcar_ref.shape[1:])
            halo_ref[d] = x[x.shape[0] - SUBLANES:] if d == 0 else x[:SUBLANES]


def _rglru(lru_c, lru_l, cw, cb, wax, bax, lam, tb):
    bsz, tc, w2 = lru_c.shape
    t = lru_l.shape[1]
    dl = w2 // 2
    nb = t // tb

    def fwd(bi, s):
        return (bi, jnp.maximum(s - 1, 0), 0)

    def bwd(bi, s):
        return (bi, nb - 1 - jnp.maximum(s - 1, 0), 0)

    def ctx(bi, s):
        return (bi, 0, 0)

    def whole(a):
        return pl.BlockSpec(a.shape, lambda bi, s: (0,) * a.ndim)

    return pl.pallas_call(
        _lru_kernel,
        grid=(bsz, nb + 1),
        in_specs=[pl.BlockSpec((1, tc, dl), ctx), pl.BlockSpec((1, tb, dl), fwd), pl.BlockSpec((1, tb, dl), bwd),
                  whole(cw), whole(cb), whole(wax), whole(bax), whole(lam)],
        out_specs=[pl.BlockSpec((1, tc, dl), ctx), pl.BlockSpec((1, tc, dl), ctx),
                   pl.BlockSpec((1, tb, dl), fwd), pl.BlockSpec((1, tb, dl), bwd)],
        out_shape=[jax.ShapeDtypeStruct((bsz, tc, dl), F32), jax.ShapeDtypeStruct((bsz, tc, dl), F32),
                   jax.ShapeDtypeStruct((bsz, t, dl), F32), jax.ShapeDtypeStruct((bsz, t, dl), F32)],
        scratch_shapes=[pltpu.VMEM((N_DIR, SUBLANES, dl), F32), pltpu.VMEM((N_DIR, SUBLANES, dl), F32)],
        compiler_params=_cparams(("arbitrary", "arbitrary")),
        name="rglru",
    )(lru_c, lru_l, lru_l, cw, cb, wax, bax, lam)


def _route(logits):
    ng, ne = N_GROUPS, EXPERTS_PER_GROUP
    lane = lax.broadcasted_iota(jnp.int32, logits.shape, 1).astype(F32)
    big = float(ROUTER_LANES)
    is_g = lane < ng
    gl = jnp.where(is_g, logits, -jnp.inf)
    gmax = jnp.max(gl, axis=-1, keepdims=True)
    g_sel = jnp.min(jnp.where(gl == gmax, lane, big), axis=-1, keepdims=True)
    p_g = 1.0 / jnp.sum(jnp.where(is_g, jnp.exp(logits - gmax), 0.0), axis=-1, keepdims=True)
    lo = ng + ne * g_sel
    el = jnp.where((lane >= lo) & (lane < lo + ne), logits, -jnp.inf)
    v1 = jnp.max(el, axis=-1, keepdims=True)
    i1 = jnp.min(jnp.where(el == v1, lane, big), axis=-1, keepdims=True)
    el2 = jnp.where(lane == i1, -jnp.inf, el)
    v2 = jnp.max(el2, axis=-1, keepdims=True)
    i2 = jnp.min(jnp.where(el2 == v2, lane, big), axis=-1, keepdims=True)
    e2 = jnp.exp(v2 - v1)
    w1 = p_g / (1.0 + e2)
    w2 = p_g * e2 / (1.0 + e2)
    first_lower = i1 < i2
    e_lo = jnp.minimum(i1, i2) - lo
    e_hi = jnp.maximum(i1, i2) - lo
    pair = e_lo * (2 * ne - 1 - e_lo) * 0.5 + (e_hi - e_lo - 1.0)
    return g_sel * N_PAIRS + pair, jnp.where(first_lower, w1, w2), jnp.where(first_lower, w2, w1)


def _out_kernel(a_ref, mf_ref, mb_ref, o_ref, lf_ref, lb_ref, lg_ref, h_ref, g1_ref, sh2_ref, sc2_ref,
                n2g_ref, mng_ref, wout_ref, wrh_ref, wrl_ref, br_ref,
                hnew_ref, vx_ref, cnt_ref, info_ref, run_ref, *, transposed, d, hd):
    @pl.when((pl.program_id(0) == 0) & (pl.program_id(1) == 0))
    def _():
        run_ref[...] = jnp.zeros_like(run_ref)

    dc = a_ref.shape[2]
    dm = mf_ref.shape[2]
    mh = mf_ref[0] + mb_ref[0]
    parts = []
    for h in range(dm // hd):
        x = mh[:, h * hd:(h + 1) * hd]
        xc = x - jnp.mean(x, axis=-1, keepdims=True)
        parts.append(xc * lax.rsqrt(jnp.mean(xc * xc, axis=-1, keepdims=True) + EPS) * mng_ref[:, h * hd:(h + 1) * hd])
    m_out = (_sigmoid(o_ref[0]) * jnp.concatenate(parts, axis=-1)).astype(BF16)
    r_out = ((lf_ref[0] + lb_ref[0]) * _gelu_tanh(lg_ref[0])).astype(BF16)
    y = (_dot(a_ref[0], wout_ref[0:dc, :]) + _dot(m_out, wout_ref[dc:dc + dm, :])
         + _dot(r_out, wout_ref[dc + dm:, :]))
    hn = _load_tile(h_ref, transposed) + g1_ref[0] * y
    hnew_ref[0] = hn
    ms = jnp.mean(hn * hn, axis=-1, keepdims=True)
    v = (hn * lax.rsqrt(ms + EPS)) * n2g_ref[...]
    v = v * (1.0 + sc2_ref[0]) + sh2_ref[0]
    vh, vl = _split_hi_lo(v)
    logits = _dot(vh, wrh_ref[...]) + _dot(vl, wrh_ref[...]) + _dot(vh, wrl_ref[...]) + br_ref[...]
    cls, w_lo, w_hi = _route(logits)
    tm = v.shape[0]
    lane = lax.broadcasted_iota(jnp.int32, (tm, ROUTER_LANES), 1).astype(F32)
    onehot = jnp.where(lane == cls, 1.0, 0.0)
    row = lax.broadcasted_iota(jnp.int32, (tm, tm), 0)
    col = lax.broadcasted_iota(jnp.int32, (tm, tm), 1)
    before = _dot(jnp.where(col < row, 1.0, 0.0).astype(BF16), onehot.astype(BF16)) + run_ref[0:1, :]
    rank = jnp.sum(onehot * before, axis=-1, keepdims=True)
    run_ref[...] = run_ref[...] + jnp.sum(onehot, axis=0, keepdims=True)
    cnt_ref[...] = run_ref[...]
    info = (jnp.where(lane == INFO_CLASS, cls, 0.0) + jnp.where(lane == INFO_RANK, rank, 0.0)
            + jnp.where(lane == INFO_W_LO, w_lo, 0.0) + jnp.where(lane == INFO_W_HI, w_hi, 0.0))
    vx_ref[0, :, :d] = v
    vx_ref[0, :, d:] = info
    info_ref[0] = info.T[:SUBLANES]


def _out_proj(a, mf, mb, o, lf, lb, lru, h, g1, sh2, sc2, n2g, mng, w_out, wrh, wrl, br, tm, transposed):
    bsz, t, d = h.shape
    dc, dm, dl = a.shape[2], mf.shape[2], lf.shape[2]
    per_batch = g1.shape[0] == bsz
    mod_spec = pl.BlockSpec((1, 1, d), (lambda bi, j: (bi, 0, 0)) if per_batch else (lambda bi, j: (0, 0, 0)))

    def scan(width, blk=0):
        return pl.BlockSpec((1, tm, width), lambda bi, j: (bi, j, blk))

    def whole(x):
        return pl.BlockSpec(x.shape, lambda bi, j: (0,) * x.ndim)

    dx = d + ROUTER_LANES
    n2g, mng = n2g.reshape(1, d), mng.reshape(1, dm)
    hv = _tile_view(h, transposed)
    hnew, vx, cnt, info_t = pl.pallas_call(
        functools.partial(_out_kernel, transposed=transposed, d=d, hd=dm // MLSTM_HEADS),
        grid=(bsz, t // tm),
        in_specs=[scan(dc), scan(dm), scan(dm), scan(dm), scan(dl), scan(dl), scan(dl, 1),
                  _tile_spec(t, tm, d, transposed),
                  mod_spec, mod_spec, mod_spec, whole(n2g), whole(mng), whole(w_out), whole(wrh), whole(wrl), whole(br)],
        out_specs=[scan(d), scan(dx), pl.BlockSpec((SUBLANES, ROUTER_LANES), lambda bi, j: (0, 0)),
                   pl.BlockSpec((1, SUBLANES, tm), lambda bi, j: (bi, 0, j))],
        out_shape=[jax.ShapeDtypeStruct((bsz, t, d), F32), jax.ShapeDtypeStruct((bsz, t, dx), F32),
                   jax.ShapeDtypeStruct((SUBLANES, ROUTER_LANES), F32), jax.ShapeDtypeStruct((bsz, SUBLANES, t), F32)],
        scratch_shapes=[pltpu.VMEM((SUBLANES, ROUTER_LANES), F32)],
        compiler_params=_cparams(("arbitrary", "arbitrary")),
        name="out_proj",
    )(a, mf, mb, o, lf, lb, lru, hv, g1, sh2, sc2, n2g, mng, w_out, wrh, wrl, br)
    return hnew, vx, (cnt, info_t)


def _routing_tables(route, tmo, n_max):
    cnt, info_t = route
    cls = info_t[:, INFO_CLASS, :].reshape(-1).astype(jnp.int32)
    rank = info_t[:, INFO_RANK, :].reshape(-1).astype(jnp.int32)
    counts = cnt[0, :N_CLASSES].astype(jnp.int32)
    tiles = (counts + tmo - 1) // tmo
    tile_end = jnp.cumsum(tiles)
    tile_start = tile_end - tiles
    n_real = tile_end[N_CLASSES - 1]
    slot = jnp.take(tile_start * tmo, cls) + rank
    blk = jnp.minimum(jnp.arange(n_max, dtype=jnp.int32), n_real - 1)
    tile_cls = jnp.minimum(jnp.sum((tile_end[None, :] <= blk[:, None]).astype(jnp.int32), axis=1), N_CLASSES - 1)
    pair_lo = jnp.array([a for a in range(EXPERTS_PER_GROUP) for b in range(a + 1, EXPERTS_PER_GROUP)], jnp.int32)
    pair_hi = jnp.array([b for a in range(EXPERTS_PER_GROUP) for b in range(a + 1, EXPERTS_PER_GROUP)], jnp.int32)
    grp = tile_cls // N_PAIRS
    e_lo = grp * EXPERTS_PER_GROUP + jnp.take(pair_lo, tile_cls % N_PAIRS)
    e_hi = grp * EXPERTS_PER_GROUP + jnp.take(pair_hi, tile_cls % N_PAIRS)
    tail = n_real + jnp.arange(N_CLASSES, dtype=jnp.int32)
    fill = jnp.concatenate([jnp.where(tiles > 0, tile_end - 1, -1), jnp.where(tail < n_max, tail, -1)])
    return slot, blk, e_lo, e_hi, n_real.reshape(1), fill.astype(jnp.int32)


def _dispatch_kernel(fill_ref, slot_ref, vx_ref, xs_ref, zero_ref, zsem, sem):
    tm = vx_ref.shape[0] * SUBLANES
    tmo = zero_ref.shape[0]

    @pl.when(pl.program_id(0) == 0)
    def _():
        zero_ref[...] = jnp.zeros_like(zero_ref)

        def fill(j):
            return pltpu.make_async_copy(zero_ref, xs_ref.at[pl.ds(pl.multiple_of(fill_ref[j] * tmo, tmo), tmo)], zsem)

        for j in range(fill_ref.shape[0]):
            pl.when(fill_ref[j] >= 0)(lambda j=j: fill(j).start())
        for j in range(fill_ref.shape[0]):
            pl.when(fill_ref[j] >= 0)(lambda j=j: fill(j).wait())

    def issue(g, carry):
        for p in range(SUBLANES):
            dst = xs_ref.at[pl.ds(slot_ref[0, 0, g * SUBLANES + p], 1)]
            pltpu.make_async_copy(vx_ref.at[g, pl.ds(p, 1)], dst, sem).start()
        return carry

    lax.fori_loop(0, tm // SUBLANES, issue, 0, unroll=DMA_UNROLL)
    rows = xs_ref.at[pl.ds(0, tm)]
    pltpu.make_async_copy(rows, rows, sem).wait()


def _dispatch(vx, slot, fill, n_rows, tmo, tm):
    n, dx = vx.shape
    grid_spec = pltpu.PrefetchScalarGridSpec(
        num_scalar_prefetch=1,
        grid=(n // tm,),
        in_specs=[pl.BlockSpec((1, 1, tm), lambda i, pad: (i, 0, 0), memory_space=pltpu.SMEM),
                  pl.BlockSpec((tm // SUBLANES, SUBLANES, dx), lambda i, pad: (i, 0, 0))],
        out_specs=pl.BlockSpec(memory_space=pl.ANY),
        scratch_shapes=[pltpu.VMEM((tmo, dx), F32), pltpu.SemaphoreType.DMA(()), pltpu.SemaphoreType.DMA(())])
    return pl.pallas_call(
        _dispatch_kernel,
        grid_spec=grid_spec,
        out_shape=jax.ShapeDtypeStruct((n_rows, dx), F32),
        compiler_params=_cparams(("arbitrary",)),
        name="moe_dispatch",
    )(fill, slot.reshape(n // tm, 1, tm), vx.reshape(n // SUBLANES, SUBLANES, dx))


def _expert(x, wg_ref, wu_ref, wd_ref):
    g = _dot(x, wg_ref[0])
    return _dot((g * _sigmoid(g) * _dot(x, wu_ref[0])).astype(BF16), wd_ref[0])


def _moe_kernel(blk_ref, elo_ref, ehi_ref, nreal_ref, xs_ref, wg_lo_ref, wu_lo_ref, wd_lo_ref,
                wg_hi_ref, wu_hi_ref, wd_hi_ref, ys_ref):
    del blk_ref, elo_ref, ehi_ref
    d = ys_ref.shape[1]

    @pl.when(pl.program_id(0) < nreal_ref[0])
    def _():
        x = xs_ref[:, :d].astype(BF16)
        w_lo = xs_ref[:, d + INFO_W_LO:d + INFO_W_LO + 1]
        w_hi = xs_ref[:, d + INFO_W_HI:d + INFO_W_HI + 1]
        ys_ref[...] = (w_lo * _expert(x, wg_lo_ref, wu_lo_ref, wd_lo_ref)
                       + w_hi * _expert(x, wg_hi_ref, wu_hi_ref, wd_hi_ref))

    @pl.when(pl.program_id(0) >= nreal_ref[0])
    def _():
        ys_ref[...] = jnp.zeros_like(ys_ref)


def _moe(xs, blk, e_lo, e_hi, n_real, experts, tm):
    wg, wu, wd = experts
    dx = xs.shape[1]
    n_tiles = blk.shape[0]
    n_rows = n_tiles * tm
    _, d, de = wg.shape
    up = pl.BlockSpec((1, d, de), lambda i, blk, lo, hi, nr: (lo[i], 0, 0)), \
        pl.BlockSpec((1, d, de), lambda i, blk, lo, hi, nr: (hi[i], 0, 0))
    down = pl.BlockSpec((1, de, d), lambda i, blk, lo, hi, nr: (lo[i], 0, 0)), \
        pl.BlockSpec((1, de, d), lambda i, blk, lo, hi, nr: (hi[i], 0, 0))
    grid_spec = pltpu.PrefetchScalarGridSpec(
        num_scalar_prefetch=4,
        grid=(n_tiles,),
        in_specs=[pl.BlockSpec((tm, dx), lambda i, blk, lo, hi, nr: (blk[i], 0)),
                  up[0], up[0], down[0], up[1], up[1], down[1]],
        out_specs=pl.BlockSpec((tm, d), lambda i, blk, lo, hi, nr: (i, 0)))
    return pl.pallas_call(
        _moe_kernel,
        grid_spec=grid_spec,
        out_shape=jax.ShapeDtypeStruct((n_rows, d), F32),
        compiler_params=_cparams(("arbitrary",)),
        name="moe",
    )(blk, e_lo, e_hi, n_real, xs, wg, wu, wd, wg, wu, wd)


def _combine_kernel(slot_ref, slot_next_ref, ys_ref, h_ref, g2_ref, g_ref, o_ref, buf_ref, sem, *, final, transposed):
    i = pl.program_id(0) * pl.num_programs(1) + pl.program_id(1)
    n = pl.num_programs(0) * pl.num_programs(1)
    tm = o_ref.shape[1]

    def gather(s_ref, b):
        def issue(g, carry):
            for p in range(SUBLANES):
                src = ys_ref.at[pl.ds(s_ref[0, 0, g * SUBLANES + p], 1)]
                pltpu.make_async_copy(src, buf_ref.at[b, g, pl.ds(p, 1)], sem.at[b]).start()
            return carry
        lax.fori_loop(0, tm // SUBLANES, issue, 0, unroll=DMA_UNROLL)

    cur = i % 2

    @pl.when(i == 0)
    def _():
        gather(slot_ref, 0)

    @pl.when(i + 1 < n)
    def _():
        gather(slot_next_ref, 1 - cur)

    rows = ys_ref.at[pl.ds(0, tm)]
    pltpu.make_async_copy(rows, rows, sem.at[cur]).wait()
    h = _load_tile(h_ref, transposed) + g2_ref[0] * buf_ref[cur].reshape(tm, -1)
    if final:
        ms = jnp.mean(h * h, axis=-1, keepdims=True)
        h = (h * lax.rsqrt(ms + EPS)) * g_ref[...]
    o_ref[0] = h


def _combine(ys, slot, h, g2, g, tm, final, transposed):
    bsz, t, d = h.shape
    nj = t // tm
    nt = bsz * nj
    per_batch = g2.shape[0] == bsz
    slot3 = slot.reshape(nt, 1, tm)
    smem = functools.partial(pl.BlockSpec, (1, 1, tm), memory_space=pltpu.SMEM)
    return pl.pallas_call(
        functools.partial(_combine_kernel, final=final, transposed=transposed),
        grid=(bsz, nj),
        in_specs=[smem(lambda bi, j: (bi * nj + j, 0, 0)),
                  smem(lambda bi, j: (jnp.minimum(bi * nj + j + 1, nt - 1), 0, 0)),
                  pl.BlockSpec(memory_space=pl.ANY),
                  _tile_spec(t, tm, d, transposed, to_natural=True),
                  pl.BlockSpec((1, 1, d), (lambda bi, j: (bi, 0, 0)) if per_batch else (lambda bi, j: (0, 0, 0))),
                  pl.BlockSpec((1, d), lambda bi, j: (0, 0))],
        out_specs=pl.BlockSpec((1, tm, d), lambda bi, j: (bi, j, 0)),
        out_shape=jax.ShapeDtypeStruct((bsz, t, d), F32),
        scratch_shapes=[pltpu.VMEM((2, tm // SUBLANES, SUBLANES, d), F32), pltpu.SemaphoreType.DMA((2,))],
        compiler_params=_cparams(("arbitrary", "arbitrary")),
        name="moe_combine",
    )(slot3, slot3, ys, _tile_view(h, transposed, to_natural=True), g2, g.reshape(1, d))


def _moe_layer(vx, route, experts, h, g2, g, final, transposed):
    bsz, t, d = h.shape
    n = bsz * t
    tmo = MOE_TILE
    while tmo > MOE_TILE_MIN and n < 2 * N_CLASSES * tmo:
        tmo //= 2
    n_max = n // tmo + N_CLASSES
    vx = vx.reshape(n, d + ROUTER_LANES)
    slot, blk, e_lo, e_hi, n_real, fill = _routing_tables(route, tmo, n_max)
    xs = _dispatch(vx, slot, fill, n_max * tmo, tmo, min(DISPATCH_TILE, n))
    ys = _moe(xs, blk, e_lo, e_hi, n_real, experts, tmo)
    if transposed:
        slot = slot.reshape(bsz, GRID_W, t // GRID_W).transpose(0, 2, 1)
        tm = SUBLANES * GRID_W
    else:
        tm = min(DISPATCH_TILE, t)
    return _combine(ys, slot, h, g2, g, tm, final, transposed)


def _block_diag(w):
    nh, bw, _ = w.shape
    eye = jnp.eye(nh, dtype=w.dtype)
    return (eye[:, None, :, None] * w[:, :, None, :]).reshape(nh * bw, nh * bw)


def kernel(x, c, ctx, c_ctx, w_mod, b_mod, norm1_g, norm2_g, w_in, conv_w, conv_b, conv_ln_g, conv_ln_b,
           mlstm_b_i, mlstm_b_f, mlstm_norm_g, lru_conv_w, lru_conv_b, lru_w_a, lru_b_a, lru_w_x, lru_b_x,
           lru_lambda, w_out, w_rg, b_rg, w_re, b_re, w_gate, w_up, w_down, final_g):
    bsz, t, d = x.shape
    tc = ctx.shape[1]
    depth = w_mod.shape[0]
    dc = conv_w.shape[2]
    dm = mlstm_norm_g.shape[1]
    dl = lru_lambda.shape[2]
    ngate = N_DIR * MLSTM_HEADS
    tm_c = min(TOKEN_TILE, tc)
    tb = min(SCAN_BLOCK, t)

    rp = -(-(bsz + 1) // SUBLANES) * SUBLANES
    cvec = jnp.zeros((rp, d), F32).at[:bsz].set(c).at[bsz].set(c_ctx)
    mod = _modulation(cvec, w_mod, b_mod)

    c_q = 2 * dc
    c_k = c_q + dm
    c_v = c_k + dm
    c_o = c_v + dm
    c_g = c_o + dm
    c_l = c_g + 2 * ngate
    splits = (2 * dc, dm, dm, 2 * dl)
    out_dtypes = (F32, BF16, F32, F32)
    splits_t = (dm, dm, 2 * ngate)
    out_dtypes_t = (BF16, BF16, F32)

    h_lat, h_ctx = x, ctx
    for l in range(depth):
        last = l == depth - 1
        transposed = l % 2 == 1
        tm = SUBLANES * (t // GRID_W) if transposed else min(TOKEN_TILE, t)
        tm_in = min(IN_TILE, t)
        m_lat = [mod[l, :bsz, k * d:(k + 1) * d].reshape(bsz, 1, d) for k in range(6)]
        m_ctx = [mod[l, bsz:bsz + 1, k * d:(k + 1) * d].reshape(1, 1, d) for k in range(6)]
        wl = w_in[l]
        w_tok = jnp.concatenate([wl[:, :c_q], wl[:, c_k:c_v], wl[:, c_o:c_g], wl[:, c_l:]], axis=1).astype(BF16)
        w_chn = jnp.concatenate([wl[:, c_q:c_k], wl[:, c_v:c_o], wl[:, c_g:c_l]], axis=1).T.astype(BF16)
        gate_bias = jnp.broadcast_to(
            jnp.concatenate([mlstm_b_i[l].reshape(-1), mlstm_b_f[l].reshape(-1)])[:, None], (2 * ngate, MLSTM_CHUNK))
        wax = jnp.stack([jnp.concatenate([_block_diag(lru_w_a[l, dd]), _block_diag(lru_w_x[l, dd])], axis=1)
                         for dd in range(N_DIR)]).astype(BF16)
        bax = jnp.concatenate([lru_b_a[l], lru_b_x[l]], axis=-1).reshape(N_DIR, 1, 2 * dl)
        lam = lru_lambda[l].reshape(N_DIR, 1, dl)
        lcb = lru_conv_b[l].reshape(N_DIR, 1, dl)
        wr = jnp.zeros((d, ROUTER_LANES), F32).at[:, :N_GROUPS].set(w_rg[l]).at[:, N_GROUPS:N_GROUPS + N_EXPERTS].set(w_re[l])
        wrh, wrl = _split_hi_lo(wr)
        br = jnp.zeros((1, ROUTER_LANES), F32).at[0, :N_GROUPS].set(b_rg[l]).at[0, N_GROUPS:N_GROUPS + N_EXPERTS].set(b_re[l])
        experts = (w_gate[l].astype(BF16), w_up[l].astype(BF16), w_down[l].astype(BF16))
        wo = w_out[l].astype(BF16)

        proj = functools.partial(_in_proj, norm_g=norm1_g[l], w=w_tok, wt=w_chn, gate_bias=gate_bias, splits=splits,
                                 out_dtypes=out_dtypes, splits_t=splits_t, out_dtypes_t=out_dtypes_t)
        cv_l, k_l, o_l, lru_l, qt_l, vt_l, gt_l = proj(h_lat, m_lat[0], m_lat[1], tm=tm_in, transposed=transposed)
        cv_c, k_c, o_c, lru_c, qt_c, vt_c, gt_c = proj(h_ctx, m_ctx[0], m_ctx[1], tm=tm_c, transposed=False)

        a_l = _conformer_conv(cv_l, conv_w[l], conv_b[l], conv_ln_g[l], conv_ln_b[l], tm_in)
        mcf, mcb, mlf, mlb = _mlstm((qt_c, k_c, vt_c, gt_c), (qt_l, k_l, vt_l, gt_l), tb)
        rcf, rcb, rlf, rlb = _rglru(lru_c, lru_l, lru_conv_w[l], lcb, wax, bax, lam, tb)

        h_lat, vx_l, cnt_l = _out_proj(a_l, mlf, mlb, o_l, rlf, rlb, lru_l, h_lat, m_lat[2], m_lat[3], m_lat[4],
                                       norm2_g[l], mlstm_norm_g[l], wo, wrh, wrl, br, tm, transposed)
        h_lat = _moe_layer(vx_l, cnt_l, experts, h_lat, m_lat[5], final_g, last, transposed)
        if not last:
            a_c = _conformer_conv(cv_c, conv_w[l], conv_b[l], conv_ln_g[l], conv_ln_b[l], tm_c)
            h_ctx, vx_c, cnt_c = _out_proj(a_c, mcf, mcb, o_c, rcf, rcb, lru_c, h_ctx, m_ctx[2], m_ctx[3], m_ctx[4],
                                           norm2_g[l], mlstm_norm_g[l], wo, wrh, wrl, br, tm_c, False)
            h_ctx = _moe_layer(vx_c, cnt_c, experts, h_ctx, m_ctx[5], final_g, False, False)
    return h_lat
```

```python
import functools
import math

import jax
import jax.numpy as jnp
from jax import lax
from jax.experimental import pallas as pl
from jax.experimental.pallas import tpu as pltpu

EPS = 1e-6
GRID_W = 64
D_CONV_FRAC = 4
CONV_WIDTH = 31
CONV_HALO = 16
MLSTM_HEADS = 4
MLSTM_CHUNK = 128
LRU_HEADS = 4
LRU_CONV_WIDTH = 4
LRU_C = 8.0
N_DIR = 2
N_GROUPS = 4
EXPERTS_PER_GROUP = 4
N_EXPERTS = N_GROUPS * EXPERTS_PER_GROUP
N_PAIRS = EXPERTS_PER_GROUP * (EXPERTS_PER_GROUP - 1) // 2
N_CLASSES = N_GROUPS * N_PAIRS
ROUTER_LANES = 128
INFO_CLASS, INFO_RANK, INFO_W_LO, INFO_W_HI = 0, 1, 2, 3
SUBLANES = 8
LOG2E = 1.4426950408889634

VMEM_LIMIT = 56 * 1024 * 1024
IN_TILE = 1024
TOKEN_TILE = 512
SCAN_BLOCK = 512
MOE_TILE = 512
MOE_TILE_MIN = 128
DISPATCH_TILE = 512
DMA_UNROLL = 8

F32 = jnp.float32
BF16 = jnp.bfloat16


def _cparams(sem):
    return pltpu.CompilerParams(dimension_semantics=sem, vmem_limit_bytes=VMEM_LIMIT)


def _sigmoid(x):
    return jax.nn.sigmoid(x)


def _log_sigmoid(x):
    return jnp.minimum(x, 0.0) - jnp.log1p(jnp.exp(-jnp.abs(x)))


def _softplus(x):
    return jnp.maximum(x, 0.0) + jnp.log1p(jnp.exp(-jnp.abs(x)))


def _gelu_tanh(x):
    return x * (0.5 * (1.0 + jnp.tanh(0.7978845608028654 * (x + 0.044715 * (x * x * x)))))


def _dot(a, b):
    return jnp.dot(a, b, preferred_element_type=F32)


def _split_hi_lo(x):
    hi = x.astype(BF16)
    lo = (x - hi.astype(F32)).astype(BF16)
    return hi, lo


def _load_tile(ref, transposed):
    if not transposed:
        return ref[0]
    return jnp.concatenate([ref[0, :, w, :] for w in range(ref.shape[2])], axis=0)


def _tile_spec(t, tm, d, transposed, to_natural=False):
    if not transposed:
        return pl.BlockSpec((1, tm, d), lambda b, j: (b, j, 0))
    major = GRID_W if to_natural else t // GRID_W
    assert tm % (SUBLANES * major) == 0
    return pl.BlockSpec((1, major, tm // major, d), lambda b, j: (b, 0, j, 0))


def _tile_view(a, transposed, to_natural=False):
    if not transposed:
        return a
    b, t, d = a.shape
    return a.reshape(b, GRID_W, t // GRID_W, d) if to_natural else a.reshape(b, t // GRID_W, GRID_W, d)


def _mod_kernel(c_ref, w_ref, b_ref, o_ref):
    c = c_ref[...]
    s = (c * _sigmoid(c)).astype(BF16)
    o_ref[0] = _dot(s, w_ref[0].astype(BF16)) + b_ref[0]


def _modulation(cvec, w_mod, b_mod):
    nl, d, d6 = w_mod.shape
    rp = cvec.shape[0]
    tn = d6 // 4
    return pl.pallas_call(
        _mod_kernel,
        grid=(nl, d6 // tn),
        in_specs=[pl.BlockSpec((rp, d), lambda l, j: (0, 0)),
                  pl.BlockSpec((1, d, tn), lambda l, j: (l, 0, j)),
                  pl.BlockSpec((1, 1, tn), lambda l, j: (l, 0, j))],
        out_specs=pl.BlockSpec((1, rp, tn), lambda l, j: (l, 0, j)),
        out_shape=jax.ShapeDtypeStruct((nl, rp, d6), F32),
        compiler_params=_cparams(("arbitrary", "arbitrary")),
        name="modulation",
    )(cvec, w_mod, b_mod.reshape(nl, 1, d6))


def _chunk_scan(x, op, reverse):
    L = MLSTM_CHUNK
    n = x.shape[1]
    pos = lax.broadcasted_iota(jnp.int32, x.shape, 1) & (L - 1)
    sh = 1
    while sh < L:
        if reverse:
            xs, ok = pltpu.roll(x, n - sh, 1), pos < L - sh
        else:
            xs, ok = pltpu.roll(x, sh, 1), pos >= sh
        x = jnp.where(ok, op(x, xs), x)
        sh *= 2
    return x


def _gate_scans(g):
    nrow = g.shape[0] // 2
    fwd_row = lax.broadcasted_iota(jnp.int32, (nrow, g.shape[1]), 0) < nrow // N_DIR
    ig2 = g[:nrow] * LOG2E
    lf2 = _log_sigmoid(g[nrow:]) * LOG2E
    b2 = jnp.where(fwd_row, _chunk_scan(lf2, jnp.add, False), _chunk_scan(lf2, jnp.add, True))
    c2 = ig2 - b2
    cm2 = jnp.where(fwd_row, _chunk_scan(c2, jnp.maximum, False), _chunk_scan(c2, jnp.maximum, True))
    return jnp.concatenate([ig2, b2, cm2], axis=0)


def _in_kernel(h_ref, shift_ref, scale_ref, g_ref, w_ref, wt_ref, gbias_ref, *outs, transposed, splits, splits_t):
    h = _load_tile(h_ref, transposed)
    ms = jnp.mean(h * h, axis=-1, keepdims=True)
    u = (h * lax.rsqrt(ms + EPS)) * g_ref[...]
    u = (u * (1.0 + scale_ref[0]) + shift_ref[0]).astype(BF16)
    c0 = 0
    for o_ref, width in zip(outs, splits):
        o_ref[0] = _dot(u, w_ref[:, c0:c0 + width]).astype(o_ref.dtype)
        c0 += width
    r0 = 0
    outs_t = outs[len(splits):]
    for o_ref, width in zip(outs_t[:-1], splits_t[:-1]):
        o_ref[0] = lax.dot_general(wt_ref[r0:r0 + width, :], u, (((1,), (1,)), ((), ())),
                                   preferred_element_type=F32).astype(o_ref.dtype)
        r0 += width
    gates = lax.dot_general(wt_ref[r0:r0 + splits_t[-1], :], u, (((1,), (1,)), ((), ())), preferred_element_type=F32)
    tm = gates.shape[1]
    outs_t[-1][0] = _gate_scans(gates + jnp.concatenate([gbias_ref[...]] * (tm // MLSTM_CHUNK), axis=1))


def _in_proj(h, shift, scale, norm_g, w, wt, gate_bias, splits, out_dtypes, splits_t, out_dtypes_t, tm, transposed):
    b, t, d = h.shape
    per_batch = shift.shape[0] == b
    mod_spec = pl.BlockSpec((1, 1, d), (lambda bi, j: (bi, 0, 0)) if per_batch else (lambda bi, j: (0, 0, 0)))
    rows_t = splits_t[:-1] + (splits_t[-1] // 2 * 3,)
    return pl.pallas_call(
        functools.partial(_in_kernel, transposed=transposed, splits=splits, splits_t=splits_t),
        grid=(b, t // tm),
        in_specs=[_tile_spec(t, tm, d, transposed), mod_spec, mod_spec, pl.BlockSpec((1, d), lambda bi, j: (0, 0)),
                  pl.BlockSpec(w.shape, lambda bi, j: (0, 0)), pl.BlockSpec(wt.shape, lambda bi, j: (0, 0)),
                  pl.BlockSpec(gate_bias.shape, lambda bi, j: (0, 0))],
        out_specs=([pl.BlockSpec((1, tm, width), lambda bi, j: (bi, j, 0)) for width in splits]
                   + [pl.BlockSpec((1, width, tm), lambda bi, j: (bi, 0, j)) for width in rows_t]),
        out_shape=([jax.ShapeDtypeStruct((b, t, width), dt) for width, dt in zip(splits, out_dtypes)]
                   + [jax.ShapeDtypeStruct((b, width, t), dt) for width, dt in zip(rows_t, out_dtypes_t)]),
        compiler_params=_cparams(("arbitrary", "arbitrary")),
        name="in_proj",
    )(_tile_view(h, transposed), shift, scale, norm_g.reshape(1, d), w, wt, gate_bias)


def _conv_kernel(prev_ref, cur_ref, next_ref, w_ref, b_ref, lng_ref, lnb_ref, o_ref, *, tb, nblk, dc):
    j = pl.program_id(1)

    def glu(x):
        return x[:, :dc] * _sigmoid(x[:, dc:])

    up = jnp.where(j > 0, glu(prev_ref[0]), 0.0)
    un = jnp.where(j < nblk - 1, glu(next_ref[0]), 0.0)
    ext = jnp.concatenate([up, glu(cur_ref[0]), un], axis=0)
    base = CONV_HALO - CONV_WIDTH // 2
    nrows = tb + 2 * CONV_HALO
    acc = jnp.zeros((tb, dc), F32)
    for b in range(SUBLANES):
        shifted = ext if b == 0 else pltpu.roll(ext, nrows - b, 0)
        for a in range((base + CONV_WIDTH - 1 - b) // SUBLANES + 1):
            k = SUBLANES * a + b - base
            if 0 <= k < CONV_WIDTH:
                acc = acc + w_ref[k:k + 1, :] * shifted[SUBLANES * a:SUBLANES * a + tb, :]
    acc = acc + b_ref[...]
    mu = jnp.mean(acc, axis=-1, keepdims=True)
    xc = acc - mu
    y = xc * lax.rsqrt(jnp.mean(xc * xc, axis=-1, keepdims=True) + EPS) * lng_ref[...] + lnb_ref[...]
    o_ref[0] = (y * _sigmoid(y)).astype(o_ref.dtype)


def _conformer_conv(cv, w, b, ln_g, ln_b, tb):
    bsz, t, c2 = cv.shape
    dc = c2 // 2
    nblk = t // tb
    hb = tb // CONV_HALO
    nh = t // CONV_HALO
    wp = jnp.zeros((CONV_WIDTH + 1, dc), F32).at[:CONV_WIDTH].set(w)
    vec = pl.BlockSpec((1, dc), lambda bi, j: (0, 0))
    return pl.pallas_call(
        functools.partial(_conv_kernel, tb=tb, nblk=nblk, dc=dc),
        grid=(bsz, nblk),
        in_specs=[pl.BlockSpec((1, CONV_HALO, c2), lambda bi, j: (bi, jnp.maximum(j * hb - 1, 0), 0)),
                  pl.BlockSpec((1, tb, c2), lambda bi, j: (bi, j, 0)),
                  pl.BlockSpec((1, CONV_HALO, c2), lambda bi, j: (bi, jnp.minimum((j + 1) * hb, nh - 1), 0)),
                  pl.BlockSpec((CONV_WIDTH + 1, dc), lambda bi, j: (0, 0)), vec, vec, vec],
        out_specs=pl.BlockSpec((1, tb, dc), lambda bi, j: (bi, j, 0)),
        out_shape=jax.ShapeDtypeStruct((bsz, t, dc), BF16),
        compiler_params=_cparams(("arbitrary", "arbitrary")),
        name="conformer_conv",
    )(cv, cv, cv, wp, b.reshape(1, dc), ln_g.reshape(1, dc), ln_b.reshape(1, dc))


def _mlstm_block(qt_ref, k_ref, vt_ref, gt_ref, out_ref, c_ref, n_ref, m_ref, d, nchunks, hd):
    L = MLSTM_CHUNK
    nh = MLSTM_HEADS
    assert hd == L
    nrow = N_DIR * nh
    log2_scale = math.log2(hd ** -0.5)
    ig2, b2, cm2 = gt_ref[0, 0:nrow], gt_ref[0, nrow:2 * nrow], gt_ref[0, 2 * nrow:3 * nrow]
    c2 = ig2 - b2
    row_s = lax.broadcasted_iota(jnp.int32, (L, L), 0)
    col_t = lax.broadcasted_iota(jnp.int32, (L, L), 1)
    mask = (row_s <= col_t) if d == 0 else (row_s >= col_t)
    ones16 = jnp.ones((2 * SUBLANES, L), BF16)
    for ci in (range(nchunks) if d == 0 else range(nchunks - 1, -1, -1)):
        sl = slice(ci * L, (ci + 1) * L)
        b2c, c2c, ig2c = b2[:, sl], c2[:, sl], ig2[:, sl]
        m2 = m_ref[d]
        a2 = -jnp.maximum(m2, cm2[:, sl])
        inter = jnp.exp2(m2 + a2)
        edn = jnp.exp2(a2 - b2c)
        b_last = jnp.broadcast_to(b2c[:, L - 1:L] if d == 0 else b2c[:, 0:1], (nrow, L))
        logw2 = b_last - b2c + ig2c
        m2_new = jnp.maximum(b_last + m2, jnp.broadcast_to(jnp.max(logw2, axis=-1, keepdims=True), (nrow, L)))
        wgt = jnp.exp2(logw2 - m2_new + log2_scale)
        decay = jnp.exp2(b_last + m2 - m2_new)
        m_ref[d] = m2_new
        c_cols = (c2c + log2_scale).T
        for h in range(nh):
            r = d * nh + h
            e = c_cols[:, r:r + 1] + a2[r:r + 1]
            p = jnp.where(mask, jnp.exp2(e), 0.0)
            qt = qt_ref[0, h * hd:(h + 1) * hd, sl]
            k = k_ref[0, ci * L:(ci + 1) * L, h * hd:(h + 1) * hd]
            vt = vt_ref[0, h * hd:(h + 1) * hd, sl]
            ct = c_ref[r]
            n16 = n_ref[r]
            st = _dot(k, qt) * p
            nq = _dot(n16.astype(BF16), qt)
            nd = _dot(vt, st.astype(BF16))
            inter_r = inter[r:r + 1]
            num = nd + inter_r * _dot(ct.astype(BF16), qt)
            den = jnp.sum(st, axis=0, keepdims=True) + inter_r * nq[0:1]
            ht = num * (1.0 / jnp.maximum(jnp.abs(den), edn[r:r + 1]))
            out_ref[0, ci * L:(ci + 1) * L, h * hd:(h + 1) * hd] = ht.T
            w_r = wgt[r:r + 1]
            vw = (vt.astype(F32) * w_r).astype(BF16)
            upd = _dot(jnp.concatenate([vw, ones16 * w_r.astype(BF16)], axis=0), k)
            dec = decay[r:r + 1]
            c_ref[r] = dec * ct + upd[:hd]
            n_ref[r] = dec * n16 + upd[hd:]


def _mlstm_kernel(qtc_ref, kc_ref, vtc_ref, gtc_ref, qtf_ref, kf_ref, vtf_ref, gtf_ref,
                  qtb_ref, kb_ref, vtb_ref, gtb_ref,
                  hcf_ref, hcb_ref, hf_ref, hb_ref, c_ref, n_ref, m_ref, *, nc_ctx, nc_lat, hd):
    s = pl.program_id(1)
    state = (c_ref, n_ref, m_ref)

    @pl.when(s == 0)
    def _():
        c_ref[...] = jnp.zeros_like(c_ref)
        n_ref[...] = jnp.zeros_like(n_ref)
        m_ref[...] = jnp.zeros_like(m_ref)
        _mlstm_block(qtc_ref, kc_ref, vtc_ref, gtc_ref, hcf_ref, *state, 0, nc_ctx, hd)
        _mlstm_block(qtc_ref, kc_ref, vtc_ref, gtc_ref, hcb_ref, *state, 1, nc_ctx, hd)

    @pl.when(s > 0)
    def _():
        _mlstm_block(qtf_ref, kf_ref, vtf_ref, gtf_ref, hf_ref, *state, 0, nc_lat, hd)
        _mlstm_block(qtb_ref, kb_ref, vtb_ref, gtb_ref, hb_ref, *state, 1, nc_lat, hd)


def _mlstm(ctx_in, lat_in, tb):
    qt_c, k_c, vt_c, gt_c = ctx_in
    qt_l, k_l, vt_l, gt_l = lat_in
    bsz, tc, dm = k_c.shape
    t = k_l.shape[1]
    ng = gt_l.shape[1]
    hd = dm // MLSTM_HEADS
    nb = t // tb
    nstate = N_DIR * MLSTM_HEADS

    def fwd(s):
        return jnp.maximum(s - 1, 0)

    def bwd(s):
        return nb - 1 - jnp.maximum(s - 1, 0)

    def specs(tlen, blk):
        return [pl.BlockSpec((1, dm, tlen), lambda bi, s: (bi, 0, blk(s))),
                pl.BlockSpec((1, tlen, dm), lambda bi, s: (bi, blk(s), 0)),
                pl.BlockSpec((1, dm, tlen), lambda bi, s: (bi, 0, blk(s))),
                pl.BlockSpec((1, ng, tlen), lambda bi, s: (bi, 0, blk(s)))]

    def out(tlen, blk):
        return pl.BlockSpec((1, tlen, dm), lambda bi, s: (bi, blk(s), 0))

    first = lambda s: 0
    return pl.pallas_call(
        functools.partial(_mlstm_kernel, nc_ctx=tc // MLSTM_CHUNK, nc_lat=tb // MLSTM_CHUNK, hd=hd),
        grid=(bsz, nb + 1),
        in_specs=specs(tc, first) + specs(tb, fwd) + specs(tb, bwd),
        out_specs=[out(tc, first), out(tc, first), out(tb, fwd), out(tb, bwd)],
        out_shape=[jax.ShapeDtypeStruct((bsz, tc, dm), F32), jax.ShapeDtypeStruct((bsz, tc, dm), F32),
                   jax.ShapeDtypeStruct((bsz, t, dm), F32), jax.ShapeDtypeStruct((bsz, t, dm), F32)],
        scratch_shapes=[pltpu.VMEM((nstate, hd, hd), F32), pltpu.VMEM((nstate, 2 * SUBLANES, hd), F32),
                        pltpu.VMEM((N_DIR, nstate, MLSTM_CHUNK), F32)],
        compiler_params=_cparams(("arbitrary", "arbitrary")),
        name="mlstm",
    )(*ctx_in, *lat_in, *lat_in)


def _lru_block(x, halo, d, cw, cb, wax, bax, lam, h0):
    tb, dl = x.shape
    kw = LRU_CONV_WIDTH
    if d == 0:
        ext = jnp.concatenate([halo, x], axis=0)
        taps = [ext[SUBLANES - (kw - 1) + j:SUBLANES - (kw - 1) + j + tb] for j in range(kw)]
    else:
        ext = jnp.concatenate([x, halo], axis=0)
        taps = [ext[kw - 1 - j:kw - 1 - j + tb] for j in range(kw)]
    xc = cb
    for j in range(kw):
        xc = xc + cw[j:j + 1, :] * taps[j]
    ri = _sigmoid(_dot(xc.astype(BF16), wax) + bax)
    r, i = ri[:, :dl], ri[:, dl:]
    log_a = (-LRU_C * r) * _softplus(-lam)
    a = jnp.exp(log_a)
    th = jnp.tanh(log_a)
    u = jnp.sqrt(-2.0 * th / (1.0 - th)) * (i * xc)
    ngrp = tb // SUBLANES
    a = a.reshape(ngrp, SUBLANES, dl)
    u = u.reshape(ngrp, SUBLANES, dl)
    sub = lax.broadcasted_iota(jnp.int32, a.shape, 1)
    for sh in (1, 2, 4):
        if d == 0:
            a_s, u_s, msk = pltpu.roll(a, sh, 1), pltpu.roll(u, sh, 1), sub >= sh
        else:
            a_s, u_s, msk = pltpu.roll(a, SUBLANES - sh, 1), pltpu.roll(u, SUBLANES - sh, 1), sub < SUBLANES - sh
        u = jnp.where(msk, a * u_s + u, u)
        a = jnp.where(msk, a * a_s, a)
    a = a.reshape(tb, dl)
    u = u.reshape(tb, dl)
    outs = [None] * ngrp
    carry = h0
    for j in (range(ngrp) if d == 0 else range(ngrp - 1, -1, -1)):
        hj = u[j * SUBLANES:(j + 1) * SUBLANES] + a[j * SUBLANES:(j + 1) * SUBLANES] * carry
        outs[j] = hj
        carry = hj[SUBLANES - 1:SUBLANES] if d == 0 else hj[0:1]
    return jnp.concatenate(outs, axis=0), carry


def _lru_kernel(xc_ref, xf_ref, xb_ref, cw_ref, cb_ref, wax_ref, bax_ref, lam_ref,
                hcf_ref, hcb_ref, hf_ref, hb_ref, hcar_ref, halo_ref):
    s = pl.program_id(1)

    def params(d):
        return cw_ref[d], cb_ref[d], wax_ref[d], bax_ref[d], lam_ref[d]

    @pl.when(s == 0)
    def _():
        x = xc_ref[0]
        zero_halo = jnp.zeros((SUBLANES, x.shape[1]), F32)
        zero_h = jnp.zeros((1, x.shape[1]), F32)
        for d, o_ref in ((0, hcf_ref), (1, hcb_ref)):
            h, carry = _lru_block(x, zero_halo, d, *params(d), zero_h)
            o_ref[0] = h
            hcar_ref[d] = jnp.broadcast_to(carry, hcar_ref.shape[1:])

    @pl.when(s > 0)
    def _():
        for d, x_ref, o_ref in ((0, xf_ref, hf_ref), (1, xb_ref, hb_ref)):
            x = x_ref[0]
            halo = jnp.where(s > 1, halo_ref[d], 0.0)
            h, carry = _lru_block(x, halo, d, *params(d), hcar_ref[d][0:1])
            o_ref[0] = h
            hcar_ref[d] = jnp.broadcast_to(carry, hcar_ref.shape[1:])
            halo_ref[d] = x[x.shape[0] - SUBLANES:] if d == 0 else x[:SUBLANES]


def _rglru(lru_c, lru_l, cw, cb, wax, bax, lam, tb):
    bsz, tc, w2 = lru_c.shape
    t = lru_l.shape[1]
    dl = w2 // 2
    nb = t // tb

    def fwd(bi, s):
        return (bi, jnp.maximum(s - 1, 0), 0)

    def bwd(bi, s):
        return (bi, nb - 1 - jnp.maximum(s - 1, 0), 0)

    def ctx(bi, s):
        return (bi, 0, 0)

    def whole(a):
        return pl.BlockSpec(a.shape, lambda bi, s: (0,) * a.ndim)

    return pl.pallas_call(
        _lru_kernel,
        grid=(bsz, nb + 1),
        in_specs=[pl.BlockSpec((1, tc, dl), ctx), pl.BlockSpec((1, tb, dl), fwd), pl.BlockSpec((1, tb, dl), bwd),
                  whole(cw), whole(cb), whole(wax), whole(bax), whole(lam)],
        out_specs=[pl.BlockSpec((1, tc, dl), ctx), pl.BlockSpec((1, tc, dl), ctx),
                   pl.BlockSpec((1, tb, dl), fwd), pl.BlockSpec((1, tb, dl), bwd)],
        out_shape=[jax.ShapeDtypeStruct((bsz, tc, dl), F32), jax.ShapeDtypeStruct((bsz, tc, dl), F32),
                   jax.ShapeDtypeStruct((bsz, t, dl), F32), jax.ShapeDtypeStruct((bsz, t, dl), F32)],
        scratch_shapes=[pltpu.VMEM((N_DIR, SUBLANES, dl), F32), pltpu.VMEM((N_DIR, SUBLANES, dl), F32)],
        compiler_params=_cparams(("arbitrary", "arbitrary")),
        name="rglru",
    )(lru_c, lru_l, lru_l, cw, cb, wax, bax, lam)


def _route(logits):
    ng, ne = N_GROUPS, EXPERTS_PER_GROUP
    lane = lax.broadcasted_iota(jnp.int32, logits.shape, 1).astype(F32)
    big = float(ROUTER_LANES)
    is_g = lane < ng
    gl = jnp.where(is_g, logits, -jnp.inf)
    gmax = jnp.max(gl, axis=-1, keepdims=True)
    g_sel = jnp.min(jnp.where(gl == gmax, lane, big), axis=-1, keepdims=True)
    p_g = 1.0 / jnp.sum(jnp.where(is_g, jnp.exp(logits - gmax), 0.0), axis=-1, keepdims=True)
    lo = ng + ne * g_sel
    el = jnp.where((lane >= lo) & (lane < lo + ne), logits, -jnp.inf)
    v1 = jnp.max(el, axis=-1, keepdims=True)
    i1 = jnp.min(jnp.where(el == v1, lane, big), axis=-1, keepdims=True)
    el2 = jnp.where(lane == i1, -jnp.inf, el)
    v2 = jnp.max(el2, axis=-1, keepdims=True)
    i2 = jnp.min(jnp.where(el2 == v2, lane, big), axis=-1, keepdims=True)
    e2 = jnp.exp(v2 - v1)
    w1 = p_g / (1.0 + e2)
    w2 = p_g * e2 / (1.0 + e2)
    first_lower = i1 < i2
    e_lo = jnp.minimum(i1, i2) - lo
    e_hi = jnp.maximum(i1, i2) - lo
    pair = e_lo * (2 * ne - 1 - e_lo) * 0.5 + (e_hi - e_lo - 1.0)
    return g_sel * N_PAIRS + pair, jnp.where(first_lower, w1, w2), jnp.where(first_lower, w2, w1)


def _out_kernel(a_ref, mf_ref, mb_ref, o_ref, lf_ref, lb_ref, lg_ref, h_ref, g1_ref, sh2_ref, sc2_ref,
                n2g_ref, mng_ref, wout_ref, wrh_ref, wrl_ref, br_ref,
                hnew_ref, vx_ref, cnt_ref, info_ref, run_ref, *, transposed, d, hd):
    @pl.when((pl.program_id(0) == 0) & (pl.program_id(1) == 0))
    def _():
        run_ref[...] = jnp.zeros_like(run_ref)

    dc = a_ref.shape[2]
    dm = mf_ref.shape[2]
    mh = mf_ref[0] + mb_ref[0]
    parts = []
    for h in range(dm // hd):
        x = mh[:, h * hd:(h + 1) * hd]
        xc = x - jnp.mean(x, axis=-1, keepdims=True)
        parts.append(xc * lax.rsqrt(jnp.mean(xc * xc, axis=-1, keepdims=True) + EPS) * mng_ref[:, h * hd:(h + 1) * hd])
    m_out = (_sigmoid(o_ref[0]) * jnp.concatenate(parts, axis=-1)).astype(BF16)
    r_out = ((lf_ref[0] + lb_ref[0]) * _gelu_tanh(lg_ref[0])).astype(BF16)
    y = (_dot(a_ref[0], wout_ref[0:dc, :]) + _dot(m_out, wout_ref[dc:dc + dm, :])
         + _dot(r_out, wout_ref[dc + dm:, :]))
    hn = _load_tile(h_ref, transposed) + g1_ref[0] * y
    hnew_ref[0] = hn
    ms = jnp.mean(hn * hn, axis=-1, keepdims=True)
    v = (hn * lax.rsqrt(ms + EPS)) * n2g_ref[...]
    v = v * (1.0 + sc2_ref[0]) + sh2_ref[0]
    vh, vl = _split_hi_lo(v)
    logits = _dot(vh, wrh_ref[...]) + _dot(vl, wrh_ref[...]) + _dot(vh, wrl_ref[...]) + br_ref[...]
    cls, w_lo, w_hi = _route(logits)
    tm = v.shape[0]
    lane = lax.broadcasted_iota(jnp.int32, (tm, ROUTER_LANES), 1).astype(F32)
    onehot = jnp.where(lane == cls, 1.0, 0.0)
    row = lax.broadcasted_iota(jnp.int32, (tm, tm), 0)
    col = lax.broadcasted_iota(jnp.int32, (tm, tm), 1)
    before = _dot(jnp.where(col < row, 1.0, 0.0).astype(BF16), onehot.astype(BF16)) + run_ref[0:1, :]
    rank = jnp.sum(onehot * before, axis=-1, keepdims=True)
    run_ref[...] = run_ref[...] + jnp.sum(onehot, axis=0, keepdims=True)
    cnt_ref[...] = run_ref[...]
    info = (jnp.where(lane == INFO_CLASS, cls, 0.0) + jnp.where(lane == INFO_RANK, rank, 0.0)
            + jnp.where(lane == INFO_W_LO, w_lo, 0.0) + jnp.where(lane == INFO_W_HI, w_hi, 0.0))
    vx_ref[0, :, :d] = v
    vx_ref[0, :, d:] = info
    info_ref[0] = info.T[:SUBLANES]


def _out_proj(a, mf, mb, o, lf, lb, lru, h, g1, sh2, sc2, n2g, mng, w_out, wrh, wrl, br, tm, transposed):
    bsz, t, d = h.shape
    dc, dm, dl = a.shape[2], mf.shape[2], lf.shape[2]
    per_batch = g1.shape[0] == bsz
    mod_spec = pl.BlockSpec((1, 1, d), (lambda bi, j: (bi, 0, 0)) if per_batch else (lambda bi, j: (0, 0, 0)))

    def scan(width, blk=0):
        return pl.BlockSpec((1, tm, width), lambda bi, j: (bi, j, blk))

    def whole(x):
        return pl.BlockSpec(x.shape, lambda bi, j: (0,) * x.ndim)

    dx = d + ROUTER_LANES
    n2g, mng = n2g.reshape(1, d), mng.reshape(1, dm)
    hv = _tile_view(h, transposed)
    hnew, vx, cnt, info_t = pl.pallas_call(
        functools.partial(_out_kernel, transposed=transposed, d=d, hd=dm // MLSTM_HEADS),
        grid=(bsz, t // tm),
        in_specs=[scan(dc), scan(dm), scan(dm), scan(dm), scan(dl), scan(dl), scan(dl, 1),
                  _tile_spec(t, tm, d, transposed),
                  mod_spec, mod_spec, mod_spec, whole(n2g), whole(mng), whole(w_out), whole(wrh), whole(wrl), whole(br)],
        out_specs=[scan(d), scan(dx), pl.BlockSpec((SUBLANES, ROUTER_LANES), lambda bi, j: (0, 0)),
                   pl.BlockSpec((1, SUBLANES, tm), lambda bi, j: (bi, 0, j))],
        out_shape=[jax.ShapeDtypeStruct((bsz, t, d), F32), jax.ShapeDtypeStruct((bsz, t, dx), F32),
                   jax.ShapeDtypeStruct((SUBLANES, ROUTER_LANES), F32), jax.ShapeDtypeStruct((bsz, SUBLANES, t), F32)],
        scratch_shapes=[pltpu.VMEM((SUBLANES, ROUTER_LANES), F32)],
        compiler_params=_cparams(("arbitrary", "arbitrary")),
        name="out_proj",
    )(a, mf, mb, o, lf, lb, lru, hv, g1, sh2, sc2, n2g, mng, w_out, wrh, wrl, br)
    return hnew, vx, (cnt, info_t)


def _routing_tables(route, tmo, n_max):
    cnt, info_t = route
    cls = info_t[:, INFO_CLASS, :].reshape(-1).astype(jnp.int32)
    rank = info_t[:, INFO_RANK, :].reshape(-1).astype(jnp.int32)
    counts = cnt[0, :N_CLASSES].astype(jnp.int32)
    tiles = (counts + tmo - 1) // tmo
    tile_end = jnp.cumsum(tiles)
    tile_start = tile_end - tiles
    n_real = tile_end[N_CLASSES - 1]
    slot = jnp.take(tile_start * tmo, cls) + rank
    blk = jnp.minimum(jnp.arange(n_max, dtype=jnp.int32), n_real - 1)
    tile_cls = jnp.minimum(jnp.sum((tile_end[None, :] <= blk[:, None]).astype(jnp.int32), axis=1), N_CLASSES - 1)
    pair_lo = jnp.array([a for a in range(EXPERTS_PER_GROUP) for b in range(a + 1, EXPERTS_PER_GROUP)], jnp.int32)
    pair_hi = jnp.array([b for a in range(EXPERTS_PER_GROUP) for b in range(a + 1, EXPERTS_PER_GROUP)], jnp.int32)
    grp = tile_cls // N_PAIRS
    e_lo = grp * EXPERTS_PER_GROUP + jnp.take(pair_lo, tile_cls % N_PAIRS)
    e_hi = grp * EXPERTS_PER_GROUP + jnp.take(pair_hi, tile_cls % N_PAIRS)
    tail = n_real + jnp.arange(N_CLASSES, dtype=jnp.int32)
    fill = jnp.concatenate([jnp.where(tiles > 0, tile_end - 1, -1), jnp.where(tail < n_max, tail, -1)])
    return slot, blk, e_lo, e_hi, n_real.reshape(1), fill.astype(jnp.int32)


def _dispatch_kernel(fill_ref, slot_ref, vx_ref, xs_ref, zero_ref, zsem, sem):
    tm = vx_ref.shape[0] * SUBLANES
    tmo = zero_ref.shape[0]

    @pl.when(pl.program_id(0) == 0)
    def _():
        zero_ref[...] = jnp.zeros_like(zero_ref)

        def fill(j):
            return pltpu.make_async_copy(zero_ref, xs_ref.at[pl.ds(pl.multiple_of(fill_ref[j] * tmo, tmo), tmo)], zsem)

        for j in range(fill_ref.shape[0]):
            pl.when(fill_ref[j] >= 0)(lambda j=j: fill(j).start())
        for j in range(fill_ref.shape[0]):
            pl.when(fill_ref[j] >= 0)(lambda j=j: fill(j).wait())

    def issue(g, carry):
        for p in range(SUBLANES):
            dst = xs_ref.at[pl.ds(slot_ref[0, 0, g * SUBLANES + p], 1)]
            pltpu.make_async_copy(vx_ref.at[g, pl.ds(p, 1)], dst, sem).start()
        return carry

    lax.fori_loop(0, tm // SUBLANES, issue, 0, unroll=DMA_UNROLL)
    rows = xs_ref.at[pl.ds(0, tm)]
    pltpu.make_async_copy(rows, rows, sem).wait()


def _dispatch(vx, slot, fill, n_rows, tmo, tm):
    n, dx = vx.shape
    grid_spec = pltpu.PrefetchScalarGridSpec(
        num_scalar_prefetch=1,
        grid=(n // tm,),
        in_specs=[pl.BlockSpec((1, 1, tm), lambda i, pad: (i, 0, 0), memory_space=pltpu.SMEM),
                  pl.BlockSpec((tm // SUBLANES, SUBLANES, dx), lambda i, pad: (i, 0, 0))],
        out_specs=pl.BlockSpec(memory_space=pl.ANY),
        scratch_shapes=[pltpu.VMEM((tmo, dx), F32), pltpu.SemaphoreType.DMA(()), pltpu.SemaphoreType.DMA(())])
    return pl.pallas_call(
        _dispatch_kernel,
        grid_spec=grid_spec,
        out_shape=jax.ShapeDtypeStruct((n_rows, dx), F32),
        compiler_params=_cparams(("arbitrary",)),
        name="moe_dispatch",
    )(fill, slot.reshape(n // tm, 1, tm), vx.reshape(n // SUBLANES, SUBLANES, dx))


def _expert(x, wg_ref, wu_ref, wd_ref):
    g = _dot(x, wg_ref[0, 0])
    return _dot((g * _sigmoid(g) * _dot(x, wu_ref[0, 0])).astype(BF16), wd_ref[0, 0])


def _moe_kernel(blk_ref, elo_ref, ehi_ref, nreal_ref, xs_ref, wg_lo_ref, wu_lo_ref, wd_lo_ref,
                wg_hi_ref, wu_hi_ref, wd_hi_ref, ys_ref):
    del blk_ref, elo_ref, ehi_ref
    d = ys_ref.shape[1]

    @pl.when(pl.program_id(0) < nreal_ref[0])
    def _():
        x = xs_ref[:, :d].astype(BF16)
        w_lo = xs_ref[:, d + INFO_W_LO:d + INFO_W_LO + 1]
        w_hi = xs_ref[:, d + INFO_W_HI:d + INFO_W_HI + 1]
        ys_ref[...] = (w_lo * _expert(x, wg_lo_ref, wu_lo_ref, wd_lo_ref)
                       + w_hi * _expert(x, wg_hi_ref, wu_hi_ref, wd_hi_ref))

    @pl.when(pl.program_id(0) >= nreal_ref[0])
    def _():
        ys_ref[...] = jnp.zeros_like(ys_ref)


def _moe(xs, blk, e_lo, e_hi, n_real, experts, layer, tm):
    wg, wu, wd = experts
    dx = xs.shape[1]
    n_tiles = blk.shape[0]
    n_rows = n_tiles * tm
    _, _, d, de = wg.shape
    up = pl.BlockSpec((1, 1, d, de), lambda i, blk, lo, hi, nr: (layer, lo[i], 0, 0)), \
        pl.BlockSpec((1, 1, d, de), lambda i, blk, lo, hi, nr: (layer, hi[i], 0, 0))
    down = pl.BlockSpec((1, 1, de, d), lambda i, blk, lo, hi, nr: (layer, lo[i], 0, 0)), \
        pl.BlockSpec((1, 1, de, d), lambda i, blk, lo, hi, nr: (layer, hi[i], 0, 0))
    grid_spec = pltpu.PrefetchScalarGridSpec(
        num_scalar_prefetch=4,
        grid=(n_tiles,),
        in_specs=[pl.BlockSpec((tm, dx), lambda i, blk, lo, hi, nr: (blk[i], 0)),
                  up[0], up[0], down[0], up[1], up[1], down[1]],
        out_specs=pl.BlockSpec((tm, d), lambda i, blk, lo, hi, nr: (i, 0)))
    return pl.pallas_call(
        _moe_kernel,
        grid_spec=grid_spec,
        out_shape=jax.ShapeDtypeStruct((n_rows, d), F32),
        compiler_params=_cparams(("arbitrary",)),
        name="moe",
    )(blk, e_lo, e_hi, n_real, xs, wg, wu, wd, wg, wu, wd)


def _combine_kernel(slot_ref, slot_next_ref, ys_ref, h_ref, g2_ref, g_ref, o_ref, buf_ref, sem, *, final, transposed):
    i = pl.program_id(0) * pl.num_programs(1) + pl.program_id(1)
    n = pl.num_programs(0) * pl.num_programs(1)
    tm = o_ref.shape[1]

    def gather(s_ref, b):
        def issue(g, carry):
            for p in range(SUBLANES):
                src = ys_ref.at[pl.ds(s_ref[0, 0, g * SUBLANES + p], 1)]
                pltpu.make_async_copy(src, buf_ref.at[b, g, pl.ds(p, 1)], sem.at[b]).start()
            return carry
        lax.fori_loop(0, tm // SUBLANES, issue, 0, unroll=DMA_UNROLL)

    cur = i % 2

    @pl.when(i == 0)
    def _():
        gather(slot_ref, 0)

    @pl.when(i + 1 < n)
    def _():
        gather(slot_next_ref, 1 - cur)

    rows = ys_ref.at[pl.ds(0, tm)]
    pltpu.make_async_copy(rows, rows, sem.at[cur]).wait()
    h = _load_tile(h_ref, transposed) + g2_ref[0] * buf_ref[cur].reshape(tm, -1)
    if final:
        ms = jnp.mean(h * h, axis=-1, keepdims=True)
        h = (h * lax.rsqrt(ms + EPS)) * g_ref[...]
    o_ref[0] = h


def _combine(ys, slot, h, g2, g, tm, final, transposed):
    bsz, t, d = h.shape
    nj = t // tm
    nt = bsz * nj
    per_batch = g2.shape[0] == bsz
    slot3 = slot.reshape(nt, 1, tm)
    smem = functools.partial(pl.BlockSpec, (1, 1, tm), memory_space=pltpu.SMEM)
    return pl.pallas_call(
        functools.partial(_combine_kernel, final=final, transposed=transposed),
        grid=(bsz, nj),
        in_specs=[smem(lambda bi, j: (bi * nj + j, 0, 0)),
                  smem(lambda bi, j: (jnp.minimum(bi * nj + j + 1, nt - 1), 0, 0)),
                  pl.BlockSpec(memory_space=pl.ANY),
                  _tile_spec(t, tm, d, transposed, to_natural=True),
                  pl.BlockSpec((1, 1, d), (lambda bi, j: (bi, 0, 0)) if per_batch else (lambda bi, j: (0, 0, 0))),
                  pl.BlockSpec((1, d), lambda bi, j: (0, 0))],
        out_specs=pl.BlockSpec((1, tm, d), lambda bi, j: (bi, j, 0)),
        out_shape=jax.ShapeDtypeStruct((bsz, t, d), F32),
        scratch_shapes=[pltpu.VMEM((2, tm // SUBLANES, SUBLANES, d), F32), pltpu.SemaphoreType.DMA((2,))],
        compiler_params=_cparams(("arbitrary", "arbitrary")),
        name="moe_combine",
    )(slot3, slot3, ys, _tile_view(h, transposed, to_natural=True), g2, g.reshape(1, d))


def _moe_layer(vx, route, experts, layer, h, g2, g, final, transposed):
    bsz, t, d = h.shape
    n = bsz * t
    tmo = MOE_TILE
    while tmo > MOE_TILE_MIN and n < 2 * N_CLASSES * tmo:
        tmo //= 2
    n_max = n // tmo + N_CLASSES
    vx = vx.reshape(n, d + ROUTER_LANES)
    slot, blk, e_lo, e_hi, n_real, fill = _routing_tables(route, tmo, n_max)
    xs = _dispatch(vx, slot, fill, n_max * tmo, tmo, min(DISPATCH_TILE, n))
    ys = _moe(xs, blk, e_lo, e_hi, n_real, experts, layer, tmo)
    if transposed:
        slot = slot.reshape(bsz, GRID_W, t // GRID_W).transpose(0, 2, 1)
        tm = SUBLANES * GRID_W
    else:
        tm = min(DISPATCH_TILE, t)
    return _combine(ys, slot, h, g2, g, tm, final, transposed)


def _block_diag(w):
    nh, bw, _ = w.shape
    eye = jnp.eye(nh, dtype=w.dtype)
    return (eye[:, None, :, None] * w[:, :, None, :]).reshape(nh * bw, nh * bw)


def kernel(x, c, ctx, c_ctx, w_mod, b_mod, norm1_g, norm2_g, w_in, conv_w, conv_b, conv_ln_g, conv_ln_b,
           mlstm_b_i, mlstm_b_f, mlstm_norm_g, lru_conv_w, lru_conv_b, lru_w_a, lru_b_a, lru_w_x, lru_b_x,
           lru_lambda, w_out, w_rg, b_rg, w_re, b_re, w_gate, w_up, w_down, final_g):
    bsz, t, d = x.shape
    tc = ctx.shape[1]
    depth = w_mod.shape[0]
    dc = conv_w.shape[2]
    dm = mlstm_norm_g.shape[1]
    dl = lru_lambda.shape[2]
    ngate = N_DIR * MLSTM_HEADS
    tm_c = min(TOKEN_TILE, tc)
    tb = min(SCAN_BLOCK, t)

    rp = -(-(bsz + 1) // SUBLANES) * SUBLANES
    cvec = jnp.zeros((rp, d), F32).at[:bsz].set(c).at[bsz].set(c_ctx)
    mod = _modulation(cvec, w_mod, b_mod)

    c_q = 2 * dc
    c_k = c_q + dm
    c_v = c_k + dm
    c_o = c_v + dm
    c_g = c_o + dm
    c_l = c_g + 2 * ngate
    splits = (2 * dc, dm, dm, 2 * dl)
    out_dtypes = (F32, BF16, F32, F32)
    splits_t = (dm, dm, 2 * ngate)
    out_dtypes_t = (BF16, BF16, F32)

    experts = (w_gate.astype(BF16), w_up.astype(BF16), w_down.astype(BF16))
    h_lat, h_ctx = x, ctx
    for l in range(depth):
        last = l == depth - 1
        transposed = l % 2 == 1
        tm = SUBLANES * (t // GRID_W) if transposed else min(TOKEN_TILE, t)
        tm_in = min(IN_TILE, t)
        m_lat = [mod[l, :bsz, k * d:(k + 1) * d].reshape(bsz, 1, d) for k in range(6)]
        m_ctx = [mod[l, bsz:bsz + 1, k * d:(k + 1) * d].reshape(1, 1, d) for k in range(6)]
        wl = w_in[l]
        w_tok = jnp.concatenate([wl[:, :c_q], wl[:, c_k:c_v], wl[:, c_o:c_g], wl[:, c_l:]], axis=1).astype(BF16)
        w_chn = jnp.concatenate([wl[:, c_q:c_k], wl[:, c_v:c_o], wl[:, c_g:c_l]], axis=1).T.astype(BF16)
        gate_bias = jnp.broadcast_to(
            jnp.concatenate([mlstm_b_i[l].reshape(-1), mlstm_b_f[l].reshape(-1)])[:, None], (2 * ngate, MLSTM_CHUNK))
        wax = jnp.stack([jnp.concatenate([_block_diag(lru_w_a[l, dd]), _block_diag(lru_w_x[l, dd])], axis=1)
                         for dd in range(N_DIR)]).astype(BF16)
        bax = jnp.concatenate([lru_b_a[l], lru_b_x[l]], axis=-1).reshape(N_DIR, 1, 2 * dl)
        lam = lru_lambda[l].reshape(N_DIR, 1, dl)
        lcb = lru_conv_b[l].reshape(N_DIR, 1, dl)
        wr = jnp.zeros((d, ROUTER_LANES), F32).at[:, :N_GROUPS].set(w_rg[l]).at[:, N_GROUPS:N_GROUPS + N_EXPERTS].set(w_re[l])
        wrh, wrl = _split_hi_lo(wr)
        br = jnp.zeros((1, ROUTER_LANES), F32).at[0, :N_GROUPS].set(b_rg[l]).at[0, N_GROUPS:N_GROUPS + N_EXPERTS].set(b_re[l])
        wo = w_out[l].astype(BF16)

        proj = functools.partial(_in_proj, norm_g=norm1_g[l], w=w_tok, wt=w_chn, gate_bias=gate_bias, splits=splits,
                                 out_dtypes=out_dtypes, splits_t=splits_t, out_dtypes_t=out_dtypes_t)
        cv_l, k_l, o_l, lru_l, qt_l, vt_l, gt_l = proj(h_lat, m_lat[0], m_lat[1], tm=tm_in, transposed=transposed)
        cv_c, k_c, o_c, lru_c, qt_c, vt_c, gt_c = proj(h_ctx, m_ctx[0], m_ctx[1], tm=tm_c, transposed=False)

        a_l = _conformer_conv(cv_l, conv_w[l], conv_b[l], conv_ln_g[l], conv_ln_b[l], tm_in)
        mcf, mcb, mlf, mlb = _mlstm((qt_c, k_c, vt_c, gt_c), (qt_l, k_l, vt_l, gt_l), tb)
        rcf, rcb, rlf, rlb = _rglru(lru_c, lru_l, lru_conv_w[l], lcb, wax, bax, lam, tb)

        h_lat, vx_l, cnt_l = _out_proj(a_l, mlf, mlb, o_l, rlf, rlb, lru_l, h_lat, m_lat[2], m_lat[3], m_lat[4],
                                       norm2_g[l], mlstm_norm_g[l], wo, wrh, wrl, br, tm, transposed)
        h_lat = _moe_layer(vx_l, cnt_l, experts, l, h_lat, m_lat[5], final_g, last, transposed)
        if not last:
            a_c = _conformer_conv(cv_c, conv_w[l], conv_b[l], conv_ln_g[l], conv_ln_b[l], tm_c)
            h_ctx, vx_c, cnt_c = _out_proj(a_c, mcf, mcb, o_c, rcf, rcb, lru_c, h_ctx, m_ctx[2], m_ctx[3], m_ctx[4],
                                           norm2_g[l], mlstm_norm_g[l], wo, wrh, wrl, br, tm_c, False)
            h_ctx = _moe_layer(vx_c, cnt_c, experts, l, h_ctx, m_ctx[5], final_g, False, False)
    return h_lat
```

```python
import functools
import math

import jax
import jax.numpy as jnp
from jax import lax
from jax.experimental import pallas as pl
from jax.experimental.pallas import tpu as pltpu

EPS = 1e-6
GRID_W = 64
CONV_WIDTH = 31
CONV_HALO = 16
MLSTM_HEADS = 4
MLSTM_CHUNK = 128
LRU_CONV_WIDTH = 4
LRU_C = 8.0
N_DIR = 2
N_GROUPS = 4
EXPERTS_PER_GROUP = 4
N_EXPERTS = N_GROUPS * EXPERTS_PER_GROUP
N_PAIRS = EXPERTS_PER_GROUP * (EXPERTS_PER_GROUP - 1) // 2
N_CLASSES = N_GROUPS * N_PAIRS
ROUTER_LANES = 128
INFO_CLASS, INFO_RANK, INFO_W_LO, INFO_W_HI = 0, 1, 2, 3
SUBLANES = 8
LOG2E = 1.4426950408889634

VMEM_LIMIT = 56 * 1024 * 1024
IN_TILE = 1024
TOKEN_TILE = 512
SCAN_BLOCK = 512
MOE_TILE = 512
MOE_TILE_MIN = 128
DISPATCH_TILE = 512
DMA_UNROLL = 8

F32 = jnp.float32
BF16 = jnp.bfloat16


def _cparams(sem):
    return pltpu.CompilerParams(dimension_semantics=sem, vmem_limit_bytes=VMEM_LIMIT)


def _sigmoid(x):
    return jax.nn.sigmoid(x)


def _log_sigmoid(x):
    return jnp.minimum(x, 0.0) - jnp.log1p(jnp.exp(-jnp.abs(x)))


def _softplus(x):
    return jnp.maximum(x, 0.0) + jnp.log1p(jnp.exp(-jnp.abs(x)))


def _gelu_tanh(x):
    return x * (0.5 * (1.0 + jnp.tanh(0.7978845608028654 * (x + 0.044715 * (x * x * x)))))


def _dot(a, b):
    return jnp.dot(a, b, preferred_element_type=F32)


def _split_hi_lo(x):
    hi = x.astype(BF16)
    lo = (x - hi.astype(F32)).astype(BF16)
    return hi, lo


def _load_tile(ref, transposed):
    if not transposed:
        return ref[0]
    return jnp.concatenate([ref[0, :, w, :] for w in range(ref.shape[2])], axis=0)


def _tile_spec(t, tm, d, transposed, to_natural=False):
    if not transposed:
        return pl.BlockSpec((1, tm, d), lambda b, j: (b, j, 0))
    major = GRID_W if to_natural else t // GRID_W
    assert tm % (SUBLANES * major) == 0
    return pl.BlockSpec((1, major, tm // major, d), lambda b, j: (b, 0, j, 0))


def _tile_view(a, transposed, to_natural=False):
    if not transposed:
        return a
    b, t, d = a.shape
    return a.reshape(b, GRID_W, t // GRID_W, d) if to_natural else a.reshape(b, t // GRID_W, GRID_W, d)


def _mod_kernel(c_ref, w_ref, b_ref, o_ref):
    c = c_ref[...]
    s = (c * _sigmoid(c)).astype(BF16)
    o_ref[0] = _dot(s, w_ref[0].astype(BF16)) + b_ref[0]


def _modulation(cvec, w_mod, b_mod):
    nl, d, d6 = w_mod.shape
    rp = cvec.shape[0]
    tn = d6 // 4
    return pl.pallas_call(
        _mod_kernel,
        grid=(nl, d6 // tn),
        in_specs=[pl.BlockSpec((rp, d), lambda l, j: (0, 0)),
                  pl.BlockSpec((1, d, tn), lambda l, j: (l, 0, j)),
                  pl.BlockSpec((1, 1, tn), lambda l, j: (l, 0, j))],
        out_specs=pl.BlockSpec((1, rp, tn), lambda l, j: (l, 0, j)),
        out_shape=jax.ShapeDtypeStruct((nl, rp, d6), F32),
        compiler_params=_cparams(("arbitrary", "arbitrary")),
        name="modulation",
    )(cvec, w_mod, b_mod.reshape(nl, 1, d6))


def _chunk_scan(x, op, reverse):
    L = MLSTM_CHUNK
    n = x.shape[1]
    pos = lax.broadcasted_iota(jnp.int32, x.shape, 1) & (L - 1)
    sh = 1
    while sh < L:
        if reverse:
            xs, ok = pltpu.roll(x, n - sh, 1), pos < L - sh
        else:
            xs, ok = pltpu.roll(x, sh, 1), pos >= sh
        x = jnp.where(ok, op(x, xs), x)
        sh *= 2
    return x


def _gate_scans(g):
    nrow = g.shape[0] // 2
    fwd_row = lax.broadcasted_iota(jnp.int32, (nrow, g.shape[1]), 0) < nrow // N_DIR
    ig2 = g[:nrow] * LOG2E
    lf2 = _log_sigmoid(g[nrow:]) * LOG2E
    b2 = jnp.where(fwd_row, _chunk_scan(lf2, jnp.add, False), _chunk_scan(lf2, jnp.add, True))
    c2 = ig2 - b2
    cm2 = jnp.where(fwd_row, _chunk_scan(c2, jnp.maximum, False), _chunk_scan(c2, jnp.maximum, True))
    return jnp.concatenate([ig2, b2, cm2], axis=0)


def _in_kernel(h_ref, shift_ref, scale_ref, g_ref, w_ref, wt_ref, gbias_ref, *outs, transposed, splits, splits_t):
    h = _load_tile(h_ref, transposed)
    ms = jnp.mean(h * h, axis=-1, keepdims=True)
    u = (h * lax.rsqrt(ms + EPS)) * g_ref[...]
    u = (u * (1.0 + scale_ref[0]) + shift_ref[0]).astype(BF16)
    c0 = 0
    for o_ref, width in zip(outs, splits):
        o_ref[0] = _dot(u, w_ref[:, c0:c0 + width]).astype(o_ref.dtype)
        c0 += width
    r0 = 0
    outs_t = outs[len(splits):]
    for o_ref, width in zip(outs_t[:-1], splits_t[:-1]):
        o_ref[0] = lax.dot_general(wt_ref[r0:r0 + width, :], u, (((1,), (1,)), ((), ())),
                                   preferred_element_type=F32).astype(o_ref.dtype)
        r0 += width
    gates = lax.dot_general(wt_ref[r0:r0 + splits_t[-1], :], u, (((1,), (1,)), ((), ())), preferred_element_type=F32)
    tm = gates.shape[1]
    outs_t[-1][0] = _gate_scans(gates + jnp.concatenate([gbias_ref[...]] * (tm // MLSTM_CHUNK), axis=1))


def _in_proj(h, shift, scale, norm_g, w, wt, gate_bias, splits, out_dtypes, splits_t, out_dtypes_t, tm, transposed):
    b, t, d = h.shape
    per_batch = shift.shape[0] == b
    mod_spec = pl.BlockSpec((1, 1, d), (lambda bi, j: (bi, 0, 0)) if per_batch else (lambda bi, j: (0, 0, 0)))
    rows_t = splits_t[:-1] + (splits_t[-1] // 2 * 3,)
    return pl.pallas_call(
        functools.partial(_in_kernel, transposed=transposed, splits=splits, splits_t=splits_t),
        grid=(b, t // tm),
        in_specs=[_tile_spec(t, tm, d, transposed), mod_spec, mod_spec, pl.BlockSpec((1, d), lambda bi, j: (0, 0)),
                  pl.BlockSpec(w.shape, lambda bi, j: (0, 0)), pl.BlockSpec(wt.shape, lambda bi, j: (0, 0)),
                  pl.BlockSpec(gate_bias.shape, lambda bi, j: (0, 0))],
        out_specs=([pl.BlockSpec((1, tm, width), lambda bi, j: (bi, j, 0)) for width in splits]
                   + [pl.BlockSpec((1, width, tm), lambda bi, j: (bi, 0, j)) for width in rows_t]),
        out_shape=([jax.ShapeDtypeStruct((b, t, width), dt) for width, dt in zip(splits, out_dtypes)]
                   + [jax.ShapeDtypeStruct((b, width, t), dt) for width, dt in zip(rows_t, out_dtypes_t)]),
        compiler_params=_cparams(("arbitrary", "arbitrary")),
        name="in_proj",
    )(_tile_view(h, transposed), shift, scale, norm_g.reshape(1, d), w, wt, gate_bias)


def _conv_kernel(prev_ref, cur_ref, next_ref, w_ref, b_ref, lng_ref, lnb_ref, o_ref, *, tb, nblk, dc):
    j = pl.program_id(1)

    def glu(x):
        return x[:, :dc] * _sigmoid(x[:, dc:])

    up = jnp.where(j > 0, glu(prev_ref[0]), 0.0)
    un = jnp.where(j < nblk - 1, glu(next_ref[0]), 0.0)
    ext = jnp.concatenate([up, glu(cur_ref[0]), un], axis=0)
    base = CONV_HALO - CONV_WIDTH // 2
    nrows = tb + 2 * CONV_HALO
    acc = jnp.zeros((tb, dc), F32)
    for b in range(SUBLANES):
        shifted = ext if b == 0 else pltpu.roll(ext, nrows - b, 0)
        for a in range((base + CONV_WIDTH - 1 - b) // SUBLANES + 1):
            k = SUBLANES * a + b - base
            if 0 <= k < CONV_WIDTH:
                acc = acc + w_ref[k:k + 1, :] * shifted[SUBLANES * a:SUBLANES * a + tb, :]
    acc = acc + b_ref[...]
    mu = jnp.mean(acc, axis=-1, keepdims=True)
    xc = acc - mu
    y = xc * lax.rsqrt(jnp.mean(xc * xc, axis=-1, keepdims=True) + EPS) * lng_ref[...] + lnb_ref[...]
    o_ref[0] = (y * _sigmoid(y)).astype(o_ref.dtype)


def _conformer_conv(cv, w, b, ln_g, ln_b, tb):
    bsz, t, c2 = cv.shape
    dc = c2 // 2
    nblk = t // tb
    hb = tb // CONV_HALO
    nh = t // CONV_HALO
    wp = jnp.zeros((CONV_WIDTH + 1, dc), F32).at[:CONV_WIDTH].set(w)
    vec = pl.BlockSpec((1, dc), lambda bi, j: (0, 0))
    return pl.pallas_call(
        functools.partial(_conv_kernel, tb=tb, nblk=nblk, dc=dc),
        grid=(bsz, nblk),
        in_specs=[pl.BlockSpec((1, CONV_HALO, c2), lambda bi, j: (bi, jnp.maximum(j * hb - 1, 0), 0)),
                  pl.BlockSpec((1, tb, c2), lambda bi, j: (bi, j, 0)),
                  pl.BlockSpec((1, CONV_HALO, c2), lambda bi, j: (bi, jnp.minimum((j + 1) * hb, nh - 1), 0)),
                  pl.BlockSpec((CONV_WIDTH + 1, dc), lambda bi, j: (0, 0)), vec, vec, vec],
        out_specs=pl.BlockSpec((1, tb, dc), lambda bi, j: (bi, j, 0)),
        out_shape=jax.ShapeDtypeStruct((bsz, t, dc), BF16),
        compiler_params=_cparams(("arbitrary", "arbitrary")),
        name="conformer_conv",
    )(cv, cv, cv, wp, b.reshape(1, dc), ln_g.reshape(1, dc), ln_b.reshape(1, dc))


def _mlstm_block(qt_ref, k_ref, vt_ref, gt_ref, out_ref, c_ref, n_ref, m_ref, d, nchunks, hd):
    L = MLSTM_CHUNK
    nh = MLSTM_HEADS
    assert hd == L
    nrow = N_DIR * nh
    log2_scale = math.log2(hd ** -0.5)
    ig2, b2, cm2 = gt_ref[0, 0:nrow], gt_ref[0, nrow:2 * nrow], gt_ref[0, 2 * nrow:3 * nrow]
    c2 = ig2 - b2
    row_s = lax.broadcasted_iota(jnp.int32, (L, L), 0)
    col_t = lax.broadcasted_iota(jnp.int32, (L, L), 1)
    mask = (row_s <= col_t) if d == 0 else (row_s >= col_t)
    ones16 = jnp.ones((2 * SUBLANES, L), BF16)
    for ci in (range(nchunks) if d == 0 else range(nchunks - 1, -1, -1)):
        sl = slice(ci * L, (ci + 1) * L)
        b2c, c2c, ig2c = b2[:, sl], c2[:, sl], ig2[:, sl]
        m2 = m_ref[d]
        a2 = -jnp.maximum(m2, cm2[:, sl])
        inter = jnp.exp2(m2 + a2)
        edn = jnp.exp2(a2 - b2c)
        b_last = jnp.broadcast_to(b2c[:, L - 1:L] if d == 0 else b2c[:, 0:1], (nrow, L))
        logw2 = b_last - b2c + ig2c
        m2_new = jnp.maximum(b_last + m2, jnp.broadcast_to(jnp.max(logw2, axis=-1, keepdims=True), (nrow, L)))
        wgt = jnp.exp2(logw2 - m2_new + log2_scale)
        decay = jnp.exp2(b_last + m2 - m2_new)
        m_ref[d] = m2_new
        c_cols = (c2c + log2_scale).T
        for h in range(nh):
            r = d * nh + h
            e = c_cols[:, r:r + 1] + a2[r:r + 1]
            p = jnp.where(mask, jnp.exp2(e), 0.0)
            qt = qt_ref[0, h * hd:(h + 1) * hd, sl]
            k = k_ref[0, ci * L:(ci + 1) * L, h * hd:(h + 1) * hd]
            vt = vt_ref[0, h * hd:(h + 1) * hd, sl]
            ct = c_ref[r]
            n16 = n_ref[r]
            st = _dot(k, qt) * p
            nq = _dot(n16.astype(BF16), qt)
            nd = _dot(vt, st.astype(BF16))
            inter_r = inter[r:r + 1]
            num = nd + inter_r * _dot(ct.astype(BF16), qt)
            den = jnp.sum(st, axis=0, keepdims=True) + inter_r * nq[0:1]
            ht = num * (1.0 / jnp.maximum(jnp.abs(den), edn[r:r + 1]))
            out_ref[0, ci * L:(ci + 1) * L, h * hd:(h + 1) * hd] = ht.T
            w_r = wgt[r:r + 1]
            vw = (vt.astype(F32) * w_r).astype(BF16)
            upd = _dot(jnp.concatenate([vw, ones16 * w_r.astype(BF16)], axis=0), k)
            dec = decay[r:r + 1]
            c_ref[r] = dec * ct + upd[:hd]
            n_ref[r] = dec * n16 + upd[hd:]


def _mlstm_steps(qtc_ref, kc_ref, vtc_ref, gtc_ref, qtf_ref, kf_ref, vtf_ref, gtf_ref,
                 qtb_ref, kb_ref, vtb_ref, gtb_ref,
                 hcf_ref, hcb_ref, hf_ref, hb_ref, c_ref, n_ref, m_ref, *, nc_ctx, nc_lat, hd):
    state = (c_ref, n_ref, m_ref)

    def first():
        c_ref[...] = jnp.zeros_like(c_ref)
        n_ref[...] = jnp.zeros_like(n_ref)
        m_ref[...] = jnp.zeros_like(m_ref)
        _mlstm_block(qtc_ref, kc_ref, vtc_ref, gtc_ref, hcf_ref, *state, 0, nc_ctx, hd)
        _mlstm_block(qtc_ref, kc_ref, vtc_ref, gtc_ref, hcb_ref, *state, 1, nc_ctx, hd)

    def rest():
        _mlstm_block(qtf_ref, kf_ref, vtf_ref, gtf_ref, hf_ref, *state, 0, nc_lat, hd)
        _mlstm_block(qtb_ref, kb_ref, vtb_ref, gtb_ref, hb_ref, *state, 1, nc_lat, hd)

    return first, rest


def _lru_block(x, halo, d, cw, cb, wax, bax, lam, h0):
    tb, dl = x.shape
    kw = LRU_CONV_WIDTH
    if d == 0:
        ext = jnp.concatenate([halo, x], axis=0)
        taps = [ext[SUBLANES - (kw - 1) + j:SUBLANES - (kw - 1) + j + tb] for j in range(kw)]
    else:
        ext = jnp.concatenate([x, halo], axis=0)
        taps = [ext[kw - 1 - j:kw - 1 - j + tb] for j in range(kw)]
    xc = cb
    for j in range(kw):
        xc = xc + cw[j:j + 1, :] * taps[j]
    ri = _sigmoid(_dot(xc.astype(BF16), wax) + bax)
    r, i = ri[:, :dl], ri[:, dl:]
    log_a = (-LRU_C * r) * _softplus(-lam)
    a = jnp.exp(log_a)
    th = jnp.tanh(log_a)
    u = jnp.sqrt(-2.0 * th / (1.0 - th)) * (i * xc)
    ngrp = tb // SUBLANES
    a = a.reshape(ngrp, SUBLANES, dl)
    u = u.reshape(ngrp, SUBLANES, dl)
    sub = lax.broadcasted_iota(jnp.int32, a.shape, 1)
    for sh in (1, 2, 4):
        if d == 0:
            a_s, u_s, msk = pltpu.roll(a, sh, 1), pltpu.roll(u, sh, 1), sub >= sh
        else:
            a_s, u_s, msk = pltpu.roll(a, SUBLANES - sh, 1), pltpu.roll(u, SUBLANES - sh, 1), sub < SUBLANES - sh
        u = jnp.where(msk, a * u_s + u, u)
        a = jnp.where(msk, a * a_s, a)
    a = a.reshape(tb, dl)
    u = u.reshape(tb, dl)
    outs = [None] * ngrp
    carry = h0
    for j in (range(ngrp) if d == 0 else range(ngrp - 1, -1, -1)):
        hj = u[j * SUBLANES:(j + 1) * SUBLANES] + a[j * SUBLANES:(j + 1) * SUBLANES] * carry
        outs[j] = hj
        carry = hj[SUBLANES - 1:SUBLANES] if d == 0 else hj[0:1]
    return jnp.concatenate(outs, axis=0), carry


def _lru_steps(xc_ref, xf_ref, xb_ref, cw_ref, cb_ref, wax_ref, bax_ref, lam_ref,
               hcf_ref, hcb_ref, hf_ref, hb_ref, hcar_ref, halo_ref, *, s):
    def params(d):
        return cw_ref[d], cb_ref[d], wax_ref[d], bax_ref[d], lam_ref[d]

    def first():
        x = xc_ref[0]
        zero_halo = jnp.zeros((SUBLANES, x.shape[1]), F32)
        zero_h = jnp.zeros((1, x.shape[1]), F32)
        for d, o_ref in ((0, hcf_ref), (1, hcb_ref)):
            h, carry = _lru_block(x, zero_halo, d, *params(d), zero_h)
            o_ref[0] = h
            hcar_ref[d] = jnp.broadcast_to(carry, hcar_ref.shape[1:])

    def rest():
        for d, x_ref, o_ref in ((0, xf_ref, hf_ref), (1, xb_ref, hb_ref)):
            x = x_ref[0]
            halo = jnp.where(s > 1, halo_ref[d], 0.0)
            h, carry = _lru_block(x, halo, d, *params(d), hcar_ref[d][0:1])
            o_ref[0] = h
            hcar_ref[d] = jnp.broadcast_to(carry, hcar_ref.shape[1:])
            halo_ref[d] = x[x.shape[0] - SUBLANES:] if d == 0 else x[:SUBLANES]

    return first, rest


N_MLSTM_IN, N_LRU_IN, N_MIX_OUT, N_MLSTM_SCRATCH = 12, 8, 4, 3


def _mixer_kernel(*refs, nc_ctx, nc_lat, hd):
    m_in, refs = refs[:N_MLSTM_IN], refs[N_MLSTM_IN:]
    l_in, refs = refs[:N_LRU_IN], refs[N_LRU_IN:]
    m_out, l_out, refs = refs[:N_MIX_OUT], refs[N_MIX_OUT:2 * N_MIX_OUT], refs[2 * N_MIX_OUT:]
    m_scratch, l_scratch = refs[:N_MLSTM_SCRATCH], refs[N_MLSTM_SCRATCH:]
    s = pl.program_id(1)
    m_first, m_rest = _mlstm_steps(*m_in, *m_out, *m_scratch, nc_ctx=nc_ctx, nc_lat=nc_lat, hd=hd)
    l_first, l_rest = _lru_steps(*l_in, *l_out, *l_scratch, s=s)

    @pl.when(s == 0)
    def _():
        m_first()
        l_first()

    @pl.when(s > 0)
    def _():
        m_rest()
        l_rest()


def _recurrent_mixers(ctx_in, lat_in, lru_c, lru_l, cw, cb, wax, bax, lam, tb):
    bsz, tc, dm = ctx_in[1].shape
    t = lat_in[1].shape[1]
    ng = lat_in[3].shape[1]
    dl = lru_l.shape[2] // 2
    hd = dm // MLSTM_HEADS
    nb = t // tb
    nstate = N_DIR * MLSTM_HEADS

    def fwd(s):
        return jnp.maximum(s - 1, 0)

    def bwd(s):
        return nb - 1 - jnp.maximum(s - 1, 0)

    def first(s):
        return 0

    def rows(tlen, width, blk):
        return pl.BlockSpec((1, tlen, width), lambda bi, s: (bi, blk(s), 0))

    def cols(tlen, width, blk):
        return pl.BlockSpec((1, width, tlen), lambda bi, s: (bi, 0, blk(s)))

    def mlstm_in(tlen, blk):
        return [cols(tlen, dm, blk), rows(tlen, dm, blk), cols(tlen, dm, blk), cols(tlen, ng, blk)]

    def whole(a):
        return pl.BlockSpec(a.shape, lambda bi, s: (0,) * a.ndim)

    def outs(width):
        return [rows(tc, width, first), rows(tc, width, first), rows(tb, width, fwd), rows(tb, width, bwd)]

    def out_shapes(width):
        return [jax.ShapeDtypeStruct((bsz, n, width), F32) for n in (tc, tc, t, t)]

    res = pl.pallas_call(
        functools.partial(_mixer_kernel, nc_ctx=tc // MLSTM_CHUNK, nc_lat=tb // MLSTM_CHUNK, hd=hd),
        grid=(bsz, nb + 1),
        in_specs=(mlstm_in(tc, first) + mlstm_in(tb, fwd) + mlstm_in(tb, bwd)
                  + [rows(tc, dl, first), rows(tb, dl, fwd), rows(tb, dl, bwd)]
                  + [whole(a) for a in (cw, cb, wax, bax, lam)]),
        out_specs=outs(dm) + outs(dl),
        out_shape=out_shapes(dm) + out_shapes(dl),
        scratch_shapes=[pltpu.VMEM((nstate, hd, hd), F32), pltpu.VMEM((nstate, 2 * SUBLANES, hd), F32),
                        pltpu.VMEM((N_DIR, nstate, MLSTM_CHUNK), F32),
                        pltpu.VMEM((N_DIR, SUBLANES, dl), F32), pltpu.VMEM((N_DIR, SUBLANES, dl), F32)],
        compiler_params=_cparams(("arbitrary", "arbitrary")),
        name="recurrent_mixers",
    )(*ctx_in, *lat_in, *lat_in, lru_c, lru_l, lru_l, cw, cb, wax, bax, lam)
    return res[:N_MIX_OUT], res[N_MIX_OUT:]


def _route(logits):
    ng, ne = N_GROUPS, EXPERTS_PER_GROUP
    lane = lax.broadcasted_iota(jnp.int32, logits.shape, 1).astype(F32)
    big = float(ROUTER_LANES)
    is_g = lane < ng
    gl = jnp.where(is_g, logits, -jnp.inf)
    gmax = jnp.max(gl, axis=-1, keepdims=True)
    g_sel = jnp.min(jnp.where(gl == gmax, lane, big), axis=-1, keepdims=True)
    p_g = 1.0 / jnp.sum(jnp.where(is_g, jnp.exp(logits - gmax), 0.0), axis=-1, keepdims=True)
    lo = ng + ne * g_sel
    el = jnp.where((lane >= lo) & (lane < lo + ne), logits, -jnp.inf)
    v1 = jnp.max(el, axis=-1, keepdims=True)
    i1 = jnp.min(jnp.where(el == v1, lane, big), axis=-1, keepdims=True)
    el2 = jnp.where(lane == i1, -jnp.inf, el)
    v2 = jnp.max(el2, axis=-1, keepdims=True)
    i2 = jnp.min(jnp.where(el2 == v2, lane, big), axis=-1, keepdims=True)
    e2 = jnp.exp(v2 - v1)
    w1 = p_g / (1.0 + e2)
    w2 = p_g * e2 / (1.0 + e2)
    first_lower = i1 < i2
    e_lo = jnp.minimum(i1, i2) - lo
    e_hi = jnp.maximum(i1, i2) - lo
    pair = e_lo * (2 * ne - 1 - e_lo) * 0.5 + (e_hi - e_lo - 1.0)
    return g_sel * N_PAIRS + pair, jnp.where(first_lower, w1, w2), jnp.where(first_lower, w2, w1)


def _out_kernel(a_ref, mf_ref, mb_ref, o_ref, lf_ref, lb_ref, lg_ref, h_ref, g1_ref, sh2_ref, sc2_ref,
                n2g_ref, mng_ref, wout_ref, wrh_ref, wrl_ref, br_ref,
                hnew_ref, vx_ref, cnt_ref, info_ref, run_ref, *, transposed, d, hd):
    @pl.when((pl.program_id(0) == 0) & (pl.program_id(1) == 0))
    def _():
        run_ref[...] = jnp.zeros_like(run_ref)

    dc = a_ref.shape[2]
    dm = mf_ref.shape[2]
    mh = mf_ref[0] + mb_ref[0]
    parts = []
    for h in range(dm // hd):
        x = mh[:, h * hd:(h + 1) * hd]
        xc = x - jnp.mean(x, axis=-1, keepdims=True)
        parts.append(xc * lax.rsqrt(jnp.mean(xc * xc, axis=-1, keepdims=True) + EPS) * mng_ref[:, h * hd:(h + 1) * hd])
    m_out = (_sigmoid(o_ref[0]) * jnp.concatenate(parts, axis=-1)).astype(BF16)
    r_out = ((lf_ref[0] + lb_ref[0]) * _gelu_tanh(lg_ref[0])).astype(BF16)
    y = (_dot(a_ref[0], wout_ref[0:dc, :]) + _dot(m_out, wout_ref[dc:dc + dm, :])
         + _dot(r_out, wout_ref[dc + dm:, :]))
    hn = _load_tile(h_ref, transposed) + g1_ref[0] * y
    hnew_ref[0] = hn
    ms = jnp.mean(hn * hn, axis=-1, keepdims=True)
    v = (hn * lax.rsqrt(ms + EPS)) * n2g_ref[...]
    v = v * (1.0 + sc2_ref[0]) + sh2_ref[0]
    vh, vl = _split_hi_lo(v)
    logits = _dot(vh, wrh_ref[...]) + _dot(vl, wrh_ref[...]) + _dot(vh, wrl_ref[...]) + br_ref[...]
    cls, w_lo, w_hi = _route(logits)
    tm = v.shape[0]
    lane = lax.broadcasted_iota(jnp.int32, (tm, ROUTER_LANES), 1).astype(F32)
    onehot = jnp.where(lane == cls, 1.0, 0.0)
    row = lax.broadcasted_iota(jnp.int32, (tm, tm), 0)
    col = lax.broadcasted_iota(jnp.int32, (tm, tm), 1)
    before = _dot(jnp.where(col < row, 1.0, 0.0).astype(BF16), onehot.astype(BF16)) + run_ref[0:1, :]
    rank = jnp.sum(onehot * before, axis=-1, keepdims=True)
    run_ref[...] = run_ref[...] + jnp.sum(onehot, axis=0, keepdims=True)
    cnt_ref[...] = run_ref[...]
    info = (jnp.where(lane == INFO_CLASS, cls, 0.0) + jnp.where(lane == INFO_RANK, rank, 0.0)
            + jnp.where(lane == INFO_W_LO, w_lo, 0.0) + jnp.where(lane == INFO_W_HI, w_hi, 0.0))
    vx_ref[0, :, :d] = v
    vx_ref[0, :, d:] = info
    info_ref[0] = info.T[:SUBLANES]


def _out_proj(a, mf, mb, o, lf, lb, lru, h, g1, sh2, sc2, n2g, mng, w_out, wrh, wrl, br, tm, transposed):
    bsz, t, d = h.shape
    dc, dm, dl = a.shape[2], mf.shape[2], lf.shape[2]
    per_batch = g1.shape[0] == bsz
    mod_spec = pl.BlockSpec((1, 1, d), (lambda bi, j: (bi, 0, 0)) if per_batch else (lambda bi, j: (0, 0, 0)))

    def scan(width, blk=0):
        return pl.BlockSpec((1, tm, width), lambda bi, j: (bi, j, blk))

    def whole(x):
        return pl.BlockSpec(x.shape, lambda bi, j: (0,) * x.ndim)

    dx = d + ROUTER_LANES
    n2g, mng = n2g.reshape(1, d), mng.reshape(1, dm)
    hv = _tile_view(h, transposed)
    hnew, vx, cnt, info_t = pl.pallas_call(
        functools.partial(_out_kernel, transposed=transposed, d=d, hd=dm // MLSTM_HEADS),
        grid=(bsz, t // tm),
        in_specs=[scan(dc), scan(dm), scan(dm), scan(dm), scan(dl), scan(dl), scan(dl, 1),
                  _tile_spec(t, tm, d, transposed),
                  mod_spec, mod_spec, mod_spec, whole(n2g), whole(mng), whole(w_out), whole(wrh), whole(wrl), whole(br)],
        out_specs=[scan(d), scan(dx), pl.BlockSpec((SUBLANES, ROUTER_LANES), lambda bi, j: (0, 0)),
                   pl.BlockSpec((1, SUBLANES, tm), lambda bi, j: (bi, 0, j))],
        out_shape=[jax.ShapeDtypeStruct((bsz, t, d), F32), jax.ShapeDtypeStruct((bsz, t, dx), F32),
                   jax.ShapeDtypeStruct((SUBLANES, ROUTER_LANES), F32), jax.ShapeDtypeStruct((bsz, SUBLANES, t), F32)],
        scratch_shapes=[pltpu.VMEM((SUBLANES, ROUTER_LANES), F32)],
        compiler_params=_cparams(("arbitrary", "arbitrary")),
        name="out_proj",
    )(a, mf, mb, o, lf, lb, lru, hv, g1, sh2, sc2, n2g, mng, w_out, wrh, wrl, br)
    return hnew, vx, (cnt, info_t)


def _routing_tables(route, tmo, n_max):
    cnt, info_t = route
    cls = info_t[:, INFO_CLASS, :].reshape(-1).astype(jnp.int32)
    rank = info_t[:, INFO_RANK, :].reshape(-1).astype(jnp.int32)
    counts = cnt[0, :N_CLASSES].astype(jnp.int32)
    tiles = (counts + tmo - 1) // tmo
    tile_end = jnp.cumsum(tiles)
    tile_start = tile_end - tiles
    n_real = tile_end[N_CLASSES - 1]
    slot = jnp.take(tile_start * tmo, cls) + rank
    blk = jnp.minimum(jnp.arange(n_max, dtype=jnp.int32), n_real - 1)
    tile_cls = jnp.minimum(jnp.sum((tile_end[None, :] <= blk[:, None]).astype(jnp.int32), axis=1), N_CLASSES - 1)
    pair_lo = jnp.array([a for a in range(EXPERTS_PER_GROUP) for b in range(a + 1, EXPERTS_PER_GROUP)], jnp.int32)
    pair_hi = jnp.array([b for a in range(EXPERTS_PER_GROUP) for b in range(a + 1, EXPERTS_PER_GROUP)], jnp.int32)
    grp = tile_cls // N_PAIRS
    e_lo = grp * EXPERTS_PER_GROUP + jnp.take(pair_lo, tile_cls % N_PAIRS)
    e_hi = grp * EXPERTS_PER_GROUP + jnp.take(pair_hi, tile_cls % N_PAIRS)
    tail = n_real + jnp.arange(N_CLASSES, dtype=jnp.int32)
    fill = jnp.concatenate([jnp.where(tiles > 0, tile_end - 1, -1), jnp.where(tail < n_max, tail, -1)])
    return slot, blk, e_lo, e_hi, n_real.reshape(1), fill.astype(jnp.int32)


def _dispatch_kernel(fill_ref, slot_ref, vx_ref, xs_ref, zero_ref, zsem, sem):
    tm = vx_ref.shape[0] * SUBLANES
    tmo = zero_ref.shape[0]

    @pl.when(pl.program_id(0) == 0)
    def _():
        zero_ref[...] = jnp.zeros_like(zero_ref)

        def fill(j):
            return pltpu.make_async_copy(zero_ref, xs_ref.at[pl.ds(pl.multiple_of(fill_ref[j] * tmo, tmo), tmo)], zsem)

        for j in range(fill_ref.shape[0]):
            pl.when(fill_ref[j] >= 0)(lambda j=j: fill(j).start())
        for j in range(fill_ref.shape[0]):
            pl.when(fill_ref[j] >= 0)(lambda j=j: fill(j).wait())

    def issue(g, carry):
        for p in range(SUBLANES):
            dst = xs_ref.at[pl.ds(slot_ref[0, 0, g * SUBLANES + p], 1)]
            pltpu.make_async_copy(vx_ref.at[g, pl.ds(p, 1)], dst, sem).start()
        return carry

    lax.fori_loop(0, tm // SUBLANES, issue, 0, unroll=DMA_UNROLL)
    rows = xs_ref.at[pl.ds(0, tm)]
    pltpu.make_async_copy(rows, rows, sem).wait()


def _dispatch(vx, slot, fill, n_rows, tmo, tm):
    n, dx = vx.shape
    grid_spec = pltpu.PrefetchScalarGridSpec(
        num_scalar_prefetch=1,
        grid=(n // tm,),
        in_specs=[pl.BlockSpec((1, 1, tm), lambda i, pad: (i, 0, 0), memory_space=pltpu.SMEM),
                  pl.BlockSpec((tm // SUBLANES, SUBLANES, dx), lambda i, pad: (i, 0, 0))],
        out_specs=pl.BlockSpec(memory_space=pl.ANY),
        scratch_shapes=[pltpu.VMEM((tmo, dx), F32), pltpu.SemaphoreType.DMA(()), pltpu.SemaphoreType.DMA(())])
    return pl.pallas_call(
        _dispatch_kernel,
        grid_spec=grid_spec,
        out_shape=jax.ShapeDtypeStruct((n_rows, dx), F32),
        compiler_params=_cparams(("arbitrary",)),
        name="moe_dispatch",
    )(fill, slot.reshape(n // tm, 1, tm), vx.reshape(n // SUBLANES, SUBLANES, dx))


def _expert(x, wg_ref, wu_ref, wd_ref):
    g = _dot(x, wg_ref[0, 0])
    return _dot((g * _sigmoid(g) * _dot(x, wu_ref[0, 0])).astype(BF16), wd_ref[0, 0])


def _moe_kernel(blk_ref, elo_ref, ehi_ref, nreal_ref, xs_ref, wg_lo_ref, wu_lo_ref, wd_lo_ref,
                wg_hi_ref, wu_hi_ref, wd_hi_ref, ys_ref):
    del blk_ref, elo_ref, ehi_ref
    d = ys_ref.shape[1]

    @pl.when(pl.program_id(0) < nreal_ref[0])
    def _():
        x = xs_ref[:, :d].astype(BF16)
        w_lo = xs_ref[:, d + INFO_W_LO:d + INFO_W_LO + 1]
        w_hi = xs_ref[:, d + INFO_W_HI:d + INFO_W_HI + 1]
        ys_ref[...] = (w_lo * _expert(x, wg_lo_ref, wu_lo_ref, wd_lo_ref)
                       + w_hi * _expert(x, wg_hi_ref, wu_hi_ref, wd_hi_ref))

    @pl.when(pl.program_id(0) >= nreal_ref[0])
    def _():
        ys_ref[...] = jnp.zeros_like(ys_ref)


def _moe(xs, blk, e_lo, e_hi, n_real, experts, layer, tm):
    wg, wu, wd = experts
    dx = xs.shape[1]
    n_tiles = blk.shape[0]
    n_rows = n_tiles * tm
    _, _, d, de = wg.shape
    up = pl.BlockSpec((1, 1, d, de), lambda i, blk, lo, hi, nr: (layer, lo[i], 0, 0)), \
        pl.BlockSpec((1, 1, d, de), lambda i, blk, lo, hi, nr: (layer, hi[i], 0, 0))
    down = pl.BlockSpec((1, 1, de, d), lambda i, blk, lo, hi, nr: (layer, lo[i], 0, 0)), \
        pl.BlockSpec((1, 1, de, d), lambda i, blk, lo, hi, nr: (layer, hi[i], 0, 0))
    grid_spec = pltpu.PrefetchScalarGridSpec(
        num_scalar_prefetch=4,
        grid=(n_tiles,),
        in_specs=[pl.BlockSpec((tm, dx), lambda i, blk, lo, hi, nr: (blk[i], 0)),
                  up[0], up[0], down[0], up[1], up[1], down[1]],
        out_specs=pl.BlockSpec((tm, d), lambda i, blk, lo, hi, nr: (i, 0)))
    return pl.pallas_call(
        _moe_kernel,
        grid_spec=grid_spec,
        out_shape=jax.ShapeDtypeStruct((n_rows, d), F32),
        compiler_params=_cparams(("arbitrary",)),
        name="moe",
    )(blk, e_lo, e_hi, n_real, xs, wg, wu, wd, wg, wu, wd)


def _combine_kernel(slot_ref, slot_next_ref, ys_ref, h_ref, g2_ref, g_ref, o_ref, buf_ref, sem, *, final, transposed):
    i = pl.program_id(0) * pl.num_programs(1) + pl.program_id(1)
    n = pl.num_programs(0) * pl.num_programs(1)
    tm = o_ref.shape[1]

    def gather(s_ref, b):
        def issue(g, carry):
            for p in range(SUBLANES):
                src = ys_ref.at[pl.ds(s_ref[0, 0, g * SUBLANES + p], 1)]
                pltpu.make_async_copy(src, buf_ref.at[b, g, pl.ds(p, 1)], sem.at[b]).start()
            return carry
        lax.fori_loop(0, tm // SUBLANES, issue, 0, unroll=DMA_UNROLL)

    cur = i % 2

    @pl.when(i == 0)
    def _():
        gather(slot_ref, 0)

    @pl.when(i + 1 < n)
    def _():
        gather(slot_next_ref, 1 - cur)

    rows = ys_ref.at[pl.ds(0, tm)]
    pltpu.make_async_copy(rows, rows, sem.at[cur]).wait()
    h = _load_tile(h_ref, transposed) + g2_ref[0] * buf_ref[cur].reshape(tm, -1)
    if final:
        ms = jnp.mean(h * h, axis=-1, keepdims=True)
        h = (h * lax.rsqrt(ms + EPS)) * g_ref[...]
    o_ref[0] = h


def _combine(ys, slot, h, g2, g, tm, final, transposed):
    bsz, t, d = h.shape
    nj = t // tm
    nt = bsz * nj
    per_batch = g2.shape[0] == bsz
    slot3 = slot.reshape(nt, 1, tm)
    smem = functools.partial(pl.BlockSpec, (1, 1, tm), memory_space=pltpu.SMEM)
    return pl.pallas_call(
        functools.partial(_combine_kernel, final=final, transposed=transposed),
        grid=(bsz, nj),
        in_specs=[smem(lambda bi, j: (bi * nj + j, 0, 0)),
                  smem(lambda bi, j: (jnp.minimum(bi * nj + j + 1, nt - 1), 0, 0)),
                  pl.BlockSpec(memory_space=pl.ANY),
                  _tile_spec(t, tm, d, transposed, to_natural=True),
                  pl.BlockSpec((1, 1, d), (lambda bi, j: (bi, 0, 0)) if per_batch else (lambda bi, j: (0, 0, 0))),
                  pl.BlockSpec((1, d), lambda bi, j: (0, 0))],
        out_specs=pl.BlockSpec((1, tm, d), lambda bi, j: (bi, j, 0)),
        out_shape=jax.ShapeDtypeStruct((bsz, t, d), F32),
        scratch_shapes=[pltpu.VMEM((2, tm // SUBLANES, SUBLANES, d), F32), pltpu.SemaphoreType.DMA((2,))],
        compiler_params=_cparams(("arbitrary", "arbitrary")),
        name="moe_combine",
    )(slot3, slot3, ys, _tile_view(h, transposed, to_natural=True), g2, g.reshape(1, d))


def _moe_layer(vx, route, experts, layer, h, g2, g, final, transposed):
    bsz, t, d = h.shape
    n = bsz * t
    tmo = MOE_TILE
    while tmo > MOE_TILE_MIN and n < 2 * N_CLASSES * tmo:
        tmo //= 2
    n_max = n // tmo + N_CLASSES
    vx = vx.reshape(n, d + ROUTER_LANES)
    slot, blk, e_lo, e_hi, n_real, fill = _routing_tables(route, tmo, n_max)
    xs = _dispatch(vx, slot, fill, n_max * tmo, tmo, min(DISPATCH_TILE, n))
    ys = _moe(xs, blk, e_lo, e_hi, n_real, experts, layer, tmo)
    if transposed:
        slot = slot.reshape(bsz, GRID_W, t // GRID_W).transpose(0, 2, 1)
        tm = SUBLANES * GRID_W
    else:
        tm = min(DISPATCH_TILE, t)
    return _combine(ys, slot, h, g2, g, tm, final, transposed)


def _block_diag(w):
    nh, bw, _ = w.shape
    eye = jnp.eye(nh, dtype=w.dtype)
    return (eye[:, None, :, None] * w[:, :, None, :]).reshape(nh * bw, nh * bw)


def kernel(x, c, ctx, c_ctx, w_mod, b_mod, norm1_g, norm2_g, w_in, conv_w, conv_b, conv_ln_g, conv_ln_b,
           mlstm_b_i, mlstm_b_f, mlstm_norm_g, lru_conv_w, lru_conv_b, lru_w_a, lru_b_a, lru_w_x, lru_b_x,
           lru_lambda, w_out, w_rg, b_rg, w_re, b_re, w_gate, w_up, w_down, final_g):
    bsz, t, d = x.shape
    tc = ctx.shape[1]
    depth = w_mod.shape[0]
    dc = conv_w.shape[2]
    dm = mlstm_norm_g.shape[1]
    dl = lru_lambda.shape[2]
    ngate = N_DIR * MLSTM_HEADS
    tm_c = min(TOKEN_TILE, tc)
    tb = min(SCAN_BLOCK, t)

    rp = -(-(bsz + 1) // SUBLANES) * SUBLANES
    cvec = jnp.zeros((rp, d), F32).at[:bsz].set(c).at[bsz].set(c_ctx)
    mod = _modulation(cvec, w_mod, b_mod)

    c_q = 2 * dc
    c_k = c_q + dm
    c_v = c_k + dm
    c_o = c_v + dm
    c_g = c_o + dm
    c_l = c_g + 2 * ngate
    splits = (2 * dc, dm, dm, 2 * dl)
    out_dtypes = (F32, BF16, F32, F32)
    splits_t = (dm, dm, 2 * ngate)
    out_dtypes_t = (BF16, BF16, F32)

    experts = (w_gate.astype(BF16), w_up.astype(BF16), w_down.astype(BF16))
    h_lat, h_ctx = x, ctx
    for l in range(depth):
        last = l == depth - 1
        transposed = l % 2 == 1
        tm = SUBLANES * (t // GRID_W) if transposed else min(TOKEN_TILE, t)
        tm_in = min(IN_TILE, t)
        m_lat = [mod[l, :bsz, k * d:(k + 1) * d].reshape(bsz, 1, d) for k in range(6)]
        m_ctx = [mod[l, bsz:bsz + 1, k * d:(k + 1) * d].reshape(1, 1, d) for k in range(6)]
        wl = w_in[l]
        w_tok = jnp.concatenate([wl[:, :c_q], wl[:, c_k:c_v], wl[:, c_o:c_g], wl[:, c_l:]], axis=1).astype(BF16)
        w_chn = jnp.concatenate([wl[:, c_q:c_k], wl[:, c_v:c_o], wl[:, c_g:c_l]], axis=1).T.astype(BF16)
        gate_bias = jnp.broadcast_to(
            jnp.concatenate([mlstm_b_i[l].reshape(-1), mlstm_b_f[l].reshape(-1)])[:, None], (2 * ngate, MLSTM_CHUNK))
        wax = jnp.stack([jnp.concatenate([_block_diag(lru_w_a[l, dd]), _block_diag(lru_w_x[l, dd])], axis=1)
                         for dd in range(N_DIR)]).astype(BF16)
        bax = jnp.concatenate([lru_b_a[l], lru_b_x[l]], axis=-1).reshape(N_DIR, 1, 2 * dl)
        lam = lru_lambda[l].reshape(N_DIR, 1, dl)
        lcb = lru_conv_b[l].reshape(N_DIR, 1, dl)
        wr = jnp.zeros((d, ROUTER_LANES), F32).at[:, :N_GROUPS].set(w_rg[l]).at[:, N_GROUPS:N_GROUPS + N_EXPERTS].set(w_re[l])
        wrh, wrl = _split_hi_lo(wr)
        br = jnp.zeros((1, ROUTER_LANES), F32).at[0, :N_GROUPS].set(b_rg[l]).at[0, N_GROUPS:N_GROUPS + N_EXPERTS].set(b_re[l])
        wo = w_out[l].astype(BF16)

        proj = functools.partial(_in_proj, norm_g=norm1_g[l], w=w_tok, wt=w_chn, gate_bias=gate_bias, splits=splits,
                                 out_dtypes=out_dtypes, splits_t=splits_t, out_dtypes_t=out_dtypes_t)
        cv_l, k_l, o_l, lru_l, qt_l, vt_l, gt_l = proj(h_lat, m_lat[0], m_lat[1], tm=tm_in, transposed=transposed)
        cv_c, k_c, o_c, lru_c, qt_c, vt_c, gt_c = proj(h_ctx, m_ctx[0], m_ctx[1], tm=tm_c, transposed=False)

        a_l = _conformer_conv(cv_l, conv_w[l], conv_b[l], conv_ln_g[l], conv_ln_b[l], tm_in)
        (mcf, mcb, mlf, mlb), (rcf, rcb, rlf, rlb) = _recurrent_mixers(
            (qt_c, k_c, vt_c, gt_c), (qt_l, k_l, vt_l, gt_l), lru_c, lru_l, lru_conv_w[l], lcb, wax, bax, lam, tb)

        h_lat, vx_l, cnt_l = _out_proj(a_l, mlf, mlb, o_l, rlf, rlb, lru_l, h_lat, m_lat[2], m_lat[3], m_lat[4],
                                       norm2_g[l], mlstm_norm_g[l], wo, wrh, wrl, br, tm, transposed)
        h_lat = _moe_layer(vx_l, cnt_l, experts, l, h_lat, m_lat[5], final_g, last, transposed)
        if not last:
            a_c = _conformer_conv(cv_c, conv_w[l], conv_b[l], conv_ln_g[l], conv_ln_b[l], tm_c)
            h_ctx, vx_c, cnt_c = _out_proj(a_c, mcf, mcb, o_c, rcf, rcb, lru_c, h_ctx, m_ctx[2], m_ctx[3], m_ctx[4],
                                           norm2_g[l], mlstm_norm_g[l], wo, wrh, wrl, br, tm_c, False)
            h_ctx = _moe_layer(vx_c, cnt_c, experts, l, h_ctx, m_ctx[5], final_g, False, False)
    return h_lat
```

```python
import functools
import math

import jax
import jax.numpy as jnp
from jax import lax
from jax.experimental import pallas as pl
from jax.experimental.pallas import tpu as pltpu

EPS = 1e-6
GRID_W = 64
CONV_WIDTH = 31
CONV_HALO = 16
MLSTM_HEADS = 4
MLSTM_CHUNK = 128
LRU_CONV_WIDTH = 4
LRU_C = 8.0
N_DIR = 2
N_GROUPS = 4
EXPERTS_PER_GROUP = 4
N_EXPERTS = N_GROUPS * EXPERTS_PER_GROUP
N_PAIRS = EXPERTS_PER_GROUP * (EXPERTS_PER_GROUP - 1) // 2
N_CLASSES = N_GROUPS * N_PAIRS
ROUTER_LANES = 128
INFO_CLASS, INFO_RANK, INFO_W_LO, INFO_W_HI = 0, 1, 2, 3
SUBLANES = 8
LOG2E = 1.4426950408889634

VMEM_LIMIT = 56 * 1024 * 1024
IN_TILE = 1024
TOKEN_TILE = 512
SCAN_BLOCK = 512
MOE_TILE = 512
MOE_TILE_MIN = 128
DISPATCH_TILE = 1024
DMA_UNROLL = 8

F32 = jnp.float32
BF16 = jnp.bfloat16


def _cparams(sem):
    return pltpu.CompilerParams(dimension_semantics=sem, vmem_limit_bytes=VMEM_LIMIT)


def _sigmoid(x):
    return jax.nn.sigmoid(x)


def _log_sigmoid(x):
    return jnp.minimum(x, 0.0) - jnp.log1p(jnp.exp(-jnp.abs(x)))


def _softplus(x):
    return jnp.maximum(x, 0.0) + jnp.log1p(jnp.exp(-jnp.abs(x)))


def _gelu_tanh(x):
    return x * (0.5 * (1.0 + jnp.tanh(0.7978845608028654 * (x + 0.044715 * (x * x * x)))))


def _dot(a, b):
    return jnp.dot(a, b, preferred_element_type=F32)


def _split_hi_lo(x):
    hi = x.astype(BF16)
    lo = (x - hi.astype(F32)).astype(BF16)
    return hi, lo


def _load_tile(ref, transposed):
    if not transposed:
        return ref[0]
    return jnp.concatenate([ref[0, :, w, :] for w in range(ref.shape[2])], axis=0)


def _tile_spec(t, tm, d, transposed, to_natural=False):
    if not transposed:
        return pl.BlockSpec((1, tm, d), lambda b, j: (b, j, 0))
    major = GRID_W if to_natural else t // GRID_W
    assert tm % (SUBLANES * major) == 0
    return pl.BlockSpec((1, major, tm // major, d), lambda b, j: (b, 0, j, 0))


def _tile_view(a, transposed, to_natural=False):
    if not transposed:
        return a
    b, t, d = a.shape
    return a.reshape(b, GRID_W, t // GRID_W, d) if to_natural else a.reshape(b, t // GRID_W, GRID_W, d)


def _mod_kernel(c_ref, w_ref, b_ref, o_ref):
    c = c_ref[...]
    s = (c * _sigmoid(c)).astype(BF16)
    o_ref[0] = _dot(s, w_ref[0].astype(BF16)) + b_ref[0]


def _modulation(cvec, w_mod, b_mod):
    nl, d, d6 = w_mod.shape
    rp = cvec.shape[0]
    tn = d6 // 4
    return pl.pallas_call(
        _mod_kernel,
        grid=(nl, d6 // tn),
        in_specs=[pl.BlockSpec((rp, d), lambda l, j: (0, 0)),
                  pl.BlockSpec((1, d, tn), lambda l, j: (l, 0, j)),
                  pl.BlockSpec((1, 1, tn), lambda l, j: (l, 0, j))],
        out_specs=pl.BlockSpec((1, rp, tn), lambda l, j: (l, 0, j)),
        out_shape=jax.ShapeDtypeStruct((nl, rp, d6), F32),
        compiler_params=_cparams(("arbitrary", "arbitrary")),
        name="modulation",
    )(cvec, w_mod, b_mod.reshape(nl, 1, d6))


def _chunk_scan(x, op, reverse):
    L = MLSTM_CHUNK
    n = x.shape[1]
    pos = lax.broadcasted_iota(jnp.int32, x.shape, 1) & (L - 1)
    sh = 1
    while sh < L:
        if reverse:
            xs, ok = pltpu.roll(x, n - sh, 1), pos < L - sh
        else:
            xs, ok = pltpu.roll(x, sh, 1), pos >= sh
        x = jnp.where(ok, op(x, xs), x)
        sh *= 2
    return x


def _gate_scans(g):
    nrow = g.shape[0] // 2
    fwd_row = lax.broadcasted_iota(jnp.int32, (nrow, g.shape[1]), 0) < nrow // N_DIR
    ig2 = g[:nrow] * LOG2E
    lf2 = _log_sigmoid(g[nrow:]) * LOG2E
    b2 = jnp.where(fwd_row, _chunk_scan(lf2, jnp.add, False), _chunk_scan(lf2, jnp.add, True))
    c2 = ig2 - b2
    cm2 = jnp.where(fwd_row, _chunk_scan(c2, jnp.maximum, False), _chunk_scan(c2, jnp.maximum, True))
    return jnp.concatenate([ig2, b2, cm2], axis=0)


def _in_kernel(h_ref, shift_ref, scale_ref, g_ref, w_ref, wt_ref, gbias_ref, *outs, transposed, splits, splits_t):
    h = _load_tile(h_ref, transposed)
    ms = jnp.mean(h * h, axis=-1, keepdims=True)
    u = (h * lax.rsqrt(ms + EPS)) * g_ref[...]
    u = (u * (1.0 + scale_ref[0]) + shift_ref[0]).astype(BF16)
    c0 = 0
    for o_ref, width in zip(outs, splits):
        o_ref[0] = _dot(u, w_ref[:, c0:c0 + width]).astype(o_ref.dtype)
        c0 += width
    r0 = 0
    outs_t = outs[len(splits):]
    for o_ref, width in zip(outs_t[:-1], splits_t[:-1]):
        o_ref[0] = lax.dot_general(wt_ref[r0:r0 + width, :], u, (((1,), (1,)), ((), ())),
                                   preferred_element_type=F32).astype(o_ref.dtype)
        r0 += width
    gates = lax.dot_general(wt_ref[r0:r0 + splits_t[-1], :], u, (((1,), (1,)), ((), ())), preferred_element_type=F32)
    tm = gates.shape[1]
    outs_t[-1][0] = _gate_scans(gates + jnp.concatenate([gbias_ref[...]] * (tm // MLSTM_CHUNK), axis=1))


def _in_proj(h, shift, scale, norm_g, w, wt, gate_bias, splits, out_dtypes, splits_t, out_dtypes_t, tm, transposed):
    b, t, d = h.shape
    per_batch = shift.shape[0] == b
    mod_spec = pl.BlockSpec((1, 1, d), (lambda bi, j: (bi, 0, 0)) if per_batch else (lambda bi, j: (0, 0, 0)))
    rows_t = splits_t[:-1] + (splits_t[-1] // 2 * 3,)
    return pl.pallas_call(
        functools.partial(_in_kernel, transposed=transposed, splits=splits, splits_t=splits_t),
        grid=(b, t // tm),
        in_specs=[_tile_spec(t, tm, d, transposed), mod_spec, mod_spec, pl.BlockSpec((1, d), lambda bi, j: (0, 0)),
                  pl.BlockSpec(w.shape, lambda bi, j: (0, 0)), pl.BlockSpec(wt.shape, lambda bi, j: (0, 0)),
                  pl.BlockSpec(gate_bias.shape, lambda bi, j: (0, 0))],
        out_specs=([pl.BlockSpec((1, tm, width), lambda bi, j: (bi, j, 0)) for width in splits]
                   + [pl.BlockSpec((1, width, tm), lambda bi, j: (bi, 0, j)) for width in rows_t]),
        out_shape=([jax.ShapeDtypeStruct((b, t, width), dt) for width, dt in zip(splits, out_dtypes)]
                   + [jax.ShapeDtypeStruct((b, width, t), dt) for width, dt in zip(rows_t, out_dtypes_t)]),
        compiler_params=_cparams(("arbitrary", "arbitrary")),
        name="in_proj",
    )(_tile_view(h, transposed), shift, scale, norm_g.reshape(1, d), w, wt, gate_bias)


def _conv_kernel(prev_ref, cur_ref, next_ref, w_ref, b_ref, lng_ref, lnb_ref, o_ref, *, tb, nblk, dc):
    j = pl.program_id(1)

    def glu(x):
        return x[:, :dc] * _sigmoid(x[:, dc:])

    up = jnp.where(j > 0, glu(prev_ref[0]), 0.0)
    un = jnp.where(j < nblk - 1, glu(next_ref[0]), 0.0)
    ext = jnp.concatenate([up, glu(cur_ref[0]), un], axis=0)
    base = CONV_HALO - CONV_WIDTH // 2
    nrows = tb + 2 * CONV_HALO
    acc = jnp.zeros((tb, dc), F32)
    for b in range(SUBLANES):
        shifted = ext if b == 0 else pltpu.roll(ext, nrows - b, 0)
        for a in range((base + CONV_WIDTH - 1 - b) // SUBLANES + 1):
            k = SUBLANES * a + b - base
            if 0 <= k < CONV_WIDTH:
                acc = acc + w_ref[k:k + 1, :] * shifted[SUBLANES * a:SUBLANES * a + tb, :]
    acc = acc + b_ref[...]
    mu = jnp.mean(acc, axis=-1, keepdims=True)
    xc = acc - mu
    y = xc * lax.rsqrt(jnp.mean(xc * xc, axis=-1, keepdims=True) + EPS) * lng_ref[...] + lnb_ref[...]
    o_ref[0] = (y * _sigmoid(y)).astype(o_ref.dtype)


def _conformer_conv(cv, w, b, ln_g, ln_b, tb):
    bsz, t, c2 = cv.shape
    dc = c2 // 2
    nblk = t // tb
    hb = tb // CONV_HALO
    nh = t // CONV_HALO
    wp = jnp.zeros((CONV_WIDTH + 1, dc), F32).at[:CONV_WIDTH].set(w)
    vec = pl.BlockSpec((1, dc), lambda bi, j: (0, 0))
    return pl.pallas_call(
        functools.partial(_conv_kernel, tb=tb, nblk=nblk, dc=dc),
        grid=(bsz, nblk),
        in_specs=[pl.BlockSpec((1, CONV_HALO, c2), lambda bi, j: (bi, jnp.maximum(j * hb - 1, 0), 0)),
                  pl.BlockSpec((1, tb, c2), lambda bi, j: (bi, j, 0)),
                  pl.BlockSpec((1, CONV_HALO, c2), lambda bi, j: (bi, jnp.minimum((j + 1) * hb, nh - 1), 0)),
                  pl.BlockSpec((CONV_WIDTH + 1, dc), lambda bi, j: (0, 0)), vec, vec, vec],
        out_specs=pl.BlockSpec((1, tb, dc), lambda bi, j: (bi, j, 0)),
        out_shape=jax.ShapeDtypeStruct((bsz, t, dc), BF16),
        compiler_params=_cparams(("arbitrary", "arbitrary")),
        name="conformer_conv",
    )(cv, cv, cv, wp, b.reshape(1, dc), ln_g.reshape(1, dc), ln_b.reshape(1, dc))


def _mlstm_block(qt_ref, k_ref, vt_ref, gt_ref, out_ref, c_ref, n_ref, m_ref, d, nchunks, hd):
    L = MLSTM_CHUNK
    nh = MLSTM_HEADS
    assert hd == L
    nrow = N_DIR * nh
    log2_scale = math.log2(hd ** -0.5)
    ig2, b2, cm2 = gt_ref[0, 0:nrow], gt_ref[0, nrow:2 * nrow], gt_ref[0, 2 * nrow:3 * nrow]
    c2 = ig2 - b2
    row_s = lax.broadcasted_iota(jnp.int32, (L, L), 0)
    col_t = lax.broadcasted_iota(jnp.int32, (L, L), 1)
    mask = (row_s <= col_t) if d == 0 else (row_s >= col_t)
    ones16 = jnp.ones((2 * SUBLANES, L), BF16)
    for ci in (range(nchunks) if d == 0 else range(nchunks - 1, -1, -1)):
        sl = slice(ci * L, (ci + 1) * L)
        b2c, c2c, ig2c = b2[:, sl], c2[:, sl], ig2[:, sl]
        m2 = m_ref[d]
        a2 = -jnp.maximum(m2, cm2[:, sl])
        inter = jnp.exp2(m2 + a2)
        edn = jnp.exp2(a2 - b2c)
        b_last = jnp.broadcast_to(b2c[:, L - 1:L] if d == 0 else b2c[:, 0:1], (nrow, L))
        logw2 = b_last - b2c + ig2c
        m2_new = jnp.maximum(b_last + m2, jnp.broadcast_to(jnp.max(logw2, axis=-1, keepdims=True), (nrow, L)))
        wgt = jnp.exp2(logw2 - m2_new + log2_scale)
        decay = jnp.exp2(b_last + m2 - m2_new)
        m_ref[d] = m2_new
        c_cols = (c2c + log2_scale).T
        for h in range(nh):
            r = d * nh + h
            e = c_cols[:, r:r + 1] + a2[r:r + 1]
            p = jnp.where(mask, jnp.exp2(e), 0.0)
            qt = qt_ref[0, h * hd:(h + 1) * hd, sl]
            k = k_ref[0, ci * L:(ci + 1) * L, h * hd:(h + 1) * hd]
            vt = vt_ref[0, h * hd:(h + 1) * hd, sl]
            ct = c_ref[r]
            n16 = n_ref[r]
            st = _dot(k, qt) * p
            nq = _dot(n16.astype(BF16), qt)
            nd = _dot(vt, st.astype(BF16))
            inter_r = inter[r:r + 1]
            num = nd + inter_r * _dot(ct.astype(BF16), qt)
            den = jnp.sum(st, axis=0, keepdims=True) + inter_r * nq[0:1]
            ht = num * (1.0 / jnp.maximum(jnp.abs(den), edn[r:r + 1]))
            out_ref[0, ci * L:(ci + 1) * L, h * hd:(h + 1) * hd] = ht.T
            w_r = wgt[r:r + 1]
            vw = (vt.astype(F32) * w_r).astype(BF16)
            upd = _dot(jnp.concatenate([vw, ones16 * w_r.astype(BF16)], axis=0), k)
            dec = decay[r:r + 1]
            c_ref[r] = dec * ct + upd[:hd]
            n_ref[r] = dec * n16 + upd[hd:]


def _mlstm_steps(qtc_ref, kc_ref, vtc_ref, gtc_ref, qtf_ref, kf_ref, vtf_ref, gtf_ref,
                 qtb_ref, kb_ref, vtb_ref, gtb_ref,
                 hcf_ref, hcb_ref, hf_ref, hb_ref, c_ref, n_ref, m_ref, *, nc_ctx, nc_lat, hd):
    state = (c_ref, n_ref, m_ref)

    def first():
        c_ref[...] = jnp.zeros_like(c_ref)
        n_ref[...] = jnp.zeros_like(n_ref)
        m_ref[...] = jnp.zeros_like(m_ref)
        _mlstm_block(qtc_ref, kc_ref, vtc_ref, gtc_ref, hcf_ref, *state, 0, nc_ctx, hd)
        _mlstm_block(qtc_ref, kc_ref, vtc_ref, gtc_ref, hcb_ref, *state, 1, nc_ctx, hd)

    def rest():
        _mlstm_block(qtf_ref, kf_ref, vtf_ref, gtf_ref, hf_ref, *state, 0, nc_lat, hd)
        _mlstm_block(qtb_ref, kb_ref, vtb_ref, gtb_ref, hb_ref, *state, 1, nc_lat, hd)

    return first, rest


def _lru_block(x, halo, d, cw, cb, wax, bax, lam, h0):
    tb, dl = x.shape
    kw = LRU_CONV_WIDTH
    if d == 0:
        ext = jnp.concatenate([halo, x], axis=0)
        taps = [ext[SUBLANES - (kw - 1) + j:SUBLANES - (kw - 1) + j + tb] for j in range(kw)]
    else:
        ext = jnp.concatenate([x, halo], axis=0)
        taps = [ext[kw - 1 - j:kw - 1 - j + tb] for j in range(kw)]
    xc = cb
    for j in range(kw):
        xc = xc + cw[j:j + 1, :] * taps[j]
    ri = _sigmoid(_dot(xc.astype(BF16), wax) + bax)
    r, i = ri[:, :dl], ri[:, dl:]
    log_a = (-LRU_C * r) * _softplus(-lam)
    a = jnp.exp(log_a)
    th = jnp.tanh(log_a)
    u = jnp.sqrt(-2.0 * th / (1.0 - th)) * (i * xc)
    ngrp = tb // SUBLANES
    a = a.reshape(ngrp, SUBLANES, dl)
    u = u.reshape(ngrp, SUBLANES, dl)
    sub = lax.broadcasted_iota(jnp.int32, a.shape, 1)
    for sh in (1, 2, 4):
        if d == 0:
            a_s, u_s, msk = pltpu.roll(a, sh, 1), pltpu.roll(u, sh, 1), sub >= sh
        else:
            a_s, u_s, msk = pltpu.roll(a, SUBLANES - sh, 1), pltpu.roll(u, SUBLANES - sh, 1), sub < SUBLANES - sh
        u = jnp.where(msk, a * u_s + u, u)
        a = jnp.where(msk, a * a_s, a)
    a = a.reshape(tb, dl)
    u = u.reshape(tb, dl)
    outs = [None] * ngrp
    carry = h0
    for j in (range(ngrp) if d == 0 else range(ngrp - 1, -1, -1)):
        hj = u[j * SUBLANES:(j + 1) * SUBLANES] + a[j * SUBLANES:(j + 1) * SUBLANES] * carry
        outs[j] = hj
        carry = hj[SUBLANES - 1:SUBLANES] if d == 0 else hj[0:1]
    return jnp.concatenate(outs, axis=0), carry


def _lru_steps(xc_ref, xf_ref, xb_ref, cw_ref, cb_ref, wax_ref, bax_ref, lam_ref,
               hcf_ref, hcb_ref, hf_ref, hb_ref, hcar_ref, halo_ref, *, s):
    def params(d):
        return cw_ref[d], cb_ref[d], wax_ref[d], bax_ref[d], lam_ref[d]

    def first():
        x = xc_ref[0]
        zero_halo = jnp.zeros((SUBLANES, x.shape[1]), F32)
        zero_h = jnp.zeros((1, x.shape[1]), F32)
        for d, o_ref in ((0, hcf_ref), (1, hcb_ref)):
            h, carry = _lru_block(x, zero_halo, d, *params(d), zero_h)
            o_ref[0] = h
            hcar_ref[d] = jnp.broadcast_to(carry, hcar_ref.shape[1:])

    def rest():
        for d, x_ref, o_ref in ((0, xf_ref, hf_ref), (1, xb_ref, hb_ref)):
            x = x_ref[0]
            halo = jnp.where(s > 1, halo_ref[d], 0.0)
            h, carry = _lru_block(x, halo, d, *params(d), hcar_ref[d][0:1])
            o_ref[0] = h
            hcar_ref[d] = jnp.broadcast_to(carry, hcar_ref.shape[1:])
            halo_ref[d] = x[x.shape[0] - SUBLANES:] if d == 0 else x[:SUBLANES]

    return first, rest


N_MLSTM_IN, N_LRU_IN, N_MIX_OUT, N_MLSTM_SCRATCH = 12, 8, 4, 3


def _mixer_kernel(*refs, nc_ctx, nc_lat, hd):
    m_in, refs = refs[:N_MLSTM_IN], refs[N_MLSTM_IN:]
    l_in, refs = refs[:N_LRU_IN], refs[N_LRU_IN:]
    m_out, l_out, refs = refs[:N_MIX_OUT], refs[N_MIX_OUT:2 * N_MIX_OUT], refs[2 * N_MIX_OUT:]
    m_scratch, l_scratch = refs[:N_MLSTM_SCRATCH], refs[N_MLSTM_SCRATCH:]
    s = pl.program_id(1)
    m_first, m_rest = _mlstm_steps(*m_in, *m_out, *m_scratch, nc_ctx=nc_ctx, nc_lat=nc_lat, hd=hd)
    l_first, l_rest = _lru_steps(*l_in, *l_out, *l_scratch, s=s)

    @pl.when(s == 0)
    def _():
        m_first()
        l_first()

    @pl.when(s > 0)
    def _():
        m_rest()
        l_rest()


def _recurrent_mixers(ctx_in, lat_in, lru_c, lru_l, cw, cb, wax, bax, lam, tb):
    bsz, tc, dm = ctx_in[1].shape
    t = lat_in[1].shape[1]
    ng = lat_in[3].shape[1]
    dl = lru_l.shape[2] // 2
    hd = dm // MLSTM_HEADS
    nb = t // tb
    nstate = N_DIR * MLSTM_HEADS

    def fwd(s):
        return jnp.maximum(s - 1, 0)

    def bwd(s):
        return nb - 1 - jnp.maximum(s - 1, 0)

    def first(s):
        return 0

    def rows(tlen, width, blk):
        return pl.BlockSpec((1, tlen, width), lambda bi, s: (bi, blk(s), 0))

    def cols(tlen, width, blk):
        return pl.BlockSpec((1, width, tlen), lambda bi, s: (bi, 0, blk(s)))

    def mlstm_in(tlen, blk):
        return [cols(tlen, dm, blk), rows(tlen, dm, blk), cols(tlen, dm, blk), cols(tlen, ng, blk)]

    def whole(a):
        return pl.BlockSpec(a.shape, lambda bi, s: (0,) * a.ndim)

    def outs(width):
        return [rows(tc, width, first), rows(tc, width, first), rows(tb, width, fwd), rows(tb, width, bwd)]

    def out_shapes(width):
        return [jax.ShapeDtypeStruct((bsz, n, width), F32) for n in (tc, tc, t, t)]

    res = pl.pallas_call(
        functools.partial(_mixer_kernel, nc_ctx=tc // MLSTM_CHUNK, nc_lat=tb // MLSTM_CHUNK, hd=hd),
        grid=(bsz, nb + 1),
        in_specs=(mlstm_in(tc, first) + mlstm_in(tb, fwd) + mlstm_in(tb, bwd)
                  + [rows(tc, dl, first), rows(tb, dl, fwd), rows(tb, dl, bwd)]
                  + [whole(a) for a in (cw, cb, wax, bax, lam)]),
        out_specs=outs(dm) + outs(dl),
        out_shape=out_shapes(dm) + out_shapes(dl),
        scratch_shapes=[pltpu.VMEM((nstate, hd, hd), F32), pltpu.VMEM((nstate, 2 * SUBLANES, hd), F32),
                        pltpu.VMEM((N_DIR, nstate, MLSTM_CHUNK), F32),
                        pltpu.VMEM((N_DIR, SUBLANES, dl), F32), pltpu.VMEM((N_DIR, SUBLANES, dl), F32)],
        compiler_params=_cparams(("arbitrary", "arbitrary")),
        name="recurrent_mixers",
    )(*ctx_in, *lat_in, *lat_in, lru_c, lru_l, lru_l, cw, cb, wax, bax, lam)
    return res[:N_MIX_OUT], res[N_MIX_OUT:]


def _route(logits):
    ng, ne = N_GROUPS, EXPERTS_PER_GROUP
    lane = lax.broadcasted_iota(jnp.int32, logits.shape, 1).astype(F32)
    big = float(ROUTER_LANES)
    is_g = lane < ng
    gl = jnp.where(is_g, logits, -jnp.inf)
    gmax = jnp.max(gl, axis=-1, keepdims=True)
    g_sel = jnp.min(jnp.where(gl == gmax, lane, big), axis=-1, keepdims=True)
    p_g = 1.0 / jnp.sum(jnp.where(is_g, jnp.exp(logits - gmax), 0.0), axis=-1, keepdims=True)
    lo = ng + ne * g_sel
    el = jnp.where((lane >= lo) & (lane < lo + ne), logits, -jnp.inf)
    v1 = jnp.max(el, axis=-1, keepdims=True)
    i1 = jnp.min(jnp.where(el == v1, lane, big), axis=-1, keepdims=True)
    el2 = jnp.where(lane == i1, -jnp.inf, el)
    v2 = jnp.max(el2, axis=-1, keepdims=True)
    i2 = jnp.min(jnp.where(el2 == v2, lane, big), axis=-1, keepdims=True)
    e2 = jnp.exp(v2 - v1)
    w1 = p_g / (1.0 + e2)
    w2 = p_g * e2 / (1.0 + e2)
    first_lower = i1 < i2
    e_lo = jnp.minimum(i1, i2) - lo
    e_hi = jnp.maximum(i1, i2) - lo
    pair = e_lo * (2 * ne - 1 - e_lo) * 0.5 + (e_hi - e_lo - 1.0)
    return g_sel * N_PAIRS + pair, jnp.where(first_lower, w1, w2), jnp.where(first_lower, w2, w1)


def _out_kernel(a_ref, mf_ref, mb_ref, o_ref, lf_ref, lb_ref, lg_ref, h_ref, g1_ref, sh2_ref, sc2_ref,
                n2g_ref, mng_ref, wout_ref, wrh_ref, wrl_ref, br_ref,
                hnew_ref, vx_ref, cnt_ref, info_ref, run_ref, *, transposed, d, hd):
    @pl.when((pl.program_id(0) == 0) & (pl.program_id(1) == 0))
    def _():
        run_ref[...] = jnp.zeros_like(run_ref)

    dc = a_ref.shape[2]
    dm = mf_ref.shape[2]
    mh = mf_ref[0] + mb_ref[0]
    parts = []
    for h in range(dm // hd):
        x = mh[:, h * hd:(h + 1) * hd]
        xc = x - jnp.mean(x, axis=-1, keepdims=True)
        parts.append(xc * lax.rsqrt(jnp.mean(xc * xc, axis=-1, keepdims=True) + EPS) * mng_ref[:, h * hd:(h + 1) * hd])
    m_out = (_sigmoid(o_ref[0]) * jnp.concatenate(parts, axis=-1)).astype(BF16)
    r_out = ((lf_ref[0] + lb_ref[0]) * _gelu_tanh(lg_ref[0])).astype(BF16)
    y = (_dot(a_ref[0], wout_ref[0:dc, :]) + _dot(m_out, wout_ref[dc:dc + dm, :])
         + _dot(r_out, wout_ref[dc + dm:, :]))
    hn = _load_tile(h_ref, transposed) + g1_ref[0] * y
    hnew_ref[0] = hn
    ms = jnp.mean(hn * hn, axis=-1, keepdims=True)
    v = (hn * lax.rsqrt(ms + EPS)) * n2g_ref[...]
    v = v * (1.0 + sc2_ref[0]) + sh2_ref[0]
    vh, vl = _split_hi_lo(v)
    logits = _dot(vh, wrh_ref[...]) + _dot(vl, wrh_ref[...]) + _dot(vh, wrl_ref[...]) + br_ref[...]
    cls, w_lo, w_hi = _route(logits)
    tm = v.shape[0]
    lane = lax.broadcasted_iota(jnp.int32, (tm, ROUTER_LANES), 1).astype(F32)
    onehot = jnp.where(lane == cls, 1.0, 0.0)
    row = lax.broadcasted_iota(jnp.int32, (tm, tm), 0)
    col = lax.broadcasted_iota(jnp.int32, (tm, tm), 1)
    before = _dot(jnp.where(col < row, 1.0, 0.0).astype(BF16), onehot.astype(BF16)) + run_ref[0:1, :]
    rank = jnp.sum(onehot * before, axis=-1, keepdims=True)
    run_ref[...] = run_ref[...] + jnp.sum(onehot, axis=0, keepdims=True)
    cnt_ref[...] = run_ref[...]
    info = (jnp.where(lane == INFO_CLASS, cls, 0.0) + jnp.where(lane == INFO_RANK, rank, 0.0)
            + jnp.where(lane == INFO_W_LO, w_lo, 0.0) + jnp.where(lane == INFO_W_HI, w_hi, 0.0))
    vx_ref[0, :, :d] = v
    vx_ref[0, :, d:] = info
    info_ref[0] = info.T[:SUBLANES]


def _out_proj(a, mf, mb, o, lf, lb, lru, h, g1, sh2, sc2, n2g, mng, w_out, wrh, wrl, br, tm, transposed):
    bsz, t, d = h.shape
    dc, dm, dl = a.shape[2], mf.shape[2], lf.shape[2]
    per_batch = g1.shape[0] == bsz
    mod_spec = pl.BlockSpec((1, 1, d), (lambda bi, j: (bi, 0, 0)) if per_batch else (lambda bi, j: (0, 0, 0)))

    def scan(width, blk=0):
        return pl.BlockSpec((1, tm, width), lambda bi, j: (bi, j, blk))

    def whole(x):
        return pl.BlockSpec(x.shape, lambda bi, j: (0,) * x.ndim)

    dx = d + ROUTER_LANES
    n2g, mng = n2g.reshape(1, d), mng.reshape(1, dm)
    hv = _tile_view(h, transposed)
    hnew, vx, cnt, info_t = pl.pallas_call(
        functools.partial(_out_kernel, transposed=transposed, d=d, hd=dm // MLSTM_HEADS),
        grid=(bsz, t // tm),
        in_specs=[scan(dc), scan(dm), scan(dm), scan(dm), scan(dl), scan(dl), scan(dl, 1),
                  _tile_spec(t, tm, d, transposed),
                  mod_spec, mod_spec, mod_spec, whole(n2g), whole(mng), whole(w_out), whole(wrh), whole(wrl), whole(br)],
        out_specs=[scan(d), scan(dx), pl.BlockSpec((SUBLANES, ROUTER_LANES), lambda bi, j: (0, 0)),
                   pl.BlockSpec((1, SUBLANES, tm), lambda bi, j: (bi, 0, j))],
        out_shape=[jax.ShapeDtypeStruct((bsz, t, d), F32), jax.ShapeDtypeStruct((bsz, t, dx), F32),
                   jax.ShapeDtypeStruct((SUBLANES, ROUTER_LANES), F32), jax.ShapeDtypeStruct((bsz, SUBLANES, t), F32)],
        scratch_shapes=[pltpu.VMEM((SUBLANES, ROUTER_LANES), F32)],
        compiler_params=_cparams(("arbitrary", "arbitrary")),
        name="out_proj",
    )(a, mf, mb, o, lf, lb, lru, hv, g1, sh2, sc2, n2g, mng, w_out, wrh, wrl, br)
    return hnew, vx, (cnt, info_t)


def _routing_tables(route, tmo, n_max):
    cnt, info_t = route
    cls = info_t[:, INFO_CLASS, :].reshape(-1).astype(jnp.int32)
    rank = info_t[:, INFO_RANK, :].reshape(-1).astype(jnp.int32)
    counts = cnt[0, :N_CLASSES].astype(jnp.int32)
    tiles = (counts + tmo - 1) // tmo
    tile_end = jnp.cumsum(tiles)
    tile_start = tile_end - tiles
    n_real = tile_end[N_CLASSES - 1]
    slot = jnp.take(tile_start * tmo, cls) + rank
    blk = jnp.minimum(jnp.arange(n_max, dtype=jnp.int32), n_real - 1)
    tile_cls = jnp.minimum(jnp.sum((tile_end[None, :] <= blk[:, None]).astype(jnp.int32), axis=1), N_CLASSES - 1)
    pair_lo = jnp.array([a for a in range(EXPERTS_PER_GROUP) for b in range(a + 1, EXPERTS_PER_GROUP)], jnp.int32)
    pair_hi = jnp.array([b for a in range(EXPERTS_PER_GROUP) for b in range(a + 1, EXPERTS_PER_GROUP)], jnp.int32)
    grp = tile_cls // N_PAIRS
    e_lo = grp * EXPERTS_PER_GROUP + jnp.take(pair_lo, tile_cls % N_PAIRS)
    e_hi = grp * EXPERTS_PER_GROUP + jnp.take(pair_hi, tile_cls % N_PAIRS)
    tail = n_real + jnp.arange(N_CLASSES, dtype=jnp.int32)
    fill = jnp.concatenate([jnp.where(tiles > 0, tile_end - 1, -1), jnp.where(tail < n_max, tail, -1)])
    return slot, blk, e_lo, e_hi, n_real.reshape(1), fill.astype(jnp.int32)


def _dispatch_kernel(fill_ref, slot_ref, vx_ref, xs_ref, zero_ref, zsem, sem):
    tm = vx_ref.shape[0] * SUBLANES
    tmo = zero_ref.shape[0]

    @pl.when(pl.program_id(0) == 0)
    def _():
        zero_ref[...] = jnp.zeros_like(zero_ref)

        def fill(j):
            return pltpu.make_async_copy(zero_ref, xs_ref.at[pl.ds(pl.multiple_of(fill_ref[j] * tmo, tmo), tmo)], zsem)

        for j in range(fill_ref.shape[0]):
            pl.when(fill_ref[j] >= 0)(lambda j=j: fill(j).start())
        for j in range(fill_ref.shape[0]):
            pl.when(fill_ref[j] >= 0)(lambda j=j: fill(j).wait())

    def issue(g, carry):
        for p in range(SUBLANES):
            dst = xs_ref.at[pl.ds(slot_ref[0, 0, g * SUBLANES + p], 1)]
            pltpu.make_async_copy(vx_ref.at[g, pl.ds(p, 1)], dst, sem).start()
        return carry

    lax.fori_loop(0, tm // SUBLANES, issue, 0, unroll=DMA_UNROLL)
    rows = xs_ref.at[pl.ds(0, tm)]
    pltpu.make_async_copy(rows, rows, sem).wait()


def _dispatch(vx, slot, fill, n_rows, tmo, tm):
    n, dx = vx.shape
    grid_spec = pltpu.PrefetchScalarGridSpec(
        num_scalar_prefetch=1,
        grid=(n // tm,),
        in_specs=[pl.BlockSpec((1, 1, tm), lambda i, pad: (i, 0, 0), memory_space=pltpu.SMEM),
                  pl.BlockSpec((tm // SUBLANES, SUBLANES, dx), lambda i, pad: (i, 0, 0))],
        out_specs=pl.BlockSpec(memory_space=pl.ANY),
        scratch_shapes=[pltpu.VMEM((tmo, dx), F32), pltpu.SemaphoreType.DMA(()), pltpu.SemaphoreType.DMA(())])
    return pl.pallas_call(
        _dispatch_kernel,
        grid_spec=grid_spec,
        out_shape=jax.ShapeDtypeStruct((n_rows, dx), F32),
        compiler_params=_cparams(("arbitrary",)),
        name="moe_dispatch",
    )(fill, slot.reshape(n // tm, 1, tm), vx.reshape(n // SUBLANES, SUBLANES, dx))


def _expert(x, wg_ref, wu_ref, wd_ref):
    g = _dot(x, wg_ref[0, 0])
    return _dot((g * _sigmoid(g) * _dot(x, wu_ref[0, 0])).astype(BF16), wd_ref[0, 0])


def _moe_kernel(blk_ref, elo_ref, ehi_ref, nreal_ref, xs_ref, wg_lo_ref, wu_lo_ref, wd_lo_ref,
                wg_hi_ref, wu_hi_ref, wd_hi_ref, ys_ref):
    del blk_ref, elo_ref, ehi_ref
    d = ys_ref.shape[1]

    @pl.when(pl.program_id(0) < nreal_ref[0])
    def _():
        x = xs_ref[:, :d].astype(BF16)
        w_lo = xs_ref[:, d + INFO_W_LO:d + INFO_W_LO + 1]
        w_hi = xs_ref[:, d + INFO_W_HI:d + INFO_W_HI + 1]
        ys_ref[...] = (w_lo * _expert(x, wg_lo_ref, wu_lo_ref, wd_lo_ref)
                       + w_hi * _expert(x, wg_hi_ref, wu_hi_ref, wd_hi_ref))

    @pl.when(pl.program_id(0) >= nreal_ref[0])
    def _():
        ys_ref[...] = jnp.zeros_like(ys_ref)


def _moe(xs, blk, e_lo, e_hi, n_real, experts, layer, tm):
    wg, wu, wd = experts
    dx = xs.shape[1]
    n_tiles = blk.shape[0]
    n_rows = n_tiles * tm
    _, _, d, de = wg.shape
    up = pl.BlockSpec((1, 1, d, de), lambda i, blk, lo, hi, nr: (layer, lo[i], 0, 0)), \
        pl.BlockSpec((1, 1, d, de), lambda i, blk, lo, hi, nr: (layer, hi[i], 0, 0))
    down = pl.BlockSpec((1, 1, de, d), lambda i, blk, lo, hi, nr: (layer, lo[i], 0, 0)), \
        pl.BlockSpec((1, 1, de, d), lambda i, blk, lo, hi, nr: (layer, hi[i], 0, 0))
    grid_spec = pltpu.PrefetchScalarGridSpec(
        num_scalar_prefetch=4,
        grid=(n_tiles,),
        in_specs=[pl.BlockSpec((tm, dx), lambda i, blk, lo, hi, nr: (blk[i], 0)),
                  up[0], up[0], down[0], up[1], up[1], down[1]],
        out_specs=pl.BlockSpec((tm, d), lambda i, blk, lo, hi, nr: (i, 0)))
    return pl.pallas_call(
        _moe_kernel,
        grid_spec=grid_spec,
        out_shape=jax.ShapeDtypeStruct((n_rows, d), F32),
        compiler_params=_cparams(("arbitrary",)),
        name="moe",
    )(blk, e_lo, e_hi, n_real, xs, wg, wu, wd, wg, wu, wd)


def _combine_kernel(slot_ref, slot_next_ref, ys_ref, h_ref, g2_ref, g_ref, o_ref, buf_ref, sem, *, final, transposed):
    i = pl.program_id(0) * pl.num_programs(1) + pl.program_id(1)
    n = pl.num_programs(0) * pl.num_programs(1)
    tm = o_ref.shape[1]

    def gather(s_ref, b):
        def issue(g, carry):
            for p in range(SUBLANES):
                src = ys_ref.at[pl.ds(s_ref[0, 0, g * SUBLANES + p], 1)]
                pltpu.make_async_copy(src, buf_ref.at[b, g, pl.ds(p, 1)], sem.at[b]).start()
            return carry
        lax.fori_loop(0, tm // SUBLANES, issue, 0, unroll=DMA_UNROLL)

    cur = i % 2

    @pl.when(i == 0)
    def _():
        gather(slot_ref, 0)

    @pl.when(i + 1 < n)
    def _():
        gather(slot_next_ref, 1 - cur)

    rows = ys_ref.at[pl.ds(0, tm)]
    pltpu.make_async_copy(rows, rows, sem.at[cur]).wait()
    h = _load_tile(h_ref, transposed) + g2_ref[0] * buf_ref[cur].reshape(tm, -1)
    if final:
        ms = jnp.mean(h * h, axis=-1, keepdims=True)
        h = (h * lax.rsqrt(ms + EPS)) * g_ref[...]
    o_ref[0] = h


def _combine(ys, slot, h, g2, g, tm, final, transposed):
    bsz, t, d = h.shape
    nj = t // tm
    nt = bsz * nj
    per_batch = g2.shape[0] == bsz
    slot3 = slot.reshape(nt, 1, tm)
    smem = functools.partial(pl.BlockSpec, (1, 1, tm), memory_space=pltpu.SMEM)
    return pl.pallas_call(
        functools.partial(_combine_kernel, final=final, transposed=transposed),
        grid=(bsz, nj),
        in_specs=[smem(lambda bi, j: (bi * nj + j, 0, 0)),
                  smem(lambda bi, j: (jnp.minimum(bi * nj + j + 1, nt - 1), 0, 0)),
                  pl.BlockSpec(memory_space=pl.ANY),
                  _tile_spec(t, tm, d, transposed, to_natural=True),
                  pl.BlockSpec((1, 1, d), (lambda bi, j: (bi, 0, 0)) if per_batch else (lambda bi, j: (0, 0, 0))),
                  pl.BlockSpec((1, d), lambda bi, j: (0, 0))],
        out_specs=pl.BlockSpec((1, tm, d), lambda bi, j: (bi, j, 0)),
        out_shape=jax.ShapeDtypeStruct((bsz, t, d), F32),
        scratch_shapes=[pltpu.VMEM((2, tm // SUBLANES, SUBLANES, d), F32), pltpu.SemaphoreType.DMA((2,))],
        compiler_params=_cparams(("arbitrary", "arbitrary")),
        name="moe_combine",
    )(slot3, slot3, ys, _tile_view(h, transposed, to_natural=True), g2, g.reshape(1, d))


def _moe_layer(vx, route, experts, layer, h, g2, g, final, transposed):
    bsz, t, d = h.shape
    n = bsz * t
    tmo = MOE_TILE
    while tmo > MOE_TILE_MIN and n < 2 * N_CLASSES * tmo:
        tmo //= 2
    n_max = n // tmo + N_CLASSES
    vx = vx.reshape(n, d + ROUTER_LANES)
    slot, blk, e_lo, e_hi, n_real, fill = _routing_tables(route, tmo, n_max)
    xs = _dispatch(vx, slot, fill, n_max * tmo, tmo, min(DISPATCH_TILE, n))
    ys = _moe(xs, blk, e_lo, e_hi, n_real, experts, layer, tmo)
    tm = min(DISPATCH_TILE, t)
    if transposed:
        slot = slot.reshape(bsz, GRID_W, t // GRID_W).transpose(0, 2, 1)
        tm = max(tm // (SUBLANES * GRID_W), 1) * SUBLANES * GRID_W
    return _combine(ys, slot, h, g2, g, tm, final, transposed)


def _block_diag(w):
    nh, bw, _ = w.shape
    eye = jnp.eye(nh, dtype=w.dtype)
    return (eye[:, None, :, None] * w[:, :, None, :]).reshape(nh * bw, nh * bw)


def kernel(x, c, ctx, c_ctx, w_mod, b_mod, norm1_g, norm2_g, w_in, conv_w, conv_b, conv_ln_g, conv_ln_b,
           mlstm_b_i, mlstm_b_f, mlstm_norm_g, lru_conv_w, lru_conv_b, lru_w_a, lru_b_a, lru_w_x, lru_b_x,
           lru_lambda, w_out, w_rg, b_rg, w_re, b_re, w_gate, w_up, w_down, final_g):
    bsz, t, d = x.shape
    tc = ctx.shape[1]
    depth = w_mod.shape[0]
    dc = conv_w.shape[2]
    dm = mlstm_norm_g.shape[1]
    dl = lru_lambda.shape[2]
    ngate = N_DIR * MLSTM_HEADS
    tm_c = min(TOKEN_TILE, tc)
    tb = min(SCAN_BLOCK, t)

    rp = -(-(bsz + 1) // SUBLANES) * SUBLANES
    cvec = jnp.zeros((rp, d), F32).at[:bsz].set(c).at[bsz].set(c_ctx)
    mod = _modulation(cvec, w_mod, b_mod)

    c_q = 2 * dc
    c_k = c_q + dm
    c_v = c_k + dm
    c_o = c_v + dm
    c_g = c_o + dm
    c_l = c_g + 2 * ngate
    splits = (2 * dc, dm, dm, 2 * dl)
    out_dtypes = (F32, BF16, F32, F32)
    splits_t = (dm, dm, 2 * ngate)
    out_dtypes_t = (BF16, BF16, F32)

    experts = (w_gate.astype(BF16), w_up.astype(BF16), w_down.astype(BF16))
    h_lat, h_ctx = x, ctx
    for l in range(depth):
        last = l == depth - 1
        transposed = l % 2 == 1
        tm = SUBLANES * (t // GRID_W) if transposed else min(TOKEN_TILE, t)
        tm_in = min(IN_TILE, t)
        m_lat = [mod[l, :bsz, k * d:(k + 1) * d].reshape(bsz, 1, d) for k in range(6)]
        m_ctx = [mod[l, bsz:bsz + 1, k * d:(k + 1) * d].reshape(1, 1, d) for k in range(6)]
        wl = w_in[l]
        w_tok = jnp.concatenate([wl[:, :c_q], wl[:, c_k:c_v], wl[:, c_o:c_g], wl[:, c_l:]], axis=1).astype(BF16)
        w_chn = jnp.concatenate([wl[:, c_q:c_k], wl[:, c_v:c_o], wl[:, c_g:c_l]], axis=1).T.astype(BF16)
        gate_bias = jnp.broadcast_to(
            jnp.concatenate([mlstm_b_i[l].reshape(-1), mlstm_b_f[l].reshape(-1)])[:, None], (2 * ngate, MLSTM_CHUNK))
        wax = jnp.stack([jnp.concatenate([_block_diag(lru_w_a[l, dd]), _block_diag(lru_w_x[l, dd])], axis=1)
                         for dd in range(N_DIR)]).astype(BF16)
        bax = jnp.concatenate([lru_b_a[l], lru_b_x[l]], axis=-1).reshape(N_DIR, 1, 2 * dl)
        lam = lru_lambda[l].reshape(N_DIR, 1, dl)
        lcb = lru_conv_b[l].reshape(N_DIR, 1, dl)
        wr = jnp.zeros((d, ROUTER_LANES), F32).at[:, :N_GROUPS].set(w_rg[l]).at[:, N_GROUPS:N_GROUPS + N_EXPERTS].set(w_re[l])
        wrh, wrl = _split_hi_lo(wr)
        br = jnp.zeros((1, ROUTER_LANES), F32).at[0, :N_GROUPS].set(b_rg[l]).at[0, N_GROUPS:N_GROUPS + N_EXPERTS].set(b_re[l])
        wo = w_out[l].astype(BF16)

        proj = functools.partial(_in_proj, norm_g=norm1_g[l], w=w_tok, wt=w_chn, gate_bias=gate_bias, splits=splits,
                                 out_dtypes=out_dtypes, splits_t=splits_t, out_dtypes_t=out_dtypes_t)
        cv_l, k_l, o_l, lru_l, qt_l, vt_l, gt_l = proj(h_lat, m_lat[0], m_lat[1], tm=tm_in, transposed=transposed)
        cv_c, k_c, o_c, lru_c, qt_c, vt_c, gt_c = proj(h_ctx, m_ctx[0], m_ctx[1], tm=tm_c, transposed=False)

        a_l = _conformer_conv(cv_l, conv_w[l], conv_b[l], conv_ln_g[l], conv_ln_b[l], tm_in)
        (mcf, mcb, mlf, mlb), (rcf, rcb, rlf, rlb) = _recurrent_mixers(
            (qt_c, k_c, vt_c, gt_c), (qt_l, k_l, vt_l, gt_l), lru_c, lru_l, lru_conv_w[l], lcb, wax, bax, lam, tb)

        h_lat, vx_l, cnt_l = _out_proj(a_l, mlf, mlb, o_l, rlf, rlb, lru_l, h_lat, m_lat[2], m_lat[3], m_lat[4],
                                       norm2_g[l], mlstm_norm_g[l], wo, wrh, wrl, br, tm, transposed)
        h_lat = _moe_layer(vx_l, cnt_l, experts, l, h_lat, m_lat[5], final_g, last, transposed)
        if not last:
            a_c = _conformer_conv(cv_c, conv_w[l], conv_b[l], conv_ln_g[l], conv_ln_b[l], tm_c)
            h_ctx, vx_c, cnt_c = _out_proj(a_c, mcf, mcb, o_c, rcf, rcb, lru_c, h_ctx, m_ctx[2], m_ctx[3], m_ctx[4],
                                           norm2_g[l], mlstm_norm_g[l], wo, wrh, wrl, br, tm_c, False)
            h_ctx = _moe_layer(vx_c, cnt_c, experts, l, h_ctx, m_ctx[5], final_g, False, False)
    return h_lat
```
